```python
import math
import jax
import jax.numpy as jnp
from jax import lax
import numpy as np

D_MODEL = 1024
BATCH = 32
SEQ = 256
DEPTH = 2
DEC_BATCH = 4
DEC_SEQ = 4096
PAST_LEN = 512

GRID_W = 64
MIX_WIDTH = D_MODEL
A_WIDTH = MIX_WIDTH // 2
B_WIDTH = MIX_WIDTH - A_WIDTH
DK = 128
DV = 128
H_A = A_WIDTH // DV
N_DIR = 2
CHUNK = 64
N_POOL_GROUPS = 4
POOL_GROUP = D_MODEL // N_POOL_GROUPS
POOL_WINDOWS = (2, 4, 8, 16)
D_FF = 2816
N_MOD = 6
N_EVEN = (DEPTH + 1) // 2
N_ODD = DEPTH // 2
EPS = 1e-6
DT_MIN = 0.001
DT_MAX = 0.1
QKV_END = 3 * A_WIDTH
GATE_END = QKV_END + A_WIDTH
BETA_END = GATE_END + N_DIR * H_A
ALPHA_END = BETA_END + N_DIR * H_A
BG_END = ALPHA_END + B_WIDTH
CG_END = BG_END + B_WIDTH
IN_AB = CG_END + B_WIDTH

kernel_name = "hybrid_deltanet_shortconv_pool_diffusion_step"


def rmsnorm(x, g):
    x32 = x.astype(jnp.float32)
    y = x32 * lax.rsqrt(jnp.mean(x32 * x32, axis=-1, keepdims=True) + EPS)
    return (y * g.astype(jnp.float32)).astype(x.dtype)


def l2norm(x):
    return x * lax.rsqrt(jnp.sum(x * x, axis=-1, keepdims=True) + EPS)


def dwconv3(x, w):
    xp = jnp.pad(x, ((0, 0), (1, 1), (0, 0)))
    return xp[:, :-2] * w[0] + xp[:, 1:-1] * w[1] + xp[:, 2:] * w[2]


def sincos_2d(rows, cols, d):
    quarter = d // 4
    omega = 1.0 / (10000.0 ** (jnp.arange(quarter, dtype=jnp.float32) / quarter))
    er = jnp.arange(rows, dtype=jnp.float32)[:, None] * omega[None, :]
    ec = jnp.arange(cols, dtype=jnp.float32)[:, None] * omega[None, :]
    er = jnp.concatenate([jnp.sin(er), jnp.cos(er)], axis=-1)
    ec = jnp.concatenate([jnp.sin(ec), jnp.cos(ec)], axis=-1)
    pe = jnp.concatenate([jnp.broadcast_to(er[:, None, :], (rows, cols, d // 2)),
                          jnp.broadcast_to(ec[None, :, :], (rows, cols, d // 2))], axis=-1)
    return pe.reshape(rows * cols, d)


def _to_chunks(t):
    b, l, h = t.shape[:3]
    t = t.reshape((b, l // CHUNK, CHUNK, h) + t.shape[3:])
    return jnp.moveaxis(t, 3, 1)


def chunk_gated_delta(q, k, v, g, beta, s0):
    f32 = jnp.float32
    b, l, h, dk = q.shape
    dv = v.shape[-1]
    q = _to_chunks(q.astype(f32) * (dk ** -0.5))
    k = _to_chunks(k.astype(f32))
    v = _to_chunks(v.astype(f32))
    beta = _to_chunks(beta.astype(f32))
    gc = jnp.cumsum(_to_chunks(g.astype(f32)), axis=-1)
    tri = jnp.tril(jnp.ones((CHUNK, CHUNK), dtype=bool))
    strict = jnp.tril(jnp.ones((CHUNK, CHUNK), dtype=bool), k=-1)
    decay = jnp.exp(jnp.where(tri, gc[..., :, None] - gc[..., None, :], -jnp.inf))
    kb = k * beta[..., None]
    vb = v * beta[..., None]
    m = jnp.where(strict, jnp.einsum("bhncd,bhnsd->bhncs", kb, k) * decay, 0.0)
    a = m + jnp.eye(CHUNK, dtype=f32)
    u = lax.linalg.triangular_solve(a, vb, left_side=True, lower=True, unit_diagonal=True)
    w = lax.linalg.triangular_solve(a, kb * jnp.exp(gc)[..., None], left_side=True,
                                    lower=True, unit_diagonal=True)
    attn = jnp.where(tri, jnp.einsum("bhncd,bhnsd->bhncs", q, k) * decay, 0.0)
    qg = q * jnp.exp(gc)[..., None]
    kg = k * jnp.exp(gc[..., -1:] - gc)[..., None]
    g_last = jnp.exp(gc[..., -1])
    xs = tuple(jnp.moveaxis(t, 2, 0) for t in (qg, kg, u, w, attn, g_last))

    def step(s, inp):
        qg_i, kg_i, u_i, w_i, attn_i, gl_i = inp
        v_new = u_i - jnp.einsum("bhcd,bhde->bhce", w_i, s)
        o_i = (jnp.einsum("bhcd,bhde->bhce", qg_i, s)
               + jnp.einsum("bhcs,bhse->bhce", attn_i, v_new))
        s = s * gl_i[..., None, None] + jnp.einsum("bhcd,bhce->bhde", kg_i, v_new)
        return s, o_i

    s_fin, o = lax.scan(step, s0.astype(f32), xs)
    o = jnp.transpose(o, (1, 0, 3, 2, 4)).reshape(b, l, h, dv)
    return o, s_fin


def mixer_ab(h, s0, w_in, conv_qkv, a_log, dt_bias, o_norm_g, conv_b, w_out):
    b, l, _ = h.shape
    p = h @ w_in
    qkv = jax.nn.silu(dwconv3(p[..., :QKV_END], conv_qkv))
    q = l2norm(qkv[..., :A_WIDTH].reshape(b, l, H_A, DK).astype(jnp.float32))
    k = l2norm(qkv[..., A_WIDTH:2 * A_WIDTH].reshape(b, l, H_A, DK).astype(jnp.float32))
    v = qkv[..., 2 * A_WIDTH:QKV_END].reshape(b, l, H_A, DV)
    gate = p[..., QKV_END:GATE_END].reshape(b, l, H_A, DV).astype(jnp.float32)
    beta = jax.nn.sigmoid(p[..., GATE_END:BETA_END].astype(jnp.float32)).reshape(b, l, N_DIR, H_A)
    alpha = p[..., BETA_END:ALPHA_END].astype(jnp.float32).reshape(b, l, N_DIR, H_A)
    g = -jnp.exp(a_log.astype(jnp.float32)) * jax.nn.softplus(alpha + dt_bias.astype(jnp.float32))
    o_f, s_f = chunk_gated_delta(q, k, v, g[:, :, 0], beta[:, :, 0], s0[:, 0])
    flip = lambda t: jnp.flip(t, axis=1)
    o_b, s_b = chunk_gated_delta(flip(q), flip(k), flip(v), flip(g[:, :, 1]),
                                 flip(beta[:, :, 1]), s0[:, 1])
    o = rmsnorm(o_f + flip(o_b), o_norm_g) * jax.nn.silu(gate)
    o_a = o.reshape(b, l, A_WIDTH).astype(h.dtype)
    bg = p[..., ALPHA_END:BG_END]
    cg = p[..., BG_END:CG_END]
    hx = p[..., CG_END:IN_AB]
    y_b = bg * dwconv3(cg * hx, conv_b)
    y = jnp.concatenate([o_a, y_b], axis=-1) @ w_out
    return y, jnp.stack([s_f, s_b], axis=1).astype(h.dtype)


def multiscale_pool(h, w_pool, scale):
    b, l, d = h.shape
    hg = h.reshape(b, l, N_POOL_GROUPS, POOL_GROUP).astype(jnp.float32)
    cs = jnp.concatenate([jnp.zeros((b, 1, N_POOL_GROUPS, POOL_GROUP), jnp.float32),
                          jnp.cumsum(hg, axis=1)], axis=1)
    t = jnp.arange(l)
    means = []
    for gi, win in enumerate(POOL_WINDOWS):
        lo = jnp.clip(t - win // 2, 0, l)
        hi = jnp.clip(t - win // 2 + win, 0, l)
        s = cs[:, hi, gi] - cs[:, lo, gi]
        means.append(s / (hi - lo).astype(jnp.float32)[None, :, None])
    mixed = (jnp.stack(means, axis=2) - hg).astype(h.dtype)
    y = jnp.einsum("blgc,gcd->blgd", mixed, w_pool).reshape(b, l, d)
    return y * scale


def conv_ffn(h, w_gate, w_up, conv, w_down):
    u = dwconv3(h @ w_gate, conv)
    return (jax.nn.silu(u) * (h @ w_up)) @ w_down


def setup_inputs(seed: int = 0) -> dict:
    key = jax.random.key(seed)
    ks = jax.random.split(key, 24)
    f32 = jnp.float32

    def nrm(k, shape, s):
        return jax.random.normal(k, shape, f32) * s

    dt = jnp.exp(jax.random.uniform(ks[9], (N_EVEN, N_DIR, H_A), f32)
                 * (math.log(DT_MAX) - math.log(DT_MIN)) + math.log(DT_MIN))
    dt_bias = dt + jnp.log(-jnp.expm1(-dt))
    a_log = jnp.log(jax.random.uniform(ks[10], (N_EVEN, N_DIR, H_A), f32, 1.0, 16.0))
    return {
        "x_prompt": nrm(ks[0], (BATCH, SEQ, D_MODEL), 1.0),
        "x_sample": nrm(ks[1], (DEC_BATCH, DEC_SEQ, D_MODEL), 1.0),
        "state_delta": nrm(ks[2], (DEC_BATCH, N_EVEN, N_DIR, H_A, DK, DV), 0.05),
        "c": nrm(ks[3], (DEC_BATCH, D_MODEL), 1.0),
        "c_ctx": nrm(ks[4], (D_MODEL,), 1.0),
        "norm_mix_g": 1.0 + nrm(ks[5], (DEPTH, D_MODEL), 0.05),
        "norm_ffn_g": 1.0 + nrm(ks[6], (DEPTH, D_MODEL), 0.05),
        "w_ada": nrm(ks[7], (DEPTH, D_MODEL, N_MOD * D_MODEL), 0.5 * D_MODEL ** -0.5),
        "b_ada": nrm(ks[8], (DEPTH, N_MOD * D_MODEL), 0.02),
        "w_in_ab": nrm(ks[11], (N_EVEN, D_MODEL, IN_AB), D_MODEL ** -0.5),
        "conv_qkv": nrm(ks[12], (N_EVEN, 3, QKV_END), 3 ** -0.5),
        "a_log": a_log,
        "dt_bias": dt_bias,
        "o_norm_g": 1.0 + nrm(ks[13], (N_EVEN, DV), 0.05),
        "conv_b": nrm(ks[14], (N_EVEN, 3, B_WIDTH), 3 ** -0.5),
        "w_out_ab": nrm(ks[15], (N_EVEN, MIX_WIDTH, D_MODEL), MIX_WIDTH ** -0.5),
        "w_pool": nrm(ks[16], (N_ODD, N_POOL_GROUPS, POOL_GROUP, POOL_GROUP), POOL_GROUP ** -0.5),
        "pool_scale": 1.0 + nrm(ks[17], (N_ODD, D_MODEL), 0.1),
        "w_ffn_gate": nrm(ks[18], (DEPTH, D_MODEL, D_FF), D_MODEL ** -0.5),
        "w_ffn_up": nrm(ks[19], (DEPTH, D_MODEL, D_FF), D_MODEL ** -0.5),
        "ffn_conv": nrm(ks[20], (DEPTH, 3, D_FF), 3 ** -0.5),
        "w_ffn_down": nrm(ks[21], (DEPTH, D_FF, D_MODEL), D_FF ** -0.5),
        "final_norm_g": 1.0 + nrm(ks[22], (D_MODEL,), 0.05),
    }


def reference(x_prompt, x_sample, state_delta, c, c_ctx, norm_mix_g, norm_ffn_g, w_ada, b_ada,
              w_in_ab, conv_qkv, a_log, dt_bias, o_norm_g, conv_b, w_out_ab, w_pool, pool_scale,
              w_ffn_gate, w_ffn_up, ffn_conv, w_ffn_down, final_norm_g):

    def trunk(x, cond, states, is_context):
        finals = []
        for layer in range(DEPTH):
            mod = jax.nn.silu(cond) @ w_ada[layer] + b_ada[layer]
            shift1, scale1, gate1, shift2, scale2, gate2 = jnp.split(mod[:, None, :], N_MOD, axis=-1)
            h = rmsnorm(x, norm_mix_g[layer]) * (1 + scale1) + shift1
            if layer % 2 == 0:
                e = layer // 2
                y, s_fin = mixer_ab(h, states[:, e], w_in_ab[e], conv_qkv[e], a_log[e],
                                    dt_bias[e], o_norm_g[e], conv_b[e], w_out_ab[e])
                if is_context:
                    finals.append(s_fin)
            else:
                o = layer // 2
                y = multiscale_pool(h, w_pool[o], pool_scale[o])
            x = x + gate1 * y
            h = rmsnorm(x, norm_ffn_g[layer]) * (1 + scale2) + shift2
            x = x + gate2 * conv_ffn(h, w_ffn_gate[layer], w_ffn_up[layer], ffn_conv[layer],
                                     w_ffn_down[layer])
        return rmsnorm(x, final_norm_g), finals

    zero_states = jnp.zeros((x_prompt.shape[0], N_EVEN, N_DIR, H_A, DK, DV), x_prompt.dtype)
    y_prompt, finals = trunk(x_prompt, c_ctx[None, :], zero_states, True)
    new_state_delta = jnp.stack(finals, axis=1)

    rows = x_sample.shape[1] // GRID_W
    x_lat = x_sample + sincos_2d(rows, GRID_W, D_MODEL).astype(x_sample.dtype)[None]
    y_sample, _ = trunk(x_lat, c, state_delta, False)

    return (y_prompt, y_sample, new_state_delta)
```

```python
import functools

import jax
import jax.numpy as jnp
from jax import lax
from jax.experimental import pallas as pl
from jax.experimental.pallas import tpu as pltpu

F32 = jnp.float32
BF16 = jnp.bfloat16

D_MODEL = 1024
GRID_W = 64
A_WIDTH = 512
B_WIDTH = 512
DK = 128
DV = 128
H_A = 4
N_DIR = 2
N_POOL_GROUPS = 4
POOL_GROUP = D_MODEL // N_POOL_GROUPS
POOL_WINDOWS = (2, 4, 8, 16)
D_FF = 2816
N_MOD = 6
EPS = 1e-6
QKV_END = 3 * A_WIDTH
GATE_END = QKV_END + A_WIDTH
BETA_END = GATE_END + N_DIR * H_A
ALPHA_END = BETA_END + N_DIR * H_A
BG_END = ALPHA_END + B_WIDTH
CG_END = BG_END + B_WIDTH
IN_AB = CG_END + B_WIDTH

HALO = 8
CHUNK = 128
FF_CHUNK = 256
N_FF_CHUNKS = D_FF // FF_CHUNK
N_SMALL = 2 * N_DIR * H_A
SEQ_GROUP = 4
VMEM_LIMIT = 56 * 1024 * 1024
NEG_BIG = -1e30


def _sigmoid(x):
    return 1.0 / (1.0 + jnp.exp(-x))


def _silu(x):
    return x * _sigmoid(x)


def _softplus(x):
    return jnp.maximum(x, 0.0) + jnp.log1p(jnp.exp(-jnp.abs(x)))


def _dot(a, b):
    return jnp.dot(a, b, preferred_element_type=F32)


def _dot_nt(a, b):
    return lax.dot_general(a, b, (((1,), (1,)), ((), ())), preferred_element_type=F32)


def _dot_tn(a, b):
    return lax.dot_general(a, b, (((0,), (0,)), ((), ())), preferred_element_type=F32)


def _split3(a):
    a1 = a.astype(BF16)
    r1 = a - a1.astype(F32)
    a2 = r1.astype(BF16)
    a3 = (r1 - a2.astype(F32)).astype(BF16)
    return a1, a2, a3


def _norm_mod(x, g, scale, shift):
    ms = jnp.mean(x * x, axis=-1, keepdims=True)
    return (x * lax.rsqrt(ms + EPS)) * g * (1.0 + scale) + shift


def _shift_rows(a, k):
    n = a.shape[0]
    return pltpu.roll(a, (-k) % n, axis=0)


def _dwconv3_ext(p_ext, w, tm):
    prev = _shift_rows(p_ext, -1)[HALO:HALO + tm]
    cur = p_ext[HALO:HALO + tm]
    nxt = _shift_rows(p_ext, 1)[HALO:HALO + tm]
    return prev * w[0:1] + cur * w[1:2] + nxt * w[2:3]


def _halo_valid(tm, seq_len):
    i = pl.program_id(0)
    vp = jnp.where((i * tm) % seq_len != 0, 1.0, 0.0).astype(F32)
    vn = jnp.where(((i + 1) * tm) % seq_len != 0, 1.0, 0.0).astype(F32)
    return vp, vn


def _const_spec(shape):
    nd = len(shape)
    return pl.BlockSpec(shape, lambda *_: (0,) * nd)


def _tile_specs(tm, width, n_rows):
    per = tm // HALO
    last = n_rows // HALO - 1
    prev = pl.BlockSpec((HALO, width), lambda i: (jnp.maximum(i * per - 1, 0), 0))
    main = pl.BlockSpec((tm, width), lambda i: (i, 0))
    nxt = pl.BlockSpec((HALO, width), lambda i: (jnp.minimum((i + 1) * per, last), 0))
    return [prev, main, nxt]


def _pe_specs(tm, width, seq_len):
    per = tm // HALO
    tiles = seq_len // tm
    last = seq_len // HALO - 1
    prev = pl.BlockSpec((HALO, width), lambda i: (jnp.maximum((i % tiles) * per - 1, 0), 0))
    main = pl.BlockSpec((tm, width), lambda i: (i % tiles, 0))
    nxt = pl.BlockSpec((HALO, width), lambda i: (jnp.minimum((i % tiles + 1) * per, last), 0))
    return [prev, main, nxt]


def _mod_spec(tm, seq_len, row_base, row_stride):
    return pl.BlockSpec((1, 1, N_MOD * D_MODEL),
                        lambda i: (row_base + ((i * tm) // seq_len) * row_stride, 0, 0))


def _params(n_axes=1, semantics=None):
    return pltpu.CompilerParams(
        dimension_semantics=semantics or ("parallel",) * n_axes,
        vmem_limit_bytes=VMEM_LIMIT)


def _mod_kernel(c_ref, w_ref, b_ref, o_ref):
    c = c_ref[...]
    s1, s2, _ = _split3(_silu(c))
    w = w_ref[0]
    w1 = w.astype(BF16)
    w2 = (w - w1.astype(F32)).astype(BF16)
    o_ref[0] = _dot(s1, w1) + (_dot(s2, w1) + _dot(s1, w2)) + b_ref[0]


def _modulation(cond, w_ada, b_ada):
    depth, _, width = w_ada.shape
    rows = cond.shape[0]
    tn = 1536
    return pl.pallas_call(
        _mod_kernel,
        grid=(depth, width // tn),
        in_specs=[pl.BlockSpec((rows, D_MODEL), lambda l, j: (0, 0)),
                  pl.BlockSpec((1, D_MODEL, tn), lambda l, j: (l, 0, j)),
                  pl.BlockSpec((1, 1, tn), lambda l, j: (l, 0, j))],
        out_specs=pl.BlockSpec((1, rows, tn), lambda l, j: (l, 0, j)),
        out_shape=jax.ShapeDtypeStruct((depth, rows, width), F32),
        compiler_params=_params(2),
        name="modulation",
    )(cond, w_ada, b_ada.reshape(depth, 1, width))


def _inproj_kernel(*refs, tm, seq_len, has_pe):
    if has_pe:
        xp, xm, xn, pp, pm, pn = refs[:6]
        refs = refs[6:]
    else:
        xp, xm, xn = refs[:3]
        pp = pm = pn = None
        refs = refs[3:]
    (mod_ref, ng_ref, wqkv_ref, wgate_ref, wb_ref, wsc_ref, wsr_ref, cq_ref, cb_ref,
     pc_ref, pr_ref, q_ref, k_ref, v_ref, gate_ref, gbc_ref, gbr_ref, yb_ref) = refs

    vp, vn = _halo_valid(tm, seq_len)
    mod = mod_ref[0]
    shift = mod[:, 0:D_MODEL]
    scale = mod[:, D_MODEL:2 * D_MODEL]
    g = ng_ref[...]

    def prep(x_ref, pe_ref):
        x = x_ref[...]
        if has_pe:
            x = x + pe_ref[...]
        return _norm_mod(x, g, scale, shift)

    h_main = prep(xm, pm)
    h_ext = jnp.concatenate([prep(xp, pp) * vp, h_main, prep(xn, pn) * vn], axis=0)
    hb_ext = h_ext.astype(BF16)
    hb = h_main.astype(BF16)

    cq = cq_ref[...]
    outs = (q_ref, k_ref, v_ref)
    for part in range(3):
        cols = slice(part * A_WIDTH, (part + 1) * A_WIDTH)
        p = _dot(hb_ext, wqkv_ref[:, cols])
        a = _silu(_dwconv3_ext(p, cq[:, cols], tm))
        if part < 2:
            heads = []
            for h in range(H_A):
                ah = a[:, h * DK:(h + 1) * DK]
                ss = jnp.sum(ah * ah, axis=-1, keepdims=True)
                nrm = lax.rsqrt(ss + EPS)
                if part == 0:
                    nrm = nrm * (DK ** -0.5)
                heads.append(ah * nrm)
            a = jnp.concatenate(heads, axis=1)
        outs[part][...] = a

    gate_ref[...] = _dot(hb, wgate_ref[...])

    pc = pc_ref[...]
    ba = _dot(hb, wsc_ref[...])
    lane = lax.broadcasted_iota(jnp.int32, ba.shape, 1)
    gbc_ref[...] = jnp.where(lane < N_DIR * H_A, _sigmoid(ba),
                             -pc[0:1] * _softplus(ba + pc[1:2]))
    pr = pr_ref[...]
    bar = _dot_nt(wsr_ref[...], hb)
    sub = lax.broadcasted_iota(jnp.int32, bar.shape, 0)
    gbr_ref[...] = jnp.where(sub < N_DIR * H_A, _sigmoid(bar),
                             -pr[:, 0:1] * _softplus(bar + pr[:, 1:2]))

    pb = _dot(hb_ext, wb_ref[...])
    bg = pb[HALO:HALO + tm, 0:B_WIDTH]
    cghx = pb[:, B_WIDTH:2 * B_WIDTH] * pb[:, 2 * B_WIDTH:3 * B_WIDTH]
    yb_ref[...] = bg * _dwconv3_ext(cghx, cb_ref[...], tm)


def _inproj(x, pe, mod3, row_base, row_stride, seq_len, tm, ng, wts):
    n_rows = x.shape[0]
    has_pe = pe is not None
    in_specs = _tile_specs(tm, D_MODEL, n_rows)
    args = [x, x, x]
    if has_pe:
        in_specs += _pe_specs(tm, D_MODEL, seq_len)
        args += [pe, pe, pe]
    in_specs += [_mod_spec(tm, seq_len, row_base, row_stride), _const_spec((1, D_MODEL)),
                 _const_spec((D_MODEL, QKV_END)), _const_spec((D_MODEL, A_WIDTH)),
                 _const_spec((D_MODEL, 3 * B_WIDTH)), _const_spec((D_MODEL, N_SMALL)),
                 _const_spec((N_SMALL, D_MODEL)), _const_spec((3, QKV_END)),
                 _const_spec((3, B_WIDTH)), _const_spec((2, N_SMALL)), _const_spec((N_SMALL, 2))]
    args += [mod3, ng, wts["w_qkv"], wts["w_gate"], wts["w_b"], wts["w_small_c"], wts["w_small_r"],
             wts["conv_qkv"], wts["conv_b"], wts["p_col"], wts["p_row"]]
    wide = pl.BlockSpec((tm, A_WIDTH), lambda i: (i, 0))
    out_specs = [wide, wide, wide, wide,
                 pl.BlockSpec((tm, N_SMALL), lambda i: (i, 0)),
                 pl.BlockSpec((N_SMALL, tm), lambda i: (0, i)),
                 wide]
    wide_shape = jax.ShapeDtypeStruct((n_rows, A_WIDTH), F32)
    out_shape = [wide_shape] * 4 + [jax.ShapeDtypeStruct((n_rows, N_SMALL), F32),
                                    jax.ShapeDtypeStruct((N_SMALL, n_rows), F32), wide_shape]
    return pl.pallas_call(
        functools.partial(_inproj_kernel, tm=tm, seq_len=seq_len, has_pe=has_pe),
        grid=(n_rows // tm,), in_specs=in_specs, out_specs=out_specs, out_shape=out_shape,
        compiler_params=_params(1), name="inproj",
    )(*args)


def _dot3_right(a_exact, b):
    b1, b2, b3 = _split3(b)
    return _dot(a_exact, b1) + (_dot(a_exact, b2) + _dot(a_exact, b3))


def _dot3_left(a, b_exact):
    a1, a2, a3 = _split3(a)
    return _dot(a1, b_exact) + (_dot(a2, b_exact) + _dot(a3, b_exact))


def _chunk_local_kernel(q_ref, k_ref, v_ref, gbc_ref, gbr_ref,
                        u_ref, w_ref, qg_ref, kg_ref, at_ref, gl_ref):
    c = CHUNK
    n_beta = N_DIR * H_A
    r1 = lax.broadcasted_iota(jnp.int32, (c, c), 0)
    c1 = lax.broadcasted_iota(jnp.int32, (c, c), 1)
    tri_lo = jnp.where(r1 >= c1, 1.0, 0.0).astype(BF16)
    tri_up = jnp.where(r1 <= c1, 1.0, 0.0).astype(BF16)
    gbc = gbc_ref[...]
    gbr = gbr_ref[...]

    pre_c = _dot3_right(tri_lo, gbc)
    suf_c = _dot3_right(tri_up, gbc)
    lane = lax.broadcasted_iota(jnp.int32, gbc.shape, 1)
    fwd_c = lane < n_beta + H_A
    gc_c = jnp.where(fwd_c, pre_c, suf_c)
    rest_c = jnp.where(fwd_c, suf_c, pre_c) - gbc
    eg_c = jnp.exp(gc_c)
    er_c = jnp.exp(rest_c)
    pre_r = _dot3_left(gbr, tri_up)
    suf_r = _dot3_left(gbr, tri_lo)
    sub = lax.broadcasted_iota(jnp.int32, gbr.shape, 0)
    gc_r = jnp.where(sub < n_beta + H_A, pre_r, suf_r)
    total = jnp.sum(gbr, axis=1, keepdims=True)
    gl_ref[0] = jnp.broadcast_to(jnp.exp(total), (2 * n_beta, c))[n_beta:, :]

    r2 = lax.broadcasted_iota(jnp.int32, (2 * c, 2 * c), 0)
    c2 = lax.broadcasted_iota(jnp.int32, (2 * c, 2 * c), 1)
    x2 = r2 ^ c2
    same_head = x2 < c
    eye = jnp.where(r2 == c2, 1.0, 0.0).astype(F32)

    def col_pair(a, j):
        return jnp.concatenate([a[:, j:j + 1], a[:, j + 1:j + 2]], axis=0)

    def row_pair(a, j):
        return jnp.concatenate([a[j:j + 1, :], a[j + 1:j + 2, :]], axis=1)

    q = q_ref[...]
    k = k_ref[...]
    v = v_ref[...]
    for hp in range(H_A // 2):
        h0 = 2 * hp

        def stack(a):
            return jnp.concatenate([a[:, h0 * DK:(h0 + 1) * DK],
                                    a[:, (h0 + 1) * DK:(h0 + 2) * DK]], axis=0)

        q2, k2, v2 = stack(q), stack(k), stack(v)
        k2b = k2.astype(BF16)
        kk = _dot_nt(k2b, k2b)
        qk = _dot_nt(q2.astype(BF16), k2b)
        for d in range(N_DIR):
            jb = d * H_A + h0
            jg = n_beta + jb
            gcc = col_pair(gc_c, jg)
            egc = col_pair(eg_c, jg)
            erc = col_pair(er_c, jg)
            bc = col_pair(gbc, jb)
            gcr = row_pair(gc_r, jg)
            tri = (r2 >= c2) if d == 0 else (r2 <= c2)
            decay = jnp.exp(jnp.where(same_head & tri, gcc - gcr, NEG_BIG))
            attn = qk * decay
            m = jnp.where(r2 != c2, (bc * kk) * decay, 0.0)

            t = eye - jnp.where(x2 == 1, m, 0.0)
            for lvl in range(1, 7):
                off = jnp.where((x2 >> lvl) == 1, m, 0.0).astype(BF16)
                tb = t.astype(BF16)
                t = t - _dot(tb, _dot(off, tb).astype(BF16))

            rhs = jnp.concatenate([v2 * bc, k2 * (bc * egc)], axis=1).astype(BF16)
            uw = _dot(t.astype(BF16), rhs)
            qg = q2 * egc
            kg = k2 * erc
            for hh in range(2):
                rows = slice(hh * c, (hh + 1) * c)
                cols = slice((h0 + hh) * DK, (h0 + hh + 1) * DK)
                u_ref[d, :, cols] = uw[rows, 0:DV]
                w_ref[d, :, cols] = uw[rows, DV:2 * DV].astype(BF16)
                qg_ref[d, :, cols] = qg[rows].astype(BF16)
                kg_ref[d, :, cols] = kg[rows].astype(BF16)
                at_ref[d, :, cols] = attn[rows, hh * c:(hh + 1) * c].astype(BF16)


def _chunk_local(q, k, v, gbc, gbr):
    n_rows = q.shape[0]
    n_chunks = n_rows // CHUNK
    wide = pl.BlockSpec((CHUNK, A_WIDTH), lambda i: (i, 0))
    out_wide = pl.BlockSpec((N_DIR, CHUNK, A_WIDTH), lambda i: (0, i, 0))
    big = lambda dt: jax.ShapeDtypeStruct((N_DIR, n_rows, A_WIDTH), dt)
    return pl.pallas_call(
        _chunk_local_kernel,
        grid=(n_chunks,),
        in_specs=[wide, wide, wide,
                  pl.BlockSpec((CHUNK, N_SMALL), lambda i: (i, 0)),
                  pl.BlockSpec((N_SMALL, CHUNK), lambda i: (0, i))],
        out_specs=[out_wide] * 5 + [pl.BlockSpec((1, N_DIR * H_A, CHUNK), lambda i: (i, 0, 0))],
        out_shape=[big(F32), big(BF16), big(BF16), big(BF16), big(BF16),
                   jax.ShapeDtypeStruct((n_chunks, N_DIR * H_A, CHUNK), F32)],
        compiler_params=_params(1), name="chunk_local",
    )(q, k, v, gbc, gbr)


def _scan_kernel(*refs, n_chunks, zero_init, write_final):
    fwd = refs[0:6]
    bwd = refs[6:12]
    refs = refs[12:]
    if not zero_init:
        s0_ref = refs[0]
        refs = refs[1:]
    of_ref, ob_ref = refs[0:2]
    refs = refs[2:]
    if write_final:
        sfin_ref = refs[0]
        refs = refs[1:]
    s_scr = refs[0]
    i = pl.program_id(1)

    @pl.when(i == 0)
    def _():
        if zero_init:
            s_scr[...] = jnp.zeros(s_scr.shape, F32)
        else:
            s_scr[...] = s0_ref[...]

    for s in range(SEQ_GROUP):
        for d in range(N_DIR):
            u_ref, w_ref, qg_ref, kg_ref, at_ref, gl_ref = fwd if d == 0 else bwd
            o_ref = of_ref if d == 0 else ob_ref
            for h in range(H_A):
                cols = slice(h * DV, (h + 1) * DV)
                st = s_scr[s, d, h]
                sb = st.astype(BF16)
                v_new = u_ref[0, s, :, cols] - _dot(w_ref[0, s, :, cols], sb)
                vb = v_new.astype(BF16)
                o_ref[s, :, cols] = _dot(qg_ref[0, s, :, cols], sb) + _dot(at_ref[0, s, :, cols], vb)
                gl = gl_ref[s, 0, d * H_A + h:d * H_A + h + 1, :]
                s_scr[s, d, h] = st * gl + _dot_tn(kg_ref[0, s, :, cols], vb)

    if write_final:
        @pl.when(i == n_chunks - 1)
        def _():
            sfin_ref[...] = s_scr[...]


def _scan(local, s0, n_seq, seq_len, write_final):
    u, w, qg, kg, at, gl = local
    n_chunks = seq_len // CHUNK
    zero_init = s0 is None
    shp4 = (N_DIR, n_seq, seq_len, A_WIDTH)
    arrs = [a.reshape(shp4) for a in (u, w, qg, kg, at)]
    gl4 = gl.reshape(n_seq, n_chunks, N_DIR * H_A, CHUNK)
    sg = SEQ_GROUP

    def specs(d, chunk_of):
        big = pl.BlockSpec((1, sg, CHUNK, A_WIDTH), lambda g, i: (d, g, chunk_of(i), 0))
        return [big] * 5 + [pl.BlockSpec((sg, 1, N_DIR * H_A, CHUNK),
                                         lambda g, i: (g, chunk_of(i), 0, 0))]

    rev = lambda i: n_chunks - 1 - i
    in_specs = specs(0, lambda i: i) + specs(1, rev)
    args = arrs + [gl4] + arrs + [gl4]
    state_spec = pl.BlockSpec((sg, N_DIR, H_A, DK, DV), lambda g, i: (g, 0, 0, 0, 0))
    if not zero_init:
        in_specs.append(state_spec)
        args.append(s0)
    o_shape = jax.ShapeDtypeStruct((n_seq, seq_len, A_WIDTH), F32)
    out_specs = [pl.BlockSpec((sg, CHUNK, A_WIDTH), lambda g, i: (g, i, 0)),
                 pl.BlockSpec((sg, CHUNK, A_WIDTH), lambda g, i: (g, rev(i), 0))]
    out_shape = [o_shape, o_shape]
    if write_final:
        out_specs.append(state_spec)
        out_shape.append(jax.ShapeDtypeStruct((n_seq, N_DIR, H_A, DK, DV), F32))
    return pl.pallas_call(
        functools.partial(_scan_kernel, n_chunks=n_chunks, zero_init=zero_init,
                          write_final=write_final),
        grid=(n_seq // sg, n_chunks), in_specs=in_specs, out_specs=out_specs, out_shape=out_shape,
        scratch_shapes=[pltpu.VMEM((sg, N_DIR, H_A, DK, DV), F32)],
        compiler_params=_params(2, ("parallel", "arbitrary")), name="delta_scan",
    )(*args)


def _outproj_kernel(*refs, has_pe):
    if has_pe:
        x_ref, pe_ref = refs[:2]
        refs = refs[2:]
    else:
        x_ref = refs[0]
        refs = refs[1:]
    of_ref, ob_ref, gate_ref, yb_ref, mod_ref, og_ref, wout_ref, out_ref = refs
    o = of_ref[...] + ob_ref[...]
    sg = _silu(gate_ref[...])
    og = og_ref[...]
    parts = []
    for h in range(H_A):
        cols = slice(h * DV, (h + 1) * DV)
        oh = o[:, cols]
        ms = jnp.mean(oh * oh, axis=-1, keepdims=True)
        parts.append((oh * lax.rsqrt(ms + EPS)) * og * sg[:, cols])
    a = jnp.concatenate(parts + [yb_ref[...]], axis=1).astype(BF16)
    y = _dot(a, wout_ref[...])
    x = x_ref[...]
    if has_pe:
        x = x + pe_ref[...]
    gate1 = mod_ref[0][:, 2 * D_MODEL:3 * D_MODEL]
    out_ref[...] = x + gate1 * y


def _outproj(x, pe, o_f, o_b, gate, yb, mod3, row_base, row_stride, seq_len, tm, og, w_out):
    n_rows = x.shape[0]
    has_pe = pe is not None
    full = pl.BlockSpec((tm, D_MODEL), lambda i: (i, 0))
    half = pl.BlockSpec((tm, A_WIDTH), lambda i: (i, 0))
    in_specs = [full]
    args = [x]
    if has_pe:
        tiles = seq_len // tm
        in_specs.append(pl.BlockSpec((tm, D_MODEL), lambda i: (i % tiles, 0)))
        args.append(pe)
    in_specs += [half, half, half, half, _mod_spec(tm, seq_len, row_base, row_stride),
                 _const_spec((1, DV)), _const_spec((D_MODEL, D_MODEL))]
    args += [o_f, o_b, gate, yb, mod3, og, w_out]
    return pl.pallas_call(
        functools.partial(_outproj_kernel, has_pe=has_pe),
        grid=(n_rows // tm,), in_specs=in_specs, out_specs=full,
        out_shape=jax.ShapeDtypeStruct((n_rows, D_MODEL), F32),
        compiler_params=_params(1), name="outproj",
    )(*args)


def _ffn_kernel(*refs, tm, seq_len, final_norm):
    xp, xm, xn, mod_ref, ng_ref, wg_ref, wu_ref, wd_ref, cf_ref = refs[:9]
    refs = refs[9:]
    if final_norm:
        fg_ref = refs[0]
        refs = refs[1:]
    out_ref = refs[0]
    vp, vn = _halo_valid(tm, seq_len)
    mod = mod_ref[0]
    shift = mod[:, 3 * D_MODEL:4 * D_MODEL]
    scale = mod[:, 4 * D_MODEL:5 * D_MODEL]
    gate2 = mod[:, 5 * D_MODEL:6 * D_MODEL]
    g = ng_ref[...]
    x = xm[...]
    h_main = _norm_mod(x, g, scale, shift)
    h_ext = jnp.concatenate([_norm_mod(xp[...], g, scale, shift) * vp, h_main,
                             _norm_mod(xn[...], g, scale, shift) * vn], axis=0)
    hb_ext = h_ext.astype(BF16)
    hb = h_main.astype(BF16)
    acc = jnp.zeros((tm, D_MODEL), F32)
    for j in range(N_FF_CHUNKS):
        u = _dwconv3_ext(_dot(hb_ext, wg_ref[j]), cf_ref[j], tm)
        a = (_silu(u) * _dot(hb, wu_ref[j])).astype(BF16)
        acc = acc + _dot(a, wd_ref[j])
    y = x + gate2 * acc
    if final_norm:
        ms = jnp.mean(y * y, axis=-1, keepdims=True)
        y = (y * lax.rsqrt(ms + EPS)) * fg_ref[...]
    out_ref[...] = y


def _ffn(x, mod3, row_base, row_stride, seq_len, tm, ng, wts, final_g):
    n_rows = x.shape[0]
    final_norm = final_g is not None
    in_specs = _tile_specs(tm, D_MODEL, n_rows)
    in_specs += [_mod_spec(tm, seq_len, row_base, row_stride), _const_spec((1, D_MODEL)),
                 _const_spec((N_FF_CHUNKS, D_MODEL, FF_CHUNK)),
                 _const_spec((N_FF_CHUNKS, D_MODEL, FF_CHUNK)),
                 _const_spec((N_FF_CHUNKS, FF_CHUNK, D_MODEL)),
                 _const_spec((N_FF_CHUNKS, 3, FF_CHUNK))]
    args = [x, x, x, mod3, ng, wts["w_gate"], wts["w_up"], wts["w_down"], wts["conv"]]
    if final_norm:
        in_specs.append(_const_spec((1, D_MODEL)))
        args.append(final_g)
    return pl.pallas_call(
        functools.partial(_ffn_kernel, tm=tm, seq_len=seq_len, final_norm=final_norm),
        grid=(n_rows // tm,), in_specs=in_specs,
        out_specs=pl.BlockSpec((tm, D_MODEL), lambda i: (i, 0)),
        out_shape=jax.ShapeDtypeStruct((n_rows, D_MODEL), F32),
        compiler_params=_params(1), name="conv_ffn",
    )(*args)


def _pool_kernel(xp, xm, xn, mod_ref, ng_ref, wp_ref, ps_ref, out_ref, *, tm, seq_len):
    vp, vn = _halo_valid(tm, seq_len)
    mod = mod_ref[0]
    shift = mod[:, 0:D_MODEL]
    scale = mod[:, D_MODEL:2 * D_MODEL]
    gate1 = mod[:, 2 * D_MODEL:3 * D_MODEL]
    g = ng_ref[...]
    x = xm[...]
    h_main = _norm_mod(x, g, scale, shift)
    h_ext = jnp.concatenate([_norm_mod(xp[...], g, scale, shift) * vp, h_main,
                             _norm_mod(xn[...], g, scale, shift) * vn], axis=0)
    i = pl.program_id(0)
    pos = (i * tm + lax.broadcasted_iota(jnp.int32, (tm, POOL_GROUP), 0)) % seq_len
    ys = []
    for gi, win in enumerate(POOL_WINDOWS):
        cols = slice(gi * POOL_GROUP, (gi + 1) * POOL_GROUP)
        s = h_ext[:, cols]
        width = 1
        while width < win:
            s = s + _shift_rows(s, width)
            width *= 2
        start = HALO - win // 2
        if start:
            s = _shift_rows(s, start)
        s = s[0:tm]
        lo = jnp.maximum(pos - win // 2, 0)
        hi = jnp.minimum(pos - win // 2 + win, seq_len)
        mixed = (s / (hi - lo).astype(F32) - h_main[:, cols]).astype(BF16)
        ys.append(_dot(mixed, wp_ref[gi]))
    y = jnp.concatenate(ys, axis=1) * ps_ref[...]
    out_ref[...] = x + gate1 * y


def _pool(x, mod3, row_base, row_stride, seq_len, tm, ng, w_pool, pool_scale):
    n_rows = x.shape[0]
    in_specs = _tile_specs(tm, D_MODEL, n_rows)
    in_specs += [_mod_spec(tm, seq_len, row_base, row_stride), _const_spec((1, D_MODEL)),
                 _const_spec((N_POOL_GROUPS, POOL_GROUP, POOL_GROUP)), _const_spec((1, D_MODEL))]
    return pl.pallas_call(
        functools.partial(_pool_kernel, tm=tm, seq_len=seq_len),
        grid=(n_rows // tm,), in_specs=in_specs,
        out_specs=pl.BlockSpec((tm, D_MODEL), lambda i: (i, 0)),
        out_shape=jax.ShapeDtypeStruct((n_rows, D_MODEL), F32),
        compiler_params=_params(1), name="pool_mixer",
    )(x, x, x, mod3, ng, w_pool, pool_scale)


def _sincos_2d(rows, cols, d):
    quarter = d // 4
    omega = 1.0 / (10000.0 ** (jnp.arange(quarter, dtype=F32) / quarter))
    er = jnp.arange(rows, dtype=F32)[:, None] * omega[None, :]
    ec = jnp.arange(cols, dtype=F32)[:, None] * omega[None, :]
    er = jnp.concatenate([jnp.sin(er), jnp.cos(er)], axis=-1)
    ec = jnp.concatenate([jnp.sin(ec), jnp.cos(ec)], axis=-1)
    pe = jnp.concatenate([jnp.broadcast_to(er[:, None, :], (rows, cols, d // 2)),
                          jnp.broadcast_to(ec[None, :, :], (rows, cols, d // 2))], axis=-1)
    return pe.reshape(rows * cols, d)


def _even_layer_weights(w_in, conv_qkv, a_log, dt_bias, conv_b, w_out):
    small = w_in[:, GATE_END:ALPHA_END]
    zeros = jnp.zeros((N_DIR * H_A,), F32)
    ea = jnp.concatenate([zeros, jnp.exp(a_log.astype(F32)).reshape(-1)])
    dtb = jnp.concatenate([zeros, dt_bias.astype(F32).reshape(-1)])
    p_col = jnp.stack([ea, dtb], axis=0)
    return {
        "w_qkv": w_in[:, :QKV_END].astype(BF16),
        "w_gate": w_in[:, QKV_END:GATE_END].astype(BF16),
        "w_b": w_in[:, ALPHA_END:IN_AB].astype(BF16),
        "w_small_c": small.astype(BF16),
        "w_small_r": small.T.astype(BF16),
        "conv_qkv": conv_qkv, "conv_b": conv_b,
        "p_col": p_col, "p_row": p_col.T,
        "w_out": w_out.astype(BF16),
    }


def _ffn_weights(w_gate, w_up, conv, w_down):
    def slabs(w):
        return w.reshape(D_MODEL, N_FF_CHUNKS, FF_CHUNK).transpose(1, 0, 2).astype(BF16)
    return {
        "w_gate": slabs(w_gate), "w_up": slabs(w_up),
        "w_down": w_down.reshape(N_FF_CHUNKS, FF_CHUNK, D_MODEL).astype(BF16),
        "conv": conv.reshape(3, N_FF_CHUNKS, FF_CHUNK).transpose(1, 0, 2),
    }


def _trunk(x, pe, mod, row_base, row_stride, n_seq, seq_len, tm, s0, write_final, p):
    row = lambda a: a.reshape(1, -1)
    finals = None
    depth = mod.shape[0]
    for layer in range(depth):
        mod3 = mod[layer][:, None, :]
        place = (mod3, row_base, row_stride, seq_len, tm)
        if layer % 2 == 0:
            e = layer // 2
            wts = p["even"][e]
            q, k, v, gate, gbc, gbr, yb = _inproj(x, pe, *place, row(p["norm_mix_g"][layer]), wts)
            local = _chunk_local(q, k, v, gbc, gbr)
            res = _scan(local, None if s0 is None else s0[:, e], n_seq, seq_len, write_final)
            o_f = res[0].reshape(-1, A_WIDTH)
            o_b = res[1].reshape(-1, A_WIDTH)
            if write_final:
                finals = res[2]
            x = _outproj(x, pe, o_f, o_b, gate, yb, *place, row(p["o_norm_g"][e]), wts["w_out"])
        else:
            o = layer // 2
            x = _pool(x, *place, row(p["norm_mix_g"][layer]), p["w_pool"][o],
                      row(p["pool_scale"][o]))
        final_g = row(p["final_norm_g"]) if layer == depth - 1 else None
        x = _ffn(x, *place, row(p["norm_ffn_g"][layer]), p["ffn"][layer], final_g)
    return x, finals


def kernel(x_prompt, x_sample, state_delta, c, c_ctx, norm_mix_g, norm_ffn_g, w_ada, b_ada,
           w_in_ab, conv_qkv, a_log, dt_bias, o_norm_g, conv_b, w_out_ab, w_pool, pool_scale,
           w_ffn_gate, w_ffn_up, ffn_conv, w_ffn_down, final_norm_g):
    batch, seq, d = x_prompt.shape
    dec_batch, dec_seq, _ = x_sample.shape
    depth = w_ada.shape[0]
    n_even = w_in_ab.shape[0]
    assert d == D_MODEL and depth == 2 and n_even == 1

    p = {
        "norm_mix_g": norm_mix_g, "norm_ffn_g": norm_ffn_g, "o_norm_g": o_norm_g,
        "pool_scale": pool_scale, "final_norm_g": final_norm_g,
        "w_pool": w_pool.astype(BF16),
        "even": [_even_layer_weights(w_in_ab[e], conv_qkv[e], a_log[e], dt_bias[e], conv_b[e],
                                     w_out_ab[e]) for e in range(n_even)],
        "ffn": [_ffn_weights(w_ffn_gate[l], w_ffn_up[l], ffn_conv[l], w_ffn_down[l])
                for l in range(depth)],
    }

    n_cond = 1 + dec_batch
    pad = (-n_cond) % HALO
    cond = jnp.concatenate([c_ctx[None, :], c, jnp.zeros((pad, d), F32)], axis=0)
    mod = _modulation(cond, w_ada, b_ada)

    y_prompt, finals = _trunk(x_prompt.reshape(batch * seq, d), None, mod, 0, 0, batch, seq,
                              min(seq, 256), None, True, p)
    pe = _sincos_2d(dec_seq // GRID_W, GRID_W, d)
    y_sample, _ = _trunk(x_sample.reshape(dec_batch * dec_seq, d), pe, mod, 1, 1, dec_batch,
                         dec_seq, 512, state_delta, False, p)
    return (y_prompt.reshape(batch, seq, d), y_sample.reshape(dec_batch, dec_seq, d),
            finals[:, None])
```

```python
import functools

import jax
import jax.numpy as jnp
from jax import lax
from jax.experimental import pallas as pl
from jax.experimental.pallas import tpu as pltpu

F32 = jnp.float32
BF16 = jnp.bfloat16

D_MODEL = 1024
GRID_W = 64
A_WIDTH = 512
B_WIDTH = 512
DK = 128
DV = 128
H_A = 4
N_DIR = 2
N_POOL_GROUPS = 4
POOL_GROUP = D_MODEL // N_POOL_GROUPS
POOL_WINDOWS = (2, 4, 8, 16)
D_FF = 2816
N_MOD = 6
EPS = 1e-6
QKV_END = 3 * A_WIDTH
GATE_END = QKV_END + A_WIDTH
BETA_END = GATE_END + N_DIR * H_A
ALPHA_END = BETA_END + N_DIR * H_A
BG_END = ALPHA_END + B_WIDTH
CG_END = BG_END + B_WIDTH
IN_AB = CG_END + B_WIDTH

HALO = 8
CHUNK = 128
FF_CHUNK = 256
N_FF_CHUNKS = D_FF // FF_CHUNK
N_SMALL = 2 * N_DIR * H_A
SEQ_GROUP = 4
VMEM_LIMIT = 56 * 1024 * 1024
NEG_BIG = -1e30


def _sigmoid(x):
    return 1.0 / (1.0 + jnp.exp(-x))


def _silu(x):
    return x * _sigmoid(x)


def _softplus(x):
    return jnp.maximum(x, 0.0) + jnp.log1p(jnp.exp(-jnp.abs(x)))


def _dot(a, b):
    return jnp.dot(a, b, preferred_element_type=F32)


def _dot_nt(a, b):
    return lax.dot_general(a, b, (((1,), (1,)), ((), ())), preferred_element_type=F32)


def _dot_tn(a, b):
    return lax.dot_general(a, b, (((0,), (0,)), ((), ())), preferred_element_type=F32)


def _split3(a):
    a1 = a.astype(BF16)
    r1 = a - a1.astype(F32)
    a2 = r1.astype(BF16)
    a3 = (r1 - a2.astype(F32)).astype(BF16)
    return a1, a2, a3


def _norm_mod(x, g, scale, shift):
    ms = jnp.mean(x * x, axis=-1, keepdims=True)
    return (x * lax.rsqrt(ms + EPS)) * g * (1.0 + scale) + shift


def _shift_rows(a, k):
    n = a.shape[0]
    return pltpu.roll(a, (-k) % n, axis=0)


def _dwconv3_ext(p_ext, w, tm):
    prev = _shift_rows(p_ext, -1)[HALO:HALO + tm]
    cur = p_ext[HALO:HALO + tm]
    nxt = _shift_rows(p_ext, 1)[HALO:HALO + tm]
    return prev * w[0:1] + cur * w[1:2] + nxt * w[2:3]


def _halo_valid(tm, seq_len):
    i = pl.program_id(0)
    vp = jnp.where((i * tm) % seq_len != 0, 1.0, 0.0).astype(F32)
    vn = jnp.where(((i + 1) * tm) % seq_len != 0, 1.0, 0.0).astype(F32)
    return vp, vn


def _const_spec(shape):
    nd = len(shape)
    return pl.BlockSpec(shape, lambda *_: (0,) * nd)


def _tile_specs(tm, width, n_rows):
    per = tm // HALO
    last = n_rows // HALO - 1
    prev = pl.BlockSpec((HALO, width), lambda i: (jnp.maximum(i * per - 1, 0), 0))
    main = pl.BlockSpec((tm, width), lambda i: (i, 0))
    nxt = pl.BlockSpec((HALO, width), lambda i: (jnp.minimum((i + 1) * per, last), 0))
    return [prev, main, nxt]


def _pe_specs(tm, width, seq_len):
    per = tm // HALO
    tiles = seq_len // tm
    last = seq_len // HALO - 1
    prev = pl.BlockSpec((HALO, width), lambda i: (jnp.maximum((i % tiles) * per - 1, 0), 0))
    main = pl.BlockSpec((tm, width), lambda i: (i % tiles, 0))
    nxt = pl.BlockSpec((HALO, width), lambda i: (jnp.minimum((i % tiles + 1) * per, last), 0))
    return [prev, main, nxt]


def _mod_spec(tm, seq_len, row_base, row_stride):
    return pl.BlockSpec((1, 1, N_MOD * D_MODEL),
                        lambda i: (row_base + ((i * tm) // seq_len) * row_stride, 0, 0))


def _params(n_axes=1, semantics=None):
    return pltpu.CompilerParams(
        dimension_semantics=semantics or ("parallel",) * n_axes,
        vmem_limit_bytes=VMEM_LIMIT)


def _mod_kernel(c_ref, w_ref, b_ref, o_ref):
    c = c_ref[...]
    s1, s2, _ = _split3(_silu(c))
    w = w_ref[0]
    w1 = w.astype(BF16)
    w2 = (w - w1.astype(F32)).astype(BF16)
    o_ref[0] = _dot(s1, w1) + (_dot(s2, w1) + _dot(s1, w2)) + b_ref[0]


def _modulation(cond, w_ada, b_ada):
    depth, _, width = w_ada.shape
    rows = cond.shape[0]
    tn = 1536
    return pl.pallas_call(
        _mod_kernel,
        grid=(depth, width // tn),
        in_specs=[pl.BlockSpec((rows, D_MODEL), lambda l, j: (0, 0)),
                  pl.BlockSpec((1, D_MODEL, tn), lambda l, j: (l, 0, j)),
                  pl.BlockSpec((1, 1, tn), lambda l, j: (l, 0, j))],
        out_specs=pl.BlockSpec((1, rows, tn), lambda l, j: (l, 0, j)),
        out_shape=jax.ShapeDtypeStruct((depth, rows, width), F32),
        compiler_params=_params(2),
        name="modulation",
    )(cond, w_ada, b_ada.reshape(depth, 1, width))


def _inproj_kernel(*refs, tm, seq_len, has_pe):
    if has_pe:
        xp, xm, xn, pp, pm, pn = refs[:6]
        refs = refs[6:]
    else:
        xp, xm, xn = refs[:3]
        pp = pm = pn = None
        refs = refs[3:]
    (mod_ref, ng_ref, wqkv_ref, wgate_ref, wb_ref, wsc_ref, wsr_ref, cq_ref, cb_ref,
     pc_ref, pr_ref, q_ref, k_ref, v_ref, gate_ref, gbc_ref, gbr_ref, yb_ref) = refs

    vp, vn = _halo_valid(tm, seq_len)
    mod = mod_ref[0]
    shift = mod[:, 0:D_MODEL]
    scale = mod[:, D_MODEL:2 * D_MODEL]
    g = ng_ref[...]

    def prep(x_ref, pe_ref):
        x = x_ref[...]
        if has_pe:
            x = x + pe_ref[...]
        return _norm_mod(x, g, scale, shift)

    h_main = prep(xm, pm)
    h_ext = jnp.concatenate([prep(xp, pp) * vp, h_main, prep(xn, pn) * vn], axis=0)
    hb_ext = h_ext.astype(BF16)
    hb = h_main.astype(BF16)

    cq = cq_ref[...]
    outs = (q_ref, k_ref, v_ref)
    for part in range(3):
        cols = slice(part * A_WIDTH, (part + 1) * A_WIDTH)
        p = _dot(hb_ext, wqkv_ref[:, cols])
        a = _silu(_dwconv3_ext(p, cq[:, cols], tm))
        if part < 2:
            heads = []
            for h in range(H_A):
                ah = a[:, h * DK:(h + 1) * DK]
                ss = jnp.sum(ah * ah, axis=-1, keepdims=True)
                nrm = lax.rsqrt(ss + EPS)
                if part == 0:
                    nrm = nrm * (DK ** -0.5)
                heads.append(ah * nrm)
            a = jnp.concatenate(heads, axis=1)
        outs[part][...] = a

    gate_ref[...] = _dot(hb, wgate_ref[...])

    pc = pc_ref[...]
    ba = _dot(hb, wsc_ref[...])
    lane = lax.broadcasted_iota(jnp.int32, ba.shape, 1)
    gbc_ref[...] = jnp.where(lane < N_DIR * H_A, _sigmoid(ba),
                             -pc[0:1] * _softplus(ba + pc[1:2]))
    pr = pr_ref[...]
    bar = _dot_nt(wsr_ref[...], hb)
    sub = lax.broadcasted_iota(jnp.int32, bar.shape, 0)
    gbr_ref[...] = jnp.where(sub < N_DIR * H_A, _sigmoid(bar),
                             -pr[:, 0:1] * _softplus(bar + pr[:, 1:2]))

    pb = _dot(hb_ext, wb_ref[...])
    bg = pb[HALO:HALO + tm, 0:B_WIDTH]
    cghx = pb[:, B_WIDTH:2 * B_WIDTH] * pb[:, 2 * B_WIDTH:3 * B_WIDTH]
    yb_ref[...] = bg * _dwconv3_ext(cghx, cb_ref[...], tm)


def _inproj(x, pe, mod3, row_base, row_stride, seq_len, tm, ng, wts):
    n_rows = x.shape[0]
    has_pe = pe is not None
    in_specs = _tile_specs(tm, D_MODEL, n_rows)
    args = [x, x, x]
    if has_pe:
        in_specs += _pe_specs(tm, D_MODEL, seq_len)
        args += [pe, pe, pe]
    in_specs += [_mod_spec(tm, seq_len, row_base, row_stride), _const_spec((1, D_MODEL)),
                 _const_spec((D_MODEL, QKV_END)), _const_spec((D_MODEL, A_WIDTH)),
                 _const_spec((D_MODEL, 3 * B_WIDTH)), _const_spec((D_MODEL, N_SMALL)),
                 _const_spec((N_SMALL, D_MODEL)), _const_spec((3, QKV_END)),
                 _const_spec((3, B_WIDTH)), _const_spec((2, N_SMALL)), _const_spec((N_SMALL, 2))]
    args += [mod3, ng, wts["w_qkv"], wts["w_gate"], wts["w_b"], wts["w_small_c"], wts["w_small_r"],
             wts["conv_qkv"], wts["conv_b"], wts["p_col"], wts["p_row"]]
    wide = pl.BlockSpec((tm, A_WIDTH), lambda i: (i, 0))
    out_specs = [wide, wide, wide, wide,
                 pl.BlockSpec((tm, N_SMALL), lambda i: (i, 0)),
                 pl.BlockSpec((N_SMALL, tm), lambda i: (0, i)),
                 wide]
    wide_shape = jax.ShapeDtypeStruct((n_rows, A_WIDTH), F32)
    out_shape = [wide_shape] * 4 + [jax.ShapeDtypeStruct((n_rows, N_SMALL), F32),
                                    jax.ShapeDtypeStruct((N_SMALL, n_rows), F32), wide_shape]
    return pl.pallas_call(
        functools.partial(_inproj_kernel, tm=tm, seq_len=seq_len, has_pe=has_pe),
        grid=(n_rows // tm,), in_specs=in_specs, out_specs=out_specs, out_shape=out_shape,
        compiler_params=_params(1), name="inproj",
    )(*args)


def _dot3_right(a_exact, b):
    b1, b2, b3 = _split3(b)
    return _dot(a_exact, b1) + (_dot(a_exact, b2) + _dot(a_exact, b3))


def _dot3_left(a, b_exact):
    a1, a2, a3 = _split3(a)
    return _dot(a1, b_exact) + (_dot(a2, b_exact) + _dot(a3, b_exact))


def _chunk_local_kernel(q_ref, k_ref, v_ref, gbc_ref, gbr_ref,
                        u_ref, w_ref, qg_ref, kg_ref, at_ref, gl_ref, *, n_sub):
    c = CHUNK
    n_beta = N_DIR * H_A
    r1 = lax.broadcasted_iota(jnp.int32, (c, c), 0)
    c1 = lax.broadcasted_iota(jnp.int32, (c, c), 1)
    tri_lo = jnp.where(r1 >= c1, 1.0, 0.0).astype(BF16)
    tri_up = jnp.where(r1 <= c1, 1.0, 0.0).astype(BF16)
    r2 = lax.broadcasted_iota(jnp.int32, (2 * c, 2 * c), 0)
    c2 = lax.broadcasted_iota(jnp.int32, (2 * c, 2 * c), 1)
    x2 = r2 ^ c2
    same_head = x2 < c
    eye = jnp.where(r2 == c2, 1.0, 0.0).astype(F32)

    def col_pair(a, j):
        return jnp.concatenate([a[:, j:j + 1], a[:, j + 1:j + 2]], axis=0)

    def row_pair(a, j):
        return jnp.concatenate([a[j:j + 1, :], a[j + 1:j + 2, :]], axis=1)

    chains = []
    for ci in range(n_sub):
        rows_ci = slice(ci * c, (ci + 1) * c)
        gbc = gbc_ref[rows_ci, :]
        gbr = gbr_ref[:, rows_ci]
        pre_c = _dot3_right(tri_lo, gbc)
        suf_c = _dot3_right(tri_up, gbc)
        lane = lax.broadcasted_iota(jnp.int32, gbc.shape, 1)
        fwd_c = lane < n_beta + H_A
        gc_c = jnp.where(fwd_c, pre_c, suf_c)
        rest_c = jnp.where(fwd_c, suf_c, pre_c) - gbc
        eg_c = jnp.exp(gc_c)
        er_c = jnp.exp(rest_c)
        pre_r = _dot3_left(gbr, tri_up)
        suf_r = _dot3_left(gbr, tri_lo)
        sub = lax.broadcasted_iota(jnp.int32, gbr.shape, 0)
        gc_r = jnp.where(sub < n_beta + H_A, pre_r, suf_r)
        total = jnp.sum(gbr, axis=1, keepdims=True)
        gl_ref[ci] = jnp.broadcast_to(jnp.exp(total), (2 * n_beta, c))[n_beta:, :]

        q = q_ref[rows_ci, :]
        k = k_ref[rows_ci, :]
        v = v_ref[rows_ci, :]
        for hp in range(H_A // 2):
            h0 = 2 * hp

            def stack(a):
                return jnp.concatenate([a[:, h0 * DK:(h0 + 1) * DK],
                                        a[:, (h0 + 1) * DK:(h0 + 2) * DK]], axis=0)

            q2, k2, v2 = stack(q), stack(k), stack(v)
            k2b = k2.astype(BF16)
            kk = _dot_nt(k2b, k2b)
            qk = _dot_nt(q2.astype(BF16), k2b)
            for d in range(N_DIR):
                jb = d * H_A + h0
                jg = n_beta + jb
                gcc = col_pair(gc_c, jg)
                egc = col_pair(eg_c, jg)
                erc = col_pair(er_c, jg)
                bc = col_pair(gbc, jb)
                gcr = row_pair(gc_r, jg)
                tri = (r2 >= c2) if d == 0 else (r2 <= c2)
                decay = jnp.exp(jnp.where(same_head & tri, gcc - gcr, NEG_BIG))
                attn = qk * decay
                m = jnp.where(r2 != c2, (bc * kk) * decay, 0.0)
                rhs = jnp.concatenate([v2 * bc, k2 * (bc * egc)], axis=1).astype(BF16)
                qg = q2 * egc
                kg = k2 * erc
                for hh in range(2):
                    rows = slice(hh * c, (hh + 1) * c)
                    cols = slice((h0 + hh) * DK, (h0 + hh + 1) * DK)
                    qg_ref[d, rows_ci, cols] = qg[rows].astype(BF16)
                    kg_ref[d, rows_ci, cols] = kg[rows].astype(BF16)
                    at_ref[d, rows_ci, cols] = attn[rows, hh * c:(hh + 1) * c].astype(BF16)
                chains.append({"ci": ci, "h0": h0, "d": d, "m": m, "rhs": rhs,
                               "t": eye - jnp.where(x2 == 1, m, 0.0)})

    for lvl in range(1, 7):
        level = (x2 >> lvl) == 1
        tbs = [ch["t"].astype(BF16) for ch in chains]
        xs = [_dot(jnp.where(level, ch["m"], 0.0).astype(BF16), tb) for ch, tb in zip(chains, tbs)]
        for ch, tb, x in zip(chains, tbs, xs):
            ch["t"] = ch["t"] - _dot(tb, x.astype(BF16))

    for ch in chains:
        uw = _dot(ch["t"].astype(BF16), ch["rhs"])
        rows_ci = slice(ch["ci"] * c, (ch["ci"] + 1) * c)
        for hh in range(2):
            rows = slice(hh * c, (hh + 1) * c)
            cols = slice((ch["h0"] + hh) * DK, (ch["h0"] + hh + 1) * DK)
            u_ref[ch["d"], rows_ci, cols] = uw[rows, 0:DV]
            w_ref[ch["d"], rows_ci, cols] = uw[rows, DV:2 * DV].astype(BF16)


def _chunk_local(q, k, v, gbc, gbr, n_sub=2):
    n_rows = q.shape[0]
    n_chunks = n_rows // CHUNK
    rows = n_sub * CHUNK
    wide = pl.BlockSpec((rows, A_WIDTH), lambda i: (i, 0))
    out_wide = pl.BlockSpec((N_DIR, rows, A_WIDTH), lambda i: (0, i, 0))
    big = lambda dt: jax.ShapeDtypeStruct((N_DIR, n_rows, A_WIDTH), dt)
    return pl.pallas_call(
        functools.partial(_chunk_local_kernel, n_sub=n_sub),
        grid=(n_chunks // n_sub,),
        in_specs=[wide, wide, wide,
                  pl.BlockSpec((rows, N_SMALL), lambda i: (i, 0)),
                  pl.BlockSpec((N_SMALL, rows), lambda i: (0, i))],
        out_specs=[out_wide] * 5 + [pl.BlockSpec((n_sub, N_DIR * H_A, CHUNK), lambda i: (i, 0, 0))],
        out_shape=[big(F32), big(BF16), big(BF16), big(BF16), big(BF16),
                   jax.ShapeDtypeStruct((n_chunks, N_DIR * H_A, CHUNK), F32)],
        compiler_params=_params(1), name="chunk_local",
    )(q, k, v, gbc, gbr)


def _scan_kernel(*refs, n_chunks, zero_init, write_final):
    fwd = refs[0:6]
    bwd = refs[6:12]
    refs = refs[12:]
    if not zero_init:
        s0_ref = refs[0]
        refs = refs[1:]
    of_ref, ob_ref = refs[0:2]
    refs = refs[2:]
    if write_final:
        sfin_ref = refs[0]
        refs = refs[1:]
    s_scr = refs[0]
    i = pl.program_id(1)

    @pl.when(i == 0)
    def _():
        if zero_init:
            s_scr[...] = jnp.zeros(s_scr.shape, F32)
        else:
            s_scr[...] = s0_ref[...]

    for s in range(SEQ_GROUP):
        for d in range(N_DIR):
            u_ref, w_ref, qg_ref, kg_ref, at_ref, gl_ref = fwd if d == 0 else bwd
            o_ref = of_ref if d == 0 else ob_ref
            for h in range(H_A):
                cols = slice(h * DV, (h + 1) * DV)
                st = s_scr[s, d, h]
                sb = st.astype(BF16)
                v_new = u_ref[0, s, :, cols] - _dot(w_ref[0, s, :, cols], sb)
                vb = v_new.astype(BF16)
                o_ref[s, :, cols] = _dot(qg_ref[0, s, :, cols], sb) + _dot(at_ref[0, s, :, cols], vb)
                gl = gl_ref[s, 0, d * H_A + h:d * H_A + h + 1, :]
                s_scr[s, d, h] = st * gl + _dot_tn(kg_ref[0, s, :, cols], vb)

    if write_final:
        @pl.when(i == n_chunks - 1)
        def _():
            sfin_ref[...] = s_scr[...]


def _scan(local, s0, n_seq, seq_len, write_final):
    u, w, qg, kg, at, gl = local
    n_chunks = seq_len // CHUNK
    zero_init = s0 is None
    shp4 = (N_DIR, n_seq, seq_len, A_WIDTH)
    arrs = [a.reshape(shp4) for a in (u, w, qg, kg, at)]
    gl4 = gl.reshape(n_seq, n_chunks, N_DIR * H_A, CHUNK)
    sg = SEQ_GROUP

    def specs(d, chunk_of):
        big = pl.BlockSpec((1, sg, CHUNK, A_WIDTH), lambda g, i: (d, g, chunk_of(i), 0))
        return [big] * 5 + [pl.BlockSpec((sg, 1, N_DIR * H_A, CHUNK),
                                         lambda g, i: (g, chunk_of(i), 0, 0))]

    rev = lambda i: n_chunks - 1 - i
    in_specs = specs(0, lambda i: i) + specs(1, rev)
    args = arrs + [gl4] + arrs + [gl4]
    state_spec = pl.BlockSpec((sg, N_DIR, H_A, DK, DV), lambda g, i: (g, 0, 0, 0, 0))
    if not zero_init:
        in_specs.append(state_spec)
        args.append(s0)
    o_shape = jax.ShapeDtypeStruct((n_seq, seq_len, A_WIDTH), F32)
    out_specs = [pl.BlockSpec((sg, CHUNK, A_WIDTH), lambda g, i: (g, i, 0)),
                 pl.BlockSpec((sg, CHUNK, A_WIDTH), lambda g, i: (g, rev(i), 0))]
    out_shape = [o_shape, o_shape]
    if write_final:
        out_specs.append(state_spec)
        out_shape.append(jax.ShapeDtypeStruct((n_seq, N_DIR, H_A, DK, DV), F32))
    return pl.pallas_call(
        functools.partial(_scan_kernel, n_chunks=n_chunks, zero_init=zero_init,
                          write_final=write_final),
        grid=(n_seq // sg, n_chunks), in_specs=in_specs, out_specs=out_specs, out_shape=out_shape,
        scratch_shapes=[pltpu.VMEM((sg, N_DIR, H_A, DK, DV), F32)],
        compiler_params=_params(2, ("parallel", "arbitrary")), name="delta_scan",
    )(*args)


def _outproj_kernel(*refs, has_pe):
    if has_pe:
        x_ref, pe_ref = refs[:2]
        refs = refs[2:]
    else:
        x_ref = refs[0]
        refs = refs[1:]
    of_ref, ob_ref, gate_ref, yb_ref, mod_ref, og_ref, wout_ref, out_ref = refs
    o = of_ref[...] + ob_ref[...]
    sg = _silu(gate_ref[...])
    og = og_ref[...]
    parts = []
    for h in range(H_A):
        cols = slice(h * DV, (h + 1) * DV)
        oh = o[:, cols]
        ms = jnp.mean(oh * oh, axis=-1, keepdims=True)
        parts.append((oh * lax.rsqrt(ms + EPS)) * og * sg[:, cols])
    a = jnp.concatenate(parts + [yb_ref[...]], axis=1).astype(BF16)
    y = _dot(a, wout_ref[...])
    x = x_ref[...]
    if has_pe:
        x = x + pe_ref[...]
    gate1 = mod_ref[0][:, 2 * D_MODEL:3 * D_MODEL]
    out_ref[...] = x + gate1 * y


def _outproj(x, pe, o_f, o_b, gate, yb, mod3, row_base, row_stride, seq_len, tm, og, w_out):
    n_rows = x.shape[0]
    has_pe = pe is not None
    full = pl.BlockSpec((tm, D_MODEL), lambda i: (i, 0))
    half = pl.BlockSpec((tm, A_WIDTH), lambda i: (i, 0))
    in_specs = [full]
    args = [x]
    if has_pe:
        tiles = seq_len // tm
        in_specs.append(pl.BlockSpec((tm, D_MODEL), lambda i: (i % tiles, 0)))
        args.append(pe)
    in_specs += [half, half, half, half, _mod_spec(tm, seq_len, row_base, row_stride),
                 _const_spec((1, DV)), _const_spec((D_MODEL, D_MODEL))]
    args += [o_f, o_b, gate, yb, mod3, og, w_out]
    return pl.pallas_call(
        functools.partial(_outproj_kernel, has_pe=has_pe),
        grid=(n_rows // tm,), in_specs=in_specs, out_specs=full,
        out_shape=jax.ShapeDtypeStruct((n_rows, D_MODEL), F32),
        compiler_params=_params(1), name="outproj",
    )(*args)


def _ffn_kernel(*refs, tm, seq_len, final_norm):
    xp, xm, xn, mod_ref, ng_ref, wg_ref, wu_ref, wd_ref, cf_ref = refs[:9]
    refs = refs[9:]
    if final_norm:
        fg_ref = refs[0]
        refs = refs[1:]
    out_ref = refs[0]
    vp, vn = _halo_valid(tm, seq_len)
    mod = mod_ref[0]
    shift = mod[:, 3 * D_MODEL:4 * D_MODEL]
    scale = mod[:, 4 * D_MODEL:5 * D_MODEL]
    gate2 = mod[:, 5 * D_MODEL:6 * D_MODEL]
    g = ng_ref[...]
    x = xm[...]
    h_main = _norm_mod(x, g, scale, shift)
    h_ext = jnp.concatenate([_norm_mod(xp[...], g, scale, shift) * vp, h_main,
                             _norm_mod(xn[...], g, scale, shift) * vn], axis=0)
    hb_ext = h_ext.astype(BF16)
    hb = h_main.astype(BF16)
    acc = jnp.zeros((tm, D_MODEL), F32)
    for j in range(N_FF_CHUNKS):
        u = _dwconv3_ext(_dot(hb_ext, wg_ref[j]), cf_ref[j], tm)
        a = (_silu(u) * _dot(hb, wu_ref[j])).astype(BF16)
        acc = acc + _dot(a, wd_ref[j])
    y = x + gate2 * acc
    if final_norm:
        ms = jnp.mean(y * y, axis=-1, keepdims=True)
        y = (y * lax.rsqrt(ms + EPS)) * fg_ref[...]
    out_ref[...] = y


def _ffn(x, mod3, row_base, row_stride, seq_len, tm, ng, wts, final_g):
    n_rows = x.shape[0]
    final_norm = final_g is not None
    in_specs = _tile_specs(tm, D_MODEL, n_rows)
    in_specs += [_mod_spec(tm, seq_len, row_base, row_stride), _const_spec((1, D_MODEL)),
                 _const_spec((N_FF_CHUNKS, D_MODEL, FF_CHUNK)),
                 _const_spec((N_FF_CHUNKS, D_MODEL, FF_CHUNK)),
                 _const_spec((N_FF_CHUNKS, FF_CHUNK, D_MODEL)),
                 _const_spec((N_FF_CHUNKS, 3, FF_CHUNK))]
    args = [x, x, x, mod3, ng, wts["w_gate"], wts["w_up"], wts["w_down"], wts["conv"]]
    if final_norm:
        in_specs.append(_const_spec((1, D_MODEL)))
        args.append(final_g)
    return pl.pallas_call(
        functools.partial(_ffn_kernel, tm=tm, seq_len=seq_len, final_norm=final_norm),
        grid=(n_rows // tm,), in_specs=in_specs,
        out_specs=pl.BlockSpec((tm, D_MODEL), lambda i: (i, 0)),
        out_shape=jax.ShapeDtypeStruct((n_rows, D_MODEL), F32),
        compiler_params=_params(1), name="conv_ffn",
    )(*args)


def _pool_kernel(xp, xm, xn, mod_ref, ng_ref, wp_ref, ps_ref, out_ref, *, tm, seq_len):
    vp, vn = _halo_valid(tm, seq_len)
    mod = mod_ref[0]
    shift = mod[:, 0:D_MODEL]
    scale = mod[:, D_MODEL:2 * D_MODEL]
    gate1 = mod[:, 2 * D_MODEL:3 * D_MODEL]
    g = ng_ref[...]
    x = xm[...]
    h_main = _norm_mod(x, g, scale, shift)
    h_ext = jnp.concatenate([_norm_mod(xp[...], g, scale, shift) * vp, h_main,
                             _norm_mod(xn[...], g, scale, shift) * vn], axis=0)
    i = pl.program_id(0)
    pos = (i * tm + lax.broadcasted_iota(jnp.int32, (tm, POOL_GROUP), 0)) % seq_len
    ys = []
    for gi, win in enumerate(POOL_WINDOWS):
        cols = slice(gi * POOL_GROUP, (gi + 1) * POOL_GROUP)
        s = h_ext[:, cols]
        width = 1
        while width < win:
            s = s + _shift_rows(s, width)
            width *= 2
        start = HALO - win // 2
        if start:
            s = _shift_rows(s, start)
        s = s[0:tm]
        lo = jnp.maximum(pos - win // 2, 0)
        hi = jnp.minimum(pos - win // 2 + win, seq_len)
        mixed = (s / (hi - lo).astype(F32) - h_main[:, cols]).astype(BF16)
        ys.append(_dot(mixed, wp_ref[gi]))
    y = jnp.concatenate(ys, axis=1) * ps_ref[...]
    out_ref[...] = x + gate1 * y


def _pool(x, mod3, row_base, row_stride, seq_len, tm, ng, w_pool, pool_scale):
    n_rows = x.shape[0]
    in_specs = _tile_specs(tm, D_MODEL, n_rows)
    in_specs += [_mod_spec(tm, seq_len, row_base, row_stride), _const_spec((1, D_MODEL)),
                 _const_spec((N_POOL_GROUPS, POOL_GROUP, POOL_GROUP)), _const_spec((1, D_MODEL))]
    return pl.pallas_call(
        functools.partial(_pool_kernel, tm=tm, seq_len=seq_len),
        grid=(n_rows // tm,), in_specs=in_specs,
        out_specs=pl.BlockSpec((tm, D_MODEL), lambda i: (i, 0)),
        out_shape=jax.ShapeDtypeStruct((n_rows, D_MODEL), F32),
        compiler_params=_params(1), name="pool_mixer",
    )(x, x, x, mod3, ng, w_pool, pool_scale)


def _sincos_2d(rows, cols, d):
    quarter = d // 4
    omega = 1.0 / (10000.0 ** (jnp.arange(quarter, dtype=F32) / quarter))
    er = jnp.arange(rows, dtype=F32)[:, None] * omega[None, :]
    ec = jnp.arange(cols, dtype=F32)[:, None] * omega[None, :]
    er = jnp.concatenate([jnp.sin(er), jnp.cos(er)], axis=-1)
    ec = jnp.concatenate([jnp.sin(ec), jnp.cos(ec)], axis=-1)
    pe = jnp.concatenate([jnp.broadcast_to(er[:, None, :], (rows, cols, d // 2)),
                          jnp.broadcast_to(ec[None, :, :], (rows, cols, d // 2))], axis=-1)
    return pe.reshape(rows * cols, d)


def _even_layer_weights(w_in, conv_qkv, a_log, dt_bias, conv_b, w_out):
    small = w_in[:, GATE_END:ALPHA_END]
    zeros = jnp.zeros((N_DIR * H_A,), F32)
    ea = jnp.concatenate([zeros, jnp.exp(a_log.astype(F32)).reshape(-1)])
    dtb = jnp.concatenate([zeros, dt_bias.astype(F32).reshape(-1)])
    p_col = jnp.stack([ea, dtb], axis=0)
    return {
        "w_qkv": w_in[:, :QKV_END].astype(BF16),
        "w_gate": w_in[:, QKV_END:GATE_END].astype(BF16),
        "w_b": w_in[:, ALPHA_END:IN_AB].astype(BF16),
        "w_small_c": small.astype(BF16),
        "w_small_r": small.T.astype(BF16),
        "conv_qkv": conv_qkv, "conv_b": conv_b,
        "p_col": p_col, "p_row": p_col.T,
        "w_out": w_out.astype(BF16),
    }


def _ffn_weights(w_gate, w_up, conv, w_down):
    def slabs(w):
        return w.reshape(D_MODEL, N_FF_CHUNKS, FF_CHUNK).transpose(1, 0, 2).astype(BF16)
    return {
        "w_gate": slabs(w_gate), "w_up": slabs(w_up),
        "w_down": w_down.reshape(N_FF_CHUNKS, FF_CHUNK, D_MODEL).astype(BF16),
        "conv": conv.reshape(3, N_FF_CHUNKS, FF_CHUNK).transpose(1, 0, 2),
    }


def _trunk(x, pe, mod, row_base, row_stride, n_seq, seq_len, tm, s0, write_final, p):
    row = lambda a: a.reshape(1, -1)
    finals = None
    depth = mod.shape[0]
    for layer in range(depth):
        mod3 = mod[layer][:, None, :]
        place = (mod3, row_base, row_stride, seq_len, tm)
        if layer % 2 == 0:
            e = layer // 2
            wts = p["even"][e]
            q, k, v, gate, gbc, gbr, yb = _inproj(x, pe, *place, row(p["norm_mix_g"][layer]), wts)
            local = _chunk_local(q, k, v, gbc, gbr)
            res = _scan(local, None if s0 is None else s0[:, e], n_seq, seq_len, write_final)
            o_f = res[0].reshape(-1, A_WIDTH)
            o_b = res[1].reshape(-1, A_WIDTH)
            if write_final:
                finals = res[2]
            x = _outproj(x, pe, o_f, o_b, gate, yb, *place, row(p["o_norm_g"][e]), wts["w_out"])
        else:
            o = layer // 2
            x = _pool(x, *place, row(p["norm_mix_g"][layer]), p["w_pool"][o],
                      row(p["pool_scale"][o]))
        final_g = row(p["final_norm_g"]) if layer == depth - 1 else None
        x = _ffn(x, *place, row(p["norm_ffn_g"][layer]), p["ffn"][layer], final_g)
    return x, finals


def kernel(x_prompt, x_sample, state_delta, c, c_ctx, norm_mix_g, norm_ffn_g, w_ada, b_ada,
           w_in_ab, conv_qkv, a_log, dt_bias, o_norm_g, conv_b, w_out_ab, w_pool, pool_scale,
           w_ffn_gate, w_ffn_up, ffn_conv, w_ffn_down, final_norm_g):
    batch, seq, d = x_prompt.shape
    dec_batch, dec_seq, _ = x_sample.shape
    depth = w_ada.shape[0]
    n_even = w_in_ab.shape[0]
    assert d == D_MODEL and depth == 2 and n_even == 1

    p = {
        "norm_mix_g": norm_mix_g, "norm_ffn_g": norm_ffn_g, "o_norm_g": o_norm_g,
        "pool_scale": pool_scale, "final_norm_g": final_norm_g,
        "w_pool": w_pool.astype(BF16),
        "even": [_even_layer_weights(w_in_ab[e], conv_qkv[e], a_log[e], dt_bias[e], conv_b[e],
                                     w_out_ab[e]) for e in range(n_even)],
        "ffn": [_ffn_weights(w_ffn_gate[l], w_ffn_up[l], ffn_conv[l], w_ffn_down[l])
                for l in range(depth)],
    }

    n_cond = 1 + dec_batch
    pad = (-n_cond) % HALO
    cond = jnp.concatenate([c_ctx[None, :], c, jnp.zeros((pad, d), F32)], axis=0)
    mod = _modulation(cond, w_ada, b_ada)

    y_prompt, finals = _trunk(x_prompt.reshape(batch * seq, d), None, mod, 0, 0, batch, seq,
                              min(seq, 256), None, True, p)
    pe = _sincos_2d(dec_seq // GRID_W, GRID_W, d)
    y_sample, _ = _trunk(x_sample.reshape(dec_batch * dec_seq, d), pe, mod, 1, 1, dec_batch,
                         dec_seq, 512, state_delta, False, p)
    return (y_prompt.reshape(batch, seq, d), y_sample.reshape(dec_batch, dec_seq, d),
            finals[:, None])
```

```python
import functools

import jax
import jax.numpy as jnp
from jax import lax
from jax.experimental import pallas as pl
from jax.experimental.pallas import tpu as pltpu

F32 = jnp.float32
BF16 = jnp.bfloat16

D_MODEL = 1024
GRID_W = 64
A_WIDTH = 512
B_WIDTH = 512
DK = 128
DV = 128
H_A = 4
N_DIR = 2
N_POOL_GROUPS = 4
POOL_GROUP = D_MODEL // N_POOL_GROUPS
POOL_WINDOWS = (2, 4, 8, 16)
D_FF = 2816
N_MOD = 6
EPS = 1e-6
QKV_END = 3 * A_WIDTH
GATE_END = QKV_END + A_WIDTH
BETA_END = GATE_END + N_DIR * H_A
ALPHA_END = BETA_END + N_DIR * H_A
BG_END = ALPHA_END + B_WIDTH
CG_END = BG_END + B_WIDTH
IN_AB = CG_END + B_WIDTH

HALO = 8
CHUNK = 128
FF_CHUNK = 256
N_FF_CHUNKS = D_FF // FF_CHUNK
N_SMALL = 2 * N_DIR * H_A
SEQ_GROUP = 4
VMEM_LIMIT = 56 * 1024 * 1024
NEG_BIG = -1e30


def _sigmoid(x):
    return 1.0 / (1.0 + jnp.exp(-x))


def _silu(x):
    return x * _sigmoid(x)


def _softplus(x):
    return jnp.maximum(x, 0.0) + jnp.log1p(jnp.exp(-jnp.abs(x)))


def _dot(a, b):
    return jnp.dot(a, b, preferred_element_type=F32)


def _dot_nt(a, b):
    return lax.dot_general(a, b, (((1,), (1,)), ((), ())), preferred_element_type=F32)


def _dot_tn(a, b):
    return lax.dot_general(a, b, (((0,), (0,)), ((), ())), preferred_element_type=F32)


def _split3(a):
    a1 = a.astype(BF16)
    r1 = a - a1.astype(F32)
    a2 = r1.astype(BF16)
    a3 = (r1 - a2.astype(F32)).astype(BF16)
    return a1, a2, a3


def _norm_mod(x, g, scale, shift):
    ms = jnp.mean(x * x, axis=-1, keepdims=True)
    return (x * lax.rsqrt(ms + EPS)) * g * (1.0 + scale) + shift


def _shift_rows(a, k):
    n = a.shape[0]
    return pltpu.roll(a, (-k) % n, axis=0)


def _dwconv3_ext(p_ext, w, tm, edge_masks=None):
    prev = _shift_rows(p_ext, -1)[HALO:HALO + tm]
    cur = p_ext[HALO:HALO + tm]
    nxt = _shift_rows(p_ext, 1)[HALO:HALO + tm]
    if edge_masks is not None:
        prev = prev * edge_masks[0]
        nxt = nxt * edge_masks[1]
    return prev * w[0:1] + cur * w[1:2] + nxt * w[2:3]


def _seq_edge_masks(tm, seq_len, width):
    if tm <= seq_len:
        return None
    pos = lax.broadcasted_iota(jnp.int32, (tm, width), 0) % seq_len
    return (jnp.where(pos != 0, 1.0, 0.0).astype(F32),
            jnp.where(pos != seq_len - 1, 1.0, 0.0).astype(F32))


def _halo_valid(tm, seq_len):
    i = pl.program_id(0)
    vp = jnp.where((i * tm) % seq_len != 0, 1.0, 0.0).astype(F32)
    vn = jnp.where(((i + 1) * tm) % seq_len != 0, 1.0, 0.0).astype(F32)
    return vp, vn


def _const_spec(shape):
    nd = len(shape)
    return pl.BlockSpec(shape, lambda *_: (0,) * nd)


def _tile_specs(tm, width, n_rows):
    per = tm // HALO
    last = n_rows // HALO - 1
    prev = pl.BlockSpec((HALO, width), lambda i: (jnp.maximum(i * per - 1, 0), 0))
    main = pl.BlockSpec((tm, width), lambda i: (i, 0))
    nxt = pl.BlockSpec((HALO, width), lambda i: (jnp.minimum((i + 1) * per, last), 0))
    return [prev, main, nxt]


def _pe_specs(tm, width, seq_len):
    per = tm // HALO
    tiles = seq_len // tm
    last = seq_len // HALO - 1
    prev = pl.BlockSpec((HALO, width), lambda i: (jnp.maximum((i % tiles) * per - 1, 0), 0))
    main = pl.BlockSpec((tm, width), lambda i: (i % tiles, 0))
    nxt = pl.BlockSpec((HALO, width), lambda i: (jnp.minimum((i % tiles + 1) * per, last), 0))
    return [prev, main, nxt]


def _mod_spec(tm, seq_len, row_base, row_stride):
    return pl.BlockSpec((1, 1, N_MOD * D_MODEL),
                        lambda i: (row_base + ((i * tm) // seq_len) * row_stride, 0, 0))


def _params(n_axes=1, semantics=None):
    return pltpu.CompilerParams(
        dimension_semantics=semantics or ("parallel",) * n_axes,
        vmem_limit_bytes=VMEM_LIMIT)


def _mod_kernel(c_ref, w_ref, b_ref, o_ref):
    c = c_ref[...]
    s1, s2, _ = _split3(_silu(c))
    w = w_ref[0]
    w1 = w.astype(BF16)
    w2 = (w - w1.astype(F32)).astype(BF16)
    o_ref[0] = _dot(s1, w1) + (_dot(s2, w1) + _dot(s1, w2)) + b_ref[0]


def _modulation(cond, w_ada, b_ada):
    depth, _, width = w_ada.shape
    rows = cond.shape[0]
    tn = 1536
    return pl.pallas_call(
        _mod_kernel,
        grid=(depth, width // tn),
        in_specs=[pl.BlockSpec((rows, D_MODEL), lambda l, j: (0, 0)),
                  pl.BlockSpec((1, D_MODEL, tn), lambda l, j: (l, 0, j)),
                  pl.BlockSpec((1, 1, tn), lambda l, j: (l, 0, j))],
        out_specs=pl.BlockSpec((1, rows, tn), lambda l, j: (l, 0, j)),
        out_shape=jax.ShapeDtypeStruct((depth, rows, width), F32),
        compiler_params=_params(2),
        name="modulation",
    )(cond, w_ada, b_ada.reshape(depth, 1, width))


def _inproj_kernel(*refs, tm, seq_len, has_pe):
    if has_pe:
        xp, xm, xn, pp, pm, pn = refs[:6]
        refs = refs[6:]
    else:
        xp, xm, xn = refs[:3]
        pp = pm = pn = None
        refs = refs[3:]
    (mod_ref, ng_ref, wqkv_ref, wgate_ref, wb_ref, wsc_ref, wsr_ref, cq_ref, cb_ref,
     pc_ref, pr_ref, q_ref, k_ref, v_ref, gate_ref, gbc_ref, gbr_ref, yb_ref) = refs

    vp, vn = _halo_valid(tm, seq_len)
    mod = mod_ref[0]
    shift = mod[:, 0:D_MODEL]
    scale = mod[:, D_MODEL:2 * D_MODEL]
    g = ng_ref[...]

    def prep(x_ref, pe_ref):
        x = x_ref[...]
        if has_pe:
            x = x + pe_ref[...]
        return _norm_mod(x, g, scale, shift)

    h_main = prep(xm, pm)
    h_ext = jnp.concatenate([prep(xp, pp) * vp, h_main, prep(xn, pn) * vn], axis=0)
    hb_ext = h_ext.astype(BF16)
    hb = h_main.astype(BF16)

    cq = cq_ref[...]
    outs = (q_ref, k_ref, v_ref)
    for part in range(3):
        cols = slice(part * A_WIDTH, (part + 1) * A_WIDTH)
        p = _dot(hb_ext, wqkv_ref[:, cols])
        a = _silu(_dwconv3_ext(p, cq[:, cols], tm))
        if part < 2:
            heads = []
            for h in range(H_A):
                ah = a[:, h * DK:(h + 1) * DK]
                ss = jnp.sum(ah * ah, axis=-1, keepdims=True)
                nrm = lax.rsqrt(ss + EPS)
                if part == 0:
                    nrm = nrm * (DK ** -0.5)
                heads.append(ah * nrm)
            a = jnp.concatenate(heads, axis=1)
        outs[part][...] = a

    gate_ref[...] = _dot(hb, wgate_ref[...])

    pc = pc_ref[...]
    ba = _dot(hb, wsc_ref[...])
    lane = lax.broadcasted_iota(jnp.int32, ba.shape, 1)
    gbc_ref[...] = jnp.where(lane < N_DIR * H_A, _sigmoid(ba),
                             -pc[0:1] * _softplus(ba + pc[1:2]))
    pr = pr_ref[...]
    bar = _dot_nt(wsr_ref[...], hb)
    sub = lax.broadcasted_iota(jnp.int32, bar.shape, 0)
    gbr_ref[...] = jnp.where(sub < N_DIR * H_A, _sigmoid(bar),
                             -pr[:, 0:1] * _softplus(bar + pr[:, 1:2]))

    pb = _dot(hb_ext, wb_ref[...])
    bg = pb[HALO:HALO + tm, 0:B_WIDTH]
    cghx = pb[:, B_WIDTH:2 * B_WIDTH] * pb[:, 2 * B_WIDTH:3 * B_WIDTH]
    yb_ref[...] = bg * _dwconv3_ext(cghx, cb_ref[...], tm)


def _inproj(x, pe, mod3, row_base, row_stride, seq_len, tm, ng, wts):
    n_rows = x.shape[0]
    has_pe = pe is not None
    in_specs = _tile_specs(tm, D_MODEL, n_rows)
    args = [x, x, x]
    if has_pe:
        in_specs += _pe_specs(tm, D_MODEL, seq_len)
        args += [pe, pe, pe]
    in_specs += [_mod_spec(tm, seq_len, row_base, row_stride), _const_spec((1, D_MODEL)),
                 _const_spec((D_MODEL, QKV_END)), _const_spec((D_MODEL, A_WIDTH)),
                 _const_spec((D_MODEL, 3 * B_WIDTH)), _const_spec((D_MODEL, N_SMALL)),
                 _const_spec((N_SMALL, D_MODEL)), _const_spec((3, QKV_END)),
                 _const_spec((3, B_WIDTH)), _const_spec((2, N_SMALL)), _const_spec((N_SMALL, 2))]
    args += [mod3, ng, wts["w_qkv"], wts["w_gate"], wts["w_b"], wts["w_small_c"], wts["w_small_r"],
             wts["conv_qkv"], wts["conv_b"], wts["p_col"], wts["p_row"]]
    wide = pl.BlockSpec((tm, A_WIDTH), lambda i: (i, 0))
    out_specs = [wide, wide, wide, wide,
                 pl.BlockSpec((tm, N_SMALL), lambda i: (i, 0)),
                 pl.BlockSpec((N_SMALL, tm), lambda i: (0, i)),
                 wide]
    wide_shape = jax.ShapeDtypeStruct((n_rows, A_WIDTH), F32)
    out_shape = [wide_shape] * 4 + [jax.ShapeDtypeStruct((n_rows, N_SMALL), F32),
                                    jax.ShapeDtypeStruct((N_SMALL, n_rows), F32), wide_shape]
    return pl.pallas_call(
        functools.partial(_inproj_kernel, tm=tm, seq_len=seq_len, has_pe=has_pe),
        grid=(n_rows // tm,), in_specs=in_specs, out_specs=out_specs, out_shape=out_shape,
        compiler_params=_params(1), name="inproj",
    )(*args)


def _dot3_right(a_exact, b):
    b1, b2, b3 = _split3(b)
    return _dot(a_exact, b1) + (_dot(a_exact, b2) + _dot(a_exact, b3))


def _dot3_left(a, b_exact):
    a1, a2, a3 = _split3(a)
    return _dot(a1, b_exact) + (_dot(a2, b_exact) + _dot(a3, b_exact))


def _chunk_local_kernel(q_ref, k_ref, v_ref, gbc_ref, gbr_ref,
                        u_ref, w_ref, qg_ref, kg_ref, at_ref, gl_ref, *, n_sub):
    c = CHUNK
    n_beta = N_DIR * H_A
    r1 = lax.broadcasted_iota(jnp.int32, (c, c), 0)
    c1 = lax.broadcasted_iota(jnp.int32, (c, c), 1)
    tri_lo = jnp.where(r1 >= c1, 1.0, 0.0).astype(BF16)
    tri_up = jnp.where(r1 <= c1, 1.0, 0.0).astype(BF16)
    r2 = lax.broadcasted_iota(jnp.int32, (2 * c, 2 * c), 0)
    c2 = lax.broadcasted_iota(jnp.int32, (2 * c, 2 * c), 1)
    x2 = r2 ^ c2
    same_head = x2 < c
    eye = jnp.where(r2 == c2, 1.0, 0.0).astype(F32)

    def col_pair(a, j):
        return jnp.concatenate([a[:, j:j + 1], a[:, j + 1:j + 2]], axis=0)

    def row_pair(a, j):
        return jnp.concatenate([a[j:j + 1, :], a[j + 1:j + 2, :]], axis=1)

    chains = []
    for ci in range(n_sub):
        rows_ci = slice(ci * c, (ci + 1) * c)
        gbc = gbc_ref[rows_ci, :]
        gbr = gbr_ref[:, rows_ci]
        pre_c = _dot3_right(tri_lo, gbc)
        suf_c = _dot3_right(tri_up, gbc)
        lane = lax.broadcasted_iota(jnp.int32, gbc.shape, 1)
        fwd_c = lane < n_beta + H_A
        gc_c = jnp.where(fwd_c, pre_c, suf_c)
        rest_c = jnp.where(fwd_c, suf_c, pre_c) - gbc
        eg_c = jnp.exp(gc_c)
        er_c = jnp.exp(rest_c)
        pre_r = _dot3_left(gbr, tri_up)
        suf_r = _dot3_left(gbr, tri_lo)
        sub = lax.broadcasted_iota(jnp.int32, gbr.shape, 0)
        gc_r = jnp.where(sub < n_beta + H_A, pre_r, suf_r)
        total = jnp.sum(gbr, axis=1, keepdims=True)
        gl_ref[ci] = jnp.broadcast_to(jnp.exp(total), (2 * n_beta, c))[n_beta:, :]

        q = q_ref[rows_ci, :]
        k = k_ref[rows_ci, :]
        v = v_ref[rows_ci, :]
        for hp in range(H_A // 2):
            h0 = 2 * hp

            def stack(a):
                return jnp.concatenate([a[:, h0 * DK:(h0 + 1) * DK],
                                        a[:, (h0 + 1) * DK:(h0 + 2) * DK]], axis=0)

            q2, k2, v2 = stack(q), stack(k), stack(v)
            k2b = k2.astype(BF16)
            kk = _dot_nt(k2b, k2b)
            qk = _dot_nt(q2.astype(BF16), k2b)
            for d in range(N_DIR):
                jb = d * H_A + h0
                jg = n_beta + jb
                gcc = col_pair(gc_c, jg)
                egc = col_pair(eg_c, jg)
                erc = col_pair(er_c, jg)
                bc = col_pair(gbc, jb)
                gcr = row_pair(gc_r, jg)
                tri = (r2 >= c2) if d == 0 else (r2 <= c2)
                decay = jnp.exp(jnp.where(same_head & tri, gcc - gcr, NEG_BIG))
                attn = qk * decay
                m = jnp.where(r2 != c2, (bc * kk) * decay, 0.0)
                rhs = jnp.concatenate([v2 * bc, k2 * (bc * egc)], axis=1).astype(BF16)
                qg = q2 * egc
                kg = k2 * erc
                for hh in range(2):
                    rows = slice(hh * c, (hh + 1) * c)
                    cols = slice((h0 + hh) * DK, (h0 + hh + 1) * DK)
                    qg_ref[d, rows_ci, cols] = qg[rows].astype(BF16)
                    kg_ref[d, rows_ci, cols] = kg[rows].astype(BF16)
                    at_ref[d, rows_ci, cols] = attn[rows, hh * c:(hh + 1) * c].astype(BF16)
                chains.append({"ci": ci, "h0": h0, "d": d, "m": m, "rhs": rhs,
                               "t": eye - jnp.where(x2 == 1, m, 0.0)})

    def pick(a, b, parity):
        return jnp.concatenate([a[i * b:(i + 1) * b] for i in range(a.shape[0] // b)
                                if i % 2 == parity], axis=0)

    def weave(sel, rest, b, parity):
        blocks = []
        for i in range(2 * sel.shape[0] // b):
            src = sel if i % 2 == parity else rest
            blocks.append(src[(i // 2) * b:(i // 2 + 1) * b])
        return jnp.concatenate(blocks, axis=0)

    for lvl in range(1, 7):
        b = 1 << lvl
        tbs = [ch["t"].astype(BF16) for ch in chains]
        if b < 16:
            level = (x2 >> lvl) == 1
            xs = [_dot(jnp.where(level, ch["m"], 0.0).astype(BF16), tb)
                  for ch, tb in zip(chains, tbs)]
            for ch, tb, x in zip(chains, tbs, xs):
                ch["t"] = ch["t"] - _dot(tb, x.astype(BF16))
        else:
            par = [1 - ch["d"] for ch in chains]
            xs = []
            for ch, tb, p in zip(chains, tbs, par):
                off = jnp.where((pick(x2, b, p) >> lvl) == 1, pick(ch["m"], b, p), 0.0)
                xs.append(_dot(off.astype(BF16), tb))
            for ch, tb, x, p in zip(chains, tbs, xs, par):
                xb = x.astype(BF16)
                x_full = weave(xb, jnp.zeros_like(xb), b, p)
                t_new = pick(ch["t"], b, p) - _dot(pick(tb, b, p), x_full)
                ch["t"] = weave(t_new, pick(ch["t"], b, 1 - p), b, p)

    for ch in chains:
        uw = _dot(ch["t"].astype(BF16), ch["rhs"])
        rows_ci = slice(ch["ci"] * c, (ch["ci"] + 1) * c)
        for hh in range(2):
            rows = slice(hh * c, (hh + 1) * c)
            cols = slice((ch["h0"] + hh) * DK, (ch["h0"] + hh + 1) * DK)
            u_ref[ch["d"], rows_ci, cols] = uw[rows, 0:DV]
            w_ref[ch["d"], rows_ci, cols] = uw[rows, DV:2 * DV].astype(BF16)


def _chunk_local(q, k, v, gbc, gbr, n_sub=2):
    n_rows = q.shape[0]
    n_chunks = n_rows // CHUNK
    rows = n_sub * CHUNK
    wide = pl.BlockSpec((rows, A_WIDTH), lambda i: (i, 0))
    out_wide = pl.BlockSpec((N_DIR, rows, A_WIDTH), lambda i: (0, i, 0))
    big = lambda dt: jax.ShapeDtypeStruct((N_DIR, n_rows, A_WIDTH), dt)
    return pl.pallas_call(
        functools.partial(_chunk_local_kernel, n_sub=n_sub),
        grid=(n_chunks // n_sub,),
        in_specs=[wide, wide, wide,
                  pl.BlockSpec((rows, N_SMALL), lambda i: (i, 0)),
                  pl.BlockSpec((N_SMALL, rows), lambda i: (0, i))],
        out_specs=[out_wide] * 5 + [pl.BlockSpec((n_sub, N_DIR * H_A, CHUNK), lambda i: (i, 0, 0))],
        out_shape=[big(F32), big(BF16), big(BF16), big(BF16), big(BF16),
                   jax.ShapeDtypeStruct((n_chunks, N_DIR * H_A, CHUNK), F32)],
        compiler_params=_params(1), name="chunk_local",
    )(q, k, v, gbc, gbr)


def _scan_kernel(*refs, n_chunks, zero_init, write_final):
    fwd = refs[0:6]
    bwd = refs[6:12]
    refs = refs[12:]
    if not zero_init:
        s0_ref = refs[0]
        refs = refs[1:]
    of_ref, ob_ref = refs[0:2]
    refs = refs[2:]
    if write_final:
        sfin_ref = refs[0]
        refs = refs[1:]
    s_scr = refs[0]
    i = pl.program_id(1)

    @pl.when(i == 0)
    def _():
        if zero_init:
            s_scr[...] = jnp.zeros(s_scr.shape, F32)
        else:
            s_scr[...] = s0_ref[...]

    problems = [(s, d, h) for s in range(SEQ_GROUP) for d in range(N_DIR) for h in range(H_A)]
    c = CHUNK
    stage1 = []
    for s, d, h in problems:
        u_ref, w_ref, qg_ref, _, _, _ = fwd if d == 0 else bwd
        cols = slice(h * DV, (h + 1) * DV)
        sb = s_scr[s, d, h].astype(BF16)
        lhs = jnp.concatenate([w_ref[0, s, :, cols], qg_ref[0, s, :, cols]], axis=0)
        stage1.append(_dot(lhs, sb))
    for (s, d, h), ws_qs in zip(problems, stage1):
        u_ref, _, _, kg_ref, at_ref, gl_ref = fwd if d == 0 else bwd
        o_ref = of_ref if d == 0 else ob_ref
        cols = slice(h * DV, (h + 1) * DV)
        vb = (u_ref[0, s, :, cols] - ws_qs[0:c]).astype(BF16)
        o_ref[s, :, cols] = ws_qs[c:2 * c] + _dot(at_ref[0, s, :, cols], vb)
        gl = gl_ref[s, 0, d * H_A + h:d * H_A + h + 1, :]
        s_scr[s, d, h] = s_scr[s, d, h] * gl + _dot_tn(kg_ref[0, s, :, cols], vb)

    if write_final:
        @pl.when(i == n_chunks - 1)
        def _():
            sfin_ref[...] = s_scr[...]


def _scan(local, s0, n_seq, seq_len, write_final):
    u, w, qg, kg, at, gl = local
    n_chunks = seq_len // CHUNK
    zero_init = s0 is None
    shp4 = (N_DIR, n_seq, seq_len, A_WIDTH)
    arrs = [a.reshape(shp4) for a in (u, w, qg, kg, at)]
    gl4 = gl.reshape(n_seq, n_chunks, N_DIR * H_A, CHUNK)
    sg = SEQ_GROUP

    def specs(d, chunk_of):
        big = pl.BlockSpec((1, sg, CHUNK, A_WIDTH), lambda g, i: (d, g, chunk_of(i), 0))
        return [big] * 5 + [pl.BlockSpec((sg, 1, N_DIR * H_A, CHUNK),
                                         lambda g, i: (g, chunk_of(i), 0, 0))]

    rev = lambda i: n_chunks - 1 - i
    in_specs = specs(0, lambda i: i) + specs(1, rev)
    args = arrs + [gl4] + arrs + [gl4]
    state_spec = pl.BlockSpec((sg, N_DIR, H_A, DK, DV), lambda g, i: (g, 0, 0, 0, 0))
    if not zero_init:
        in_specs.append(state_spec)
        args.append(s0)
    o_shape = jax.ShapeDtypeStruct((n_seq, seq_len, A_WIDTH), F32)
    out_specs = [pl.BlockSpec((sg, CHUNK, A_WIDTH), lambda g, i: (g, i, 0)),
                 pl.BlockSpec((sg, CHUNK, A_WIDTH), lambda g, i: (g, rev(i), 0))]
    out_shape = [o_shape, o_shape]
    if write_final:
        out_specs.append(state_spec)
        out_shape.append(jax.ShapeDtypeStruct((n_seq, N_DIR, H_A, DK, DV), F32))
    return pl.pallas_call(
        functools.partial(_scan_kernel, n_chunks=n_chunks, zero_init=zero_init,
                          write_final=write_final),
        grid=(n_seq // sg, n_chunks), in_specs=in_specs, out_specs=out_specs, out_shape=out_shape,
        scratch_shapes=[pltpu.VMEM((sg, N_DIR, H_A, DK, DV), F32)],
        compiler_params=_params(2, ("parallel", "arbitrary")), name="delta_scan",
    )(*args)


def _outproj_kernel(*refs, has_pe):
    if has_pe:
        x_ref, pe_ref = refs[:2]
        refs = refs[2:]
    else:
        x_ref = refs[0]
        refs = refs[1:]
    of_ref, ob_ref, gate_ref, yb_ref, mod_ref, og_ref, wout_ref, out_ref = refs
    o = of_ref[...] + ob_ref[...]
    sg = _silu(gate_ref[...])
    og = og_ref[...]
    parts = []
    for h in range(H_A):
        cols = slice(h * DV, (h + 1) * DV)
        oh = o[:, cols]
        ms = jnp.mean(oh * oh, axis=-1, keepdims=True)
        parts.append((oh * lax.rsqrt(ms + EPS)) * og * sg[:, cols])
    a = jnp.concatenate(parts + [yb_ref[...]], axis=1).astype(BF16)
    y = _dot(a, wout_ref[...])
    x = x_ref[...]
    if has_pe:
        x = x + pe_ref[...]
    gate1 = mod_ref[0][:, 2 * D_MODEL:3 * D_MODEL]
    out_ref[...] = x + gate1 * y


def _outproj(x, pe, o_f, o_b, gate, yb, mod3, row_base, row_stride, seq_len, tm, og, w_out):
    n_rows = x.shape[0]
    has_pe = pe is not None
    full = pl.BlockSpec((tm, D_MODEL), lambda i: (i, 0))
    half = pl.BlockSpec((tm, A_WIDTH), lambda i: (i, 0))
    in_specs = [full]
    args = [x]
    if has_pe:
        tiles = seq_len // tm
        in_specs.append(pl.BlockSpec((tm, D_MODEL), lambda i: (i % tiles, 0)))
        args.append(pe)
    in_specs += [half, half, half, half, _mod_spec(tm, seq_len, row_base, row_stride),
                 _const_spec((1, DV)), _const_spec((D_MODEL, D_MODEL))]
    args += [o_f, o_b, gate, yb, mod3, og, w_out]
    return pl.pallas_call(
        functools.partial(_outproj_kernel, has_pe=has_pe),
        grid=(n_rows // tm,), in_specs=in_specs, out_specs=full,
        out_shape=jax.ShapeDtypeStruct((n_rows, D_MODEL), F32),
        compiler_params=_params(1), name="outproj",
    )(*args)


def _ffn_kernel(*refs, tm, seq_len, final_norm):
    xp, xm, xn, mod_ref, ng_ref, wg_ref, wu_ref, wd_ref, cf_ref = refs[:9]
    refs = refs[9:]
    if final_norm:
        fg_ref = refs[0]
        refs = refs[1:]
    out_ref = refs[0]
    vp, vn = _halo_valid(tm, seq_len)
    mod = mod_ref[0]
    shift = mod[:, 3 * D_MODEL:4 * D_MODEL]
    scale = mod[:, 4 * D_MODEL:5 * D_MODEL]
    gate2 = mod[:, 5 * D_MODEL:6 * D_MODEL]
    g = ng_ref[...]
    x = xm[...]
    h_main = _norm_mod(x, g, scale, shift)
    h_ext = jnp.concatenate([_norm_mod(xp[...], g, scale, shift) * vp, h_main,
                             _norm_mod(xn[...], g, scale, shift) * vn], axis=0)
    hb_ext = h_ext.astype(BF16)
    hb = h_main.astype(BF16)
    edge_masks = _seq_edge_masks(tm, seq_len, FF_CHUNK)
    acc = jnp.zeros((tm, D_MODEL), F32)
    for j in range(N_FF_CHUNKS):
        cols = slice(j * FF_CHUNK, (j + 1) * FF_CHUNK)
        u = _dwconv3_ext(_dot(hb_ext, wg_ref[:, cols]), cf_ref[:, cols], tm, edge_masks)
        a = (_silu(u) * _dot(hb, wu_ref[:, cols])).astype(BF16)
        acc = acc + _dot(a, wd_ref[cols, :])
    y = x + gate2 * acc
    if final_norm:
        ms = jnp.mean(y * y, axis=-1, keepdims=True)
        y = (y * lax.rsqrt(ms + EPS)) * fg_ref[...]
    out_ref[...] = y


def _ffn(x, mod3, row_base, row_stride, seq_len, tm, ng, wts, final_g):
    n_rows = x.shape[0]
    final_norm = final_g is not None
    in_specs = _tile_specs(tm, D_MODEL, n_rows)
    in_specs += [_mod_spec(tm, seq_len, row_base, row_stride), _const_spec((1, D_MODEL)),
                 _const_spec((D_MODEL, D_FF)), _const_spec((D_MODEL, D_FF)),
                 _const_spec((D_FF, D_MODEL)), _const_spec((3, D_FF))]
    args = [x, x, x, mod3, ng, wts["w_gate"], wts["w_up"], wts["w_down"], wts["conv"]]
    if final_norm:
        in_specs.append(_const_spec((1, D_MODEL)))
        args.append(final_g)
    return pl.pallas_call(
        functools.partial(_ffn_kernel, tm=tm, seq_len=seq_len, final_norm=final_norm),
        grid=(n_rows // tm,), in_specs=in_specs,
        out_specs=pl.BlockSpec((tm, D_MODEL), lambda i: (i, 0)),
        out_shape=jax.ShapeDtypeStruct((n_rows, D_MODEL), F32),
        compiler_params=_params(1), name="conv_ffn",
    )(*args)


def _pool_kernel(xp, xm, xn, mod_ref, ng_ref, wp_ref, ps_ref, out_ref, *, tm, seq_len):
    vp, vn = _halo_valid(tm, seq_len)
    mod = mod_ref[0]
    shift = mod[:, 0:D_MODEL]
    scale = mod[:, D_MODEL:2 * D_MODEL]
    gate1 = mod[:, 2 * D_MODEL:3 * D_MODEL]
    g = ng_ref[...]
    x = xm[...]
    h_main = _norm_mod(x, g, scale, shift)
    h_ext = jnp.concatenate([_norm_mod(xp[...], g, scale, shift) * vp, h_main,
                             _norm_mod(xn[...], g, scale, shift) * vn], axis=0)
    i = pl.program_id(0)
    pos = (i * tm + lax.broadcasted_iota(jnp.int32, (tm, POOL_GROUP), 0)) % seq_len
    ys = []
    for gi, win in enumerate(POOL_WINDOWS):
        cols = slice(gi * POOL_GROUP, (gi + 1) * POOL_GROUP)
        s = h_ext[:, cols]
        width = 1
        while width < win:
            s = s + _shift_rows(s, width)
            width *= 2
        start = HALO - win // 2
        if start:
            s = _shift_rows(s, start)
        s = s[0:tm]
        lo = jnp.maximum(pos - win // 2, 0)
        hi = jnp.minimum(pos - win // 2 + win, seq_len)
        mixed = (s / (hi - lo).astype(F32) - h_main[:, cols]).astype(BF16)
        ys.append(_dot(mixed, wp_ref[gi]))
    y = jnp.concatenate(ys, axis=1) * ps_ref[...]
    out_ref[...] = x + gate1 * y


def _pool(x, mod3, row_base, row_stride, seq_len, tm, ng, w_pool, pool_scale):
    n_rows = x.shape[0]
    in_specs = _tile_specs(tm, D_MODEL, n_rows)
    in_specs += [_mod_spec(tm, seq_len, row_base, row_stride), _const_spec((1, D_MODEL)),
                 _const_spec((N_POOL_GROUPS, POOL_GROUP, POOL_GROUP)), _const_spec((1, D_MODEL))]
    return pl.pallas_call(
        functools.partial(_pool_kernel, tm=tm, seq_len=seq_len),
        grid=(n_rows // tm,), in_specs=in_specs,
        out_specs=pl.BlockSpec((tm, D_MODEL), lambda i: (i, 0)),
        out_shape=jax.ShapeDtypeStruct((n_rows, D_MODEL), F32),
        compiler_params=_params(1), name="pool_mixer",
    )(x, x, x, mod3, ng, w_pool, pool_scale)


def _sincos_2d(rows, cols, d):
    quarter = d // 4
    omega = 1.0 / (10000.0 ** (jnp.arange(quarter, dtype=F32) / quarter))
    er = jnp.arange(rows, dtype=F32)[:, None] * omega[None, :]
    ec = jnp.arange(cols, dtype=F32)[:, None] * omega[None, :]
    er = jnp.concatenate([jnp.sin(er), jnp.cos(er)], axis=-1)
    ec = jnp.concatenate([jnp.sin(ec), jnp.cos(ec)], axis=-1)
    pe = jnp.concatenate([jnp.broadcast_to(er[:, None, :], (rows, cols, d // 2)),
                          jnp.broadcast_to(ec[None, :, :], (rows, cols, d // 2))], axis=-1)
    return pe.reshape(rows * cols, d)


def _even_layer_weights(w_in, conv_qkv, a_log, dt_bias, conv_b, w_out):
    small = w_in[:, GATE_END:ALPHA_END]
    zeros = jnp.zeros((N_DIR * H_A,), F32)
    ea = jnp.concatenate([zeros, jnp.exp(a_log.astype(F32)).reshape(-1)])
    dtb = jnp.concatenate([zeros, dt_bias.astype(F32).reshape(-1)])
    p_col = jnp.stack([ea, dtb], axis=0)
    return {
        "w_qkv": w_in[:, :QKV_END].astype(BF16),
        "w_gate": w_in[:, QKV_END:GATE_END].astype(BF16),
        "w_b": w_in[:, ALPHA_END:IN_AB].astype(BF16),
        "w_small_c": small.astype(BF16),
        "w_small_r": small.T.astype(BF16),
        "conv_qkv": conv_qkv, "conv_b": conv_b,
        "p_col": p_col, "p_row": p_col.T,
        "w_out": w_out.astype(BF16),
    }


def _ffn_weights(w_gate, w_up, conv, w_down):
    return {"w_gate": w_gate.astype(BF16), "w_up": w_up.astype(BF16),
            "w_down": w_down.astype(BF16), "conv": conv}


def _trunk(x, pe, mod, row_base, row_stride, n_seq, seq_len, tm, tm_ffn, s0, write_final, p):
    row = lambda a: a.reshape(1, -1)
    finals = None
    depth = mod.shape[0]
    for layer in range(depth):
        mod3 = mod[layer][:, None, :]
        place = (mod3, row_base, row_stride, seq_len, tm)
        place_ffn = (mod3, row_base, row_stride, seq_len, tm_ffn)
        if layer % 2 == 0:
            e = layer // 2
            wts = p["even"][e]
            q, k, v, gate, gbc, gbr, yb = _inproj(x, pe, *place, row(p["norm_mix_g"][layer]), wts)
            local = _chunk_local(q, k, v, gbc, gbr)
            res = _scan(local, None if s0 is None else s0[:, e], n_seq, seq_len, write_final)
            o_f = res[0].reshape(-1, A_WIDTH)
            o_b = res[1].reshape(-1, A_WIDTH)
            if write_final:
                finals = res[2]
            x = _outproj(x, pe, o_f, o_b, gate, yb, *place, row(p["o_norm_g"][e]), wts["w_out"])
        else:
            o = layer // 2
            x = _pool(x, *place, row(p["norm_mix_g"][layer]), p["w_pool"][o],
                      row(p["pool_scale"][o]))
        final_g = row(p["final_norm_g"]) if layer == depth - 1 else None
        x = _ffn(x, *place_ffn, row(p["norm_ffn_g"][layer]), p["ffn"][layer], final_g)
    return x, finals


def kernel(x_prompt, x_sample, state_delta, c, c_ctx, norm_mix_g, norm_ffn_g, w_ada, b_ada,
           w_in_ab, conv_qkv, a_log, dt_bias, o_norm_g, conv_b, w_out_ab, w_pool, pool_scale,
           w_ffn_gate, w_ffn_up, ffn_conv, w_ffn_down, final_norm_g):
    batch, seq, d = x_prompt.shape
    dec_batch, dec_seq, _ = x_sample.shape
    depth = w_ada.shape[0]
    n_even = w_in_ab.shape[0]
    assert d == D_MODEL and depth == 2 and n_even == 1

    p = {
        "norm_mix_g": norm_mix_g, "norm_ffn_g": norm_ffn_g, "o_norm_g": o_norm_g,
        "pool_scale": pool_scale, "final_norm_g": final_norm_g,
        "w_pool": w_pool.astype(BF16),
        "even": [_even_layer_weights(w_in_ab[e], conv_qkv[e], a_log[e], dt_bias[e], conv_b[e],
                                     w_out_ab[e]) for e in range(n_even)],
        "ffn": [_ffn_weights(w_ffn_gate[l], w_ffn_up[l], ffn_conv[l], w_ffn_down[l])
                for l in range(depth)],
    }

    n_cond = 1 + dec_batch
    pad = (-n_cond) % HALO
    cond = jnp.concatenate([c_ctx[None, :], c, jnp.zeros((pad, d), F32)], axis=0)
    mod = _modulation(cond, w_ada, b_ada)

    y_prompt, finals = _trunk(x_prompt.reshape(batch * seq, d), None, mod, 0, 0, batch, seq,
                              min(seq, 256), 1024, None, True, p)
    pe = _sincos_2d(dec_seq // GRID_W, GRID_W, d)
    y_sample, _ = _trunk(x_sample.reshape(dec_batch * dec_seq, d), pe, mod, 1, 1, dec_batch,
                         dec_seq, 512, 1024, state_delta, False, p)
    return (y_prompt.reshape(batch, seq, d), y_sample.reshape(dec_batch, dec_seq, d),
            finals[:, None])
```

```python
import functools

import jax
import jax.numpy as jnp
from jax import lax
from jax.experimental import pallas as pl
from jax.experimental.pallas import tpu as pltpu

F32 = jnp.float32
BF16 = jnp.bfloat16

D_MODEL = 1024
GRID_W = 64
A_WIDTH = 512
B_WIDTH = 512
DK = 128
DV = 128
H_A = 4
N_DIR = 2
N_POOL_GROUPS = 4
POOL_GROUP = D_MODEL // N_POOL_GROUPS
POOL_WINDOWS = (2, 4, 8, 16)
D_FF = 2816
N_MOD = 6
EPS = 1e-6
QKV_END = 3 * A_WIDTH
GATE_END = QKV_END + A_WIDTH
BETA_END = GATE_END + N_DIR * H_A
ALPHA_END = BETA_END + N_DIR * H_A
BG_END = ALPHA_END + B_WIDTH
CG_END = BG_END + B_WIDTH
IN_AB = CG_END + B_WIDTH

HALO = 8
CHUNK = 128
FF_CHUNK = 256
N_FF_CHUNKS = D_FF // FF_CHUNK
N_SMALL = 2 * N_DIR * H_A
SEQ_GROUP = 4
VMEM_LIMIT = 56 * 1024 * 1024
NEG_BIG = -1e30


def _sigmoid(x):
    return 1.0 / (1.0 + jnp.exp(-x))


def _silu(x):
    return x * _sigmoid(x)


def _softplus(x):
    return jnp.maximum(x, 0.0) + jnp.log1p(jnp.exp(-jnp.abs(x)))


def _dot(a, b):
    return jnp.dot(a, b, preferred_element_type=F32)


def _dot_nt(a, b):
    return lax.dot_general(a, b, (((1,), (1,)), ((), ())), preferred_element_type=F32)


def _dot_tn(a, b):
    return lax.dot_general(a, b, (((0,), (0,)), ((), ())), preferred_element_type=F32)


def _split3(a):
    a1 = a.astype(BF16)
    r1 = a - a1.astype(F32)
    a2 = r1.astype(BF16)
    a3 = (r1 - a2.astype(F32)).astype(BF16)
    return a1, a2, a3


def _norm_mod(x, g, scale, shift):
    ms = jnp.mean(x * x, axis=-1, keepdims=True)
    return (x * lax.rsqrt(ms + EPS)) * g * (1.0 + scale) + shift


def _shift_rows(a, k):
    n = a.shape[0]
    return pltpu.roll(a, (-k) % n, axis=0)


def _dwconv3_ext(p_ext, w, tm, edge_masks=None):
    prev = _shift_rows(p_ext, -1)[HALO:HALO + tm]
    cur = p_ext[HALO:HALO + tm]
    nxt = _shift_rows(p_ext, 1)[HALO:HALO + tm]
    if edge_masks is not None:
        prev = prev * edge_masks[0]
        nxt = nxt * edge_masks[1]
    return prev * w[0:1] + cur * w[1:2] + nxt * w[2:3]


def _seq_edge_masks(tm, seq_len, width):
    if tm <= seq_len:
        return None
    pos = lax.broadcasted_iota(jnp.int32, (tm, width), 0) % seq_len
    return (jnp.where(pos != 0, 1.0, 0.0).astype(F32),
            jnp.where(pos != seq_len - 1, 1.0, 0.0).astype(F32))


def _halo_valid(tm, seq_len):
    i = pl.program_id(0)
    vp = jnp.where((i * tm) % seq_len != 0, 1.0, 0.0).astype(F32)
    vn = jnp.where(((i + 1) * tm) % seq_len != 0, 1.0, 0.0).astype(F32)
    return vp, vn


def _const_spec(shape):
    nd = len(shape)
    return pl.BlockSpec(shape, lambda *_: (0,) * nd)


def _tile_specs(tm, width, n_rows):
    per = tm // HALO
    last = n_rows // HALO - 1
    prev = pl.BlockSpec((HALO, width), lambda i: (jnp.maximum(i * per - 1, 0), 0))
    main = pl.BlockSpec((tm, width), lambda i: (i, 0))
    nxt = pl.BlockSpec((HALO, width), lambda i: (jnp.minimum((i + 1) * per, last), 0))
    return [prev, main, nxt]


def _pe_specs(tm, width, seq_len):
    per = tm // HALO
    tiles = seq_len // tm
    last = seq_len // HALO - 1
    prev = pl.BlockSpec((HALO, width), lambda i: (jnp.maximum((i % tiles) * per - 1, 0), 0))
    main = pl.BlockSpec((tm, width), lambda i: (i % tiles, 0))
    nxt = pl.BlockSpec((HALO, width), lambda i: (jnp.minimum((i % tiles + 1) * per, last), 0))
    return [prev, main, nxt]


def _mod_spec(tm, seq_len, row_base, row_stride):
    return pl.BlockSpec((1, 1, N_MOD * D_MODEL),
                        lambda i: (row_base + ((i * tm) // seq_len) * row_stride, 0, 0))


def _params(n_axes=1, semantics=None):
    return pltpu.CompilerParams(
        dimension_semantics=semantics or ("parallel",) * n_axes,
        vmem_limit_bytes=VMEM_LIMIT)


def _mod_kernel(c_ref, w_ref, b_ref, o_ref):
    c = c_ref[...]
    s1, s2, _ = _split3(_silu(c))
    w = w_ref[0]
    w1 = w.astype(BF16)
    w2 = (w - w1.astype(F32)).astype(BF16)
    o_ref[0] = _dot(s1, w1) + (_dot(s2, w1) + _dot(s1, w2)) + b_ref[0]


def _modulation(cond, w_ada, b_ada):
    depth, _, width = w_ada.shape
    rows = cond.shape[0]
    tn = 1536
    return pl.pallas_call(
        _mod_kernel,
        grid=(depth, width // tn),
        in_specs=[pl.BlockSpec((rows, D_MODEL), lambda l, j: (0, 0)),
                  pl.BlockSpec((1, D_MODEL, tn), lambda l, j: (l, 0, j)),
                  pl.BlockSpec((1, 1, tn), lambda l, j: (l, 0, j))],
        out_specs=pl.BlockSpec((1, rows, tn), lambda l, j: (l, 0, j)),
        out_shape=jax.ShapeDtypeStruct((depth, rows, width), F32),
        compiler_params=_params(2),
        name="modulation",
    )(cond, w_ada, b_ada.reshape(depth, 1, width))


def _inproj_kernel(*refs, tm, seq_len, has_pe):
    if has_pe:
        xp, xm, xn, pp, pm, pn = refs[:6]
        refs = refs[6:]
    else:
        xp, xm, xn = refs[:3]
        pp = pm = pn = None
        refs = refs[3:]
    (mod_ref, ng_ref, wqkv_ref, wgate_ref, wb_ref, wsc_ref, wsr_ref, cq_ref, cb_ref,
     pc_ref, pr_ref, q_ref, k_ref, v_ref, gate_ref, gbc_ref, gbr_ref, yb_ref) = refs

    vp, vn = _halo_valid(tm, seq_len)
    mod = mod_ref[0]
    shift = mod[:, 0:D_MODEL]
    scale = mod[:, D_MODEL:2 * D_MODEL]
    g = ng_ref[...]

    def prep(x_ref, pe_ref):
        x = x_ref[...]
        if has_pe:
            x = x + pe_ref[...]
        return _norm_mod(x, g, scale, shift)

    h_main = prep(xm, pm)
    h_ext = jnp.concatenate([prep(xp, pp) * vp, h_main, prep(xn, pn) * vn], axis=0)
    hb_ext = h_ext.astype(BF16)
    hb = h_main.astype(BF16)

    cq = cq_ref[...]
    outs = (q_ref, k_ref, v_ref)
    for part in range(3):
        cols = slice(part * A_WIDTH, (part + 1) * A_WIDTH)
        p = _dot(hb_ext, wqkv_ref[:, cols])
        a = _silu(_dwconv3_ext(p, cq[:, cols], tm))
        if part < 2:
            heads = []
            for h in range(H_A):
                ah = a[:, h * DK:(h + 1) * DK]
                ss = jnp.sum(ah * ah, axis=-1, keepdims=True)
                nrm = lax.rsqrt(ss + EPS)
                if part == 0:
                    nrm = nrm * (DK ** -0.5)
                heads.append(ah * nrm)
            a = jnp.concatenate(heads, axis=1)
        outs[part][...] = a.astype(BF16)

    gate_ref[...] = _dot(hb, wgate_ref[...]).astype(BF16)

    pc = pc_ref[...]
    ba = _dot(hb, wsc_ref[...])
    lane = lax.broadcasted_iota(jnp.int32, ba.shape, 1)
    gbc_ref[...] = jnp.where(lane < N_DIR * H_A, _sigmoid(ba),
                             -pc[0:1] * _softplus(ba + pc[1:2]))
    pr = pr_ref[...]
    bar = _dot_nt(wsr_ref[...], hb)
    sub = lax.broadcasted_iota(jnp.int32, bar.shape, 0)
    gbr_ref[...] = jnp.where(sub < N_DIR * H_A, _sigmoid(bar),
                             -pr[:, 0:1] * _softplus(bar + pr[:, 1:2]))

    pb = _dot(hb_ext, wb_ref[...])
    bg = pb[HALO:HALO + tm, 0:B_WIDTH]
    cghx = pb[:, B_WIDTH:2 * B_WIDTH] * pb[:, 2 * B_WIDTH:3 * B_WIDTH]
    yb_ref[...] = (bg * _dwconv3_ext(cghx, cb_ref[...], tm)).astype(BF16)


def _inproj(x, pe, mod3, row_base, row_stride, seq_len, tm, ng, wts):
    n_rows = x.shape[0]
    has_pe = pe is not None
    in_specs = _tile_specs(tm, D_MODEL, n_rows)
    args = [x, x, x]
    if has_pe:
        in_specs += _pe_specs(tm, D_MODEL, seq_len)
        args += [pe, pe, pe]
    in_specs += [_mod_spec(tm, seq_len, row_base, row_stride), _const_spec((1, D_MODEL)),
                 _const_spec((D_MODEL, QKV_END)), _const_spec((D_MODEL, A_WIDTH)),
                 _const_spec((D_MODEL, 3 * B_WIDTH)), _const_spec((D_MODEL, N_SMALL)),
                 _const_spec((N_SMALL, D_MODEL)), _const_spec((3, QKV_END)),
                 _const_spec((3, B_WIDTH)), _const_spec((2, N_SMALL)), _const_spec((N_SMALL, 2))]
    args += [mod3, ng, wts["w_qkv"], wts["w_gate"], wts["w_b"], wts["w_small_c"], wts["w_small_r"],
             wts["conv_qkv"], wts["conv_b"], wts["p_col"], wts["p_row"]]
    wide = pl.BlockSpec((tm, A_WIDTH), lambda i: (i, 0))
    out_specs = [wide, wide, wide, wide,
                 pl.BlockSpec((tm, N_SMALL), lambda i: (i, 0)),
                 pl.BlockSpec((N_SMALL, tm), lambda i: (0, i)),
                 wide]
    wide_shape = jax.ShapeDtypeStruct((n_rows, A_WIDTH), BF16)
    out_shape = [wide_shape] * 4 + [jax.ShapeDtypeStruct((n_rows, N_SMALL), F32),
                                    jax.ShapeDtypeStruct((N_SMALL, n_rows), F32), wide_shape]
    return pl.pallas_call(
        functools.partial(_inproj_kernel, tm=tm, seq_len=seq_len, has_pe=has_pe),
        grid=(n_rows // tm,), in_specs=in_specs, out_specs=out_specs, out_shape=out_shape,
        compiler_params=_params(1), name="inproj",
    )(*args)


def _dot3_right(a_exact, b):
    b1, b2, b3 = _split3(b)
    return _dot(a_exact, b1) + (_dot(a_exact, b2) + _dot(a_exact, b3))


def _dot3_left(a, b_exact):
    a1, a2, a3 = _split3(a)
    return _dot(a1, b_exact) + (_dot(a2, b_exact) + _dot(a3, b_exact))


def _chunk_local_kernel(q_ref, k_ref, v_ref, gbc_ref, gbr_ref,
                        u_ref, w_ref, qg_ref, kg_ref, at_ref, gl_ref, *, n_sub):
    c = CHUNK
    n_beta = N_DIR * H_A
    r1 = lax.broadcasted_iota(jnp.int32, (c, c), 0)
    c1 = lax.broadcasted_iota(jnp.int32, (c, c), 1)
    tri_lo = jnp.where(r1 >= c1, 1.0, 0.0).astype(BF16)
    tri_up = jnp.where(r1 <= c1, 1.0, 0.0).astype(BF16)
    r2 = lax.broadcasted_iota(jnp.int32, (2 * c, 2 * c), 0)
    c2 = lax.broadcasted_iota(jnp.int32, (2 * c, 2 * c), 1)
    x2 = r2 ^ c2
    same_head = x2 < c
    eye = jnp.where(r2 == c2, 1.0, 0.0).astype(F32)

    def col_pair(a, j):
        return jnp.concatenate([a[:, j:j + 1], a[:, j + 1:j + 2]], axis=0)

    def row_pair(a, j):
        return jnp.concatenate([a[j:j + 1, :], a[j + 1:j + 2, :]], axis=1)

    chains = []
    for ci in range(n_sub):
        rows_ci = slice(ci * c, (ci + 1) * c)
        gbc = gbc_ref[rows_ci, :]
        gbr = gbr_ref[:, rows_ci]
        pre_c = _dot3_right(tri_lo, gbc)
        suf_c = _dot3_right(tri_up, gbc)
        lane = lax.broadcasted_iota(jnp.int32, gbc.shape, 1)
        fwd_c = lane < n_beta + H_A
        gc_c = jnp.where(fwd_c, pre_c, suf_c)
        rest_c = jnp.where(fwd_c, suf_c, pre_c) - gbc
        eg_c = jnp.exp(gc_c)
        er_c = jnp.exp(rest_c)
        pre_r = _dot3_left(gbr, tri_up)
        suf_r = _dot3_left(gbr, tri_lo)
        sub = lax.broadcasted_iota(jnp.int32, gbr.shape, 0)
        gc_r = jnp.where(sub < n_beta + H_A, pre_r, suf_r)
        total = jnp.sum(gbr, axis=1, keepdims=True)
        gl_ref[ci] = jnp.broadcast_to(jnp.exp(total), (2 * n_beta, c))[n_beta:, :]

        q = q_ref[rows_ci, :]
        k = k_ref[rows_ci, :]
        v = v_ref[rows_ci, :]
        for hp in range(H_A // 2):
            h0 = 2 * hp

            def stack(a):
                return jnp.concatenate([a[:, h0 * DK:(h0 + 1) * DK],
                                        a[:, (h0 + 1) * DK:(h0 + 2) * DK]], axis=0)

            q2b, k2b, v2b = stack(q), stack(k), stack(v)
            q2, k2, v2 = q2b.astype(F32), k2b.astype(F32), v2b.astype(F32)
            kk = _dot_nt(k2b, k2b)
            qk = _dot_nt(q2b, k2b)
            for d in range(N_DIR):
                jb = d * H_A + h0
                jg = n_beta + jb
                gcc = col_pair(gc_c, jg)
                egc = col_pair(eg_c, jg)
                erc = col_pair(er_c, jg)
                bc = col_pair(gbc, jb)
                gcr = row_pair(gc_r, jg)
                tri = (r2 >= c2) if d == 0 else (r2 <= c2)
                decay = jnp.exp(jnp.where(same_head & tri, gcc - gcr, NEG_BIG))
                attn = qk * decay
                m = jnp.where(r2 != c2, (bc * kk) * decay, 0.0)
                rhs = jnp.concatenate([v2 * bc, k2 * (bc * egc)], axis=1).astype(BF16)
                qg = q2 * egc
                kg = k2 * erc
                for hh in range(2):
                    rows = slice(hh * c, (hh + 1) * c)
                    cols = slice((h0 + hh) * DK, (h0 + hh + 1) * DK)
                    qg_ref[d, rows_ci, cols] = qg[rows].astype(BF16)
                    kg_ref[d, rows_ci, cols] = kg[rows].astype(BF16)
                    at_ref[d, rows_ci, cols] = attn[rows, hh * c:(hh + 1) * c].astype(BF16)
                chains.append({"ci": ci, "h0": h0, "d": d, "m": m, "rhs": rhs,
                               "t": eye - jnp.where(x2 == 1, m, 0.0)})

    def pick(a, b, parity):
        return jnp.concatenate([a[i * b:(i + 1) * b] for i in range(a.shape[0] // b)
                                if i % 2 == parity], axis=0)

    def weave(sel, rest, b, parity):
        blocks = []
        for i in range(2 * sel.shape[0] // b):
            src = sel if i % 2 == parity else rest
            blocks.append(src[(i // 2) * b:(i // 2 + 1) * b])
        return jnp.concatenate(blocks, axis=0)

    for lvl in range(1, 7):
        b = 1 << lvl
        tbs = [ch["t"].astype(BF16) for ch in chains]
        if b < 16:
            level = (x2 >> lvl) == 1
            xs = [_dot(jnp.where(level, ch["m"], 0.0).astype(BF16), tb)
                  for ch, tb in zip(chains, tbs)]
            for ch, tb, x in zip(chains, tbs, xs):
                ch["t"] = ch["t"] - _dot(tb, x.astype(BF16))
        else:
            par = [1 - ch["d"] for ch in chains]
            xs = []
            for ch, tb, p in zip(chains, tbs, par):
                off = jnp.where((pick(x2, b, p) >> lvl) == 1, pick(ch["m"], b, p), 0.0)
                xs.append(_dot(off.astype(BF16), tb))
            for ch, tb, x, p in zip(chains, tbs, xs, par):
                xb = x.astype(BF16)
                x_full = weave(xb, jnp.zeros_like(xb), b, p)
                t_new = pick(ch["t"], b, p) - _dot(pick(tb, b, p), x_full)
                ch["t"] = weave(t_new, pick(ch["t"], b, 1 - p), b, p)

    for ch in chains:
        uw = _dot(ch["t"].astype(BF16), ch["rhs"])
        rows_ci = slice(ch["ci"] * c, (ch["ci"] + 1) * c)
        for hh in range(2):
            rows = slice(hh * c, (hh + 1) * c)
            cols = slice((ch["h0"] + hh) * DK, (ch["h0"] + hh + 1) * DK)
            u_ref[ch["d"], rows_ci, cols] = uw[rows, 0:DV].astype(BF16)
            w_ref[ch["d"], rows_ci, cols] = uw[rows, DV:2 * DV].astype(BF16)


def _chunk_local(q, k, v, gbc, gbr, n_sub=2):
    n_rows = q.shape[0]
    n_chunks = n_rows // CHUNK
    rows = n_sub * CHUNK
    wide = pl.BlockSpec((rows, A_WIDTH), lambda i: (i, 0))
    out_wide = pl.BlockSpec((N_DIR, rows, A_WIDTH), lambda i: (0, i, 0))
    big = lambda dt: jax.ShapeDtypeStruct((N_DIR, n_rows, A_WIDTH), dt)
    return pl.pallas_call(
        functools.partial(_chunk_local_kernel, n_sub=n_sub),
        grid=(n_chunks // n_sub,),
        in_specs=[wide, wide, wide,
                  pl.BlockSpec((rows, N_SMALL), lambda i: (i, 0)),
                  pl.BlockSpec((N_SMALL, rows), lambda i: (0, i))],
        out_specs=[out_wide] * 5 + [pl.BlockSpec((n_sub, N_DIR * H_A, CHUNK), lambda i: (i, 0, 0))],
        out_shape=[big(BF16)] * 5 + [jax.ShapeDtypeStruct((n_chunks, N_DIR * H_A, CHUNK), F32)],
        compiler_params=_params(1), name="chunk_local",
    )(q, k, v, gbc, gbr)


def _scan_kernel(*refs, n_chunks, zero_init, write_final):
    fwd = refs[0:6]
    bwd = refs[6:12]
    refs = refs[12:]
    if not zero_init:
        s0_ref = refs[0]
        refs = refs[1:]
    of_ref, ob_ref = refs[0:2]
    refs = refs[2:]
    if write_final:
        sfin_ref = refs[0]
        refs = refs[1:]
    s_scr = refs[0]
    i = pl.program_id(1)

    @pl.when(i == 0)
    def _():
        if zero_init:
            s_scr[...] = jnp.zeros(s_scr.shape, F32)
        else:
            s_scr[...] = s0_ref[...]

    problems = [(s, d, h) for s in range(SEQ_GROUP) for d in range(N_DIR) for h in range(H_A)]
    c = CHUNK
    stage1 = []
    for s, d, h in problems:
        u_ref, w_ref, qg_ref, _, _, _ = fwd if d == 0 else bwd
        cols = slice(h * DV, (h + 1) * DV)
        sb = s_scr[s, d, h].astype(BF16)
        lhs = jnp.concatenate([w_ref[0, s, :, cols], qg_ref[0, s, :, cols]], axis=0)
        stage1.append(_dot(lhs, sb))
    for (s, d, h), ws_qs in zip(problems, stage1):
        u_ref, _, _, kg_ref, at_ref, gl_ref = fwd if d == 0 else bwd
        o_ref = of_ref if d == 0 else ob_ref
        cols = slice(h * DV, (h + 1) * DV)
        vb = (u_ref[0, s, :, cols].astype(F32) - ws_qs[0:c]).astype(BF16)
        o_ref[s, :, cols] = (ws_qs[c:2 * c] + _dot(at_ref[0, s, :, cols], vb)).astype(BF16)
        gl = gl_ref[s, 0, d * H_A + h:d * H_A + h + 1, :]
        s_scr[s, d, h] = s_scr[s, d, h] * gl + _dot_tn(kg_ref[0, s, :, cols], vb)

    if write_final:
        @pl.when(i == n_chunks - 1)
        def _():
            sfin_ref[...] = s_scr[...]


def _scan(local, s0, n_seq, seq_len, write_final):
    u, w, qg, kg, at, gl = local
    n_chunks = seq_len // CHUNK
    zero_init = s0 is None
    shp4 = (N_DIR, n_seq, seq_len, A_WIDTH)
    arrs = [a.reshape(shp4) for a in (u, w, qg, kg, at)]
    gl4 = gl.reshape(n_seq, n_chunks, N_DIR * H_A, CHUNK)
    sg = SEQ_GROUP

    def specs(d, chunk_of):
        big = pl.BlockSpec((1, sg, CHUNK, A_WIDTH), lambda g, i: (d, g, chunk_of(i), 0))
        return [big] * 5 + [pl.BlockSpec((sg, 1, N_DIR * H_A, CHUNK),
                                         lambda g, i: (g, chunk_of(i), 0, 0))]

    rev = lambda i: n_chunks - 1 - i
    in_specs = specs(0, lambda i: i) + specs(1, rev)
    args = arrs + [gl4] + arrs + [gl4]
    state_spec = pl.BlockSpec((sg, N_DIR, H_A, DK, DV), lambda g, i: (g, 0, 0, 0, 0))
    if not zero_init:
        in_specs.append(state_spec)
        args.append(s0)
    o_shape = jax.ShapeDtypeStruct((n_seq, seq_len, A_WIDTH), BF16)
    out_specs = [pl.BlockSpec((sg, CHUNK, A_WIDTH), lambda g, i: (g, i, 0)),
                 pl.BlockSpec((sg, CHUNK, A_WIDTH), lambda g, i: (g, rev(i), 0))]
    out_shape = [o_shape, o_shape]
    if write_final:
        out_specs.append(state_spec)
        out_shape.append(jax.ShapeDtypeStruct((n_seq, N_DIR, H_A, DK, DV), F32))
    return pl.pallas_call(
        functools.partial(_scan_kernel, n_chunks=n_chunks, zero_init=zero_init,
                          write_final=write_final),
        grid=(n_seq // sg, n_chunks), in_specs=in_specs, out_specs=out_specs, out_shape=out_shape,
        scratch_shapes=[pltpu.VMEM((sg, N_DIR, H_A, DK, DV), F32)],
        compiler_params=_params(2, ("parallel", "arbitrary")), name="delta_scan",
    )(*args)


def _outproj_kernel(*refs, has_pe):
    if has_pe:
        x_ref, pe_ref = refs[:2]
        refs = refs[2:]
    else:
        x_ref = refs[0]
        refs = refs[1:]
    of_ref, ob_ref, gate_ref, yb_ref, mod_ref, og_ref, wout_ref, out_ref = refs
    o = of_ref[...].astype(F32) + ob_ref[...].astype(F32)
    sg = _silu(gate_ref[...].astype(F32))
    og = og_ref[...]
    parts = []
    for h in range(H_A):
        cols = slice(h * DV, (h + 1) * DV)
        oh = o[:, cols]
        ms = jnp.mean(oh * oh, axis=-1, keepdims=True)
        parts.append((oh * lax.rsqrt(ms + EPS)) * og * sg[:, cols])
    a = jnp.concatenate([part.astype(BF16) for part in parts] + [yb_ref[...]], axis=1)
    y = _dot(a, wout_ref[...])
    x = x_ref[...]
    if has_pe:
        x = x + pe_ref[...]
    gate1 = mod_ref[0][:, 2 * D_MODEL:3 * D_MODEL]
    out_ref[...] = x + gate1 * y


def _outproj(x, pe, o_f, o_b, gate, yb, mod3, row_base, row_stride, seq_len, tm, og, w_out):
    n_rows = x.shape[0]
    has_pe = pe is not None
    full = pl.BlockSpec((tm, D_MODEL), lambda i: (i, 0))
    half = pl.BlockSpec((tm, A_WIDTH), lambda i: (i, 0))
    in_specs = [full]
    args = [x]
    if has_pe:
        tiles = seq_len // tm
        in_specs.append(pl.BlockSpec((tm, D_MODEL), lambda i: (i % tiles, 0)))
        args.append(pe)
    in_specs += [half, half, half, half, _mod_spec(tm, seq_len, row_base, row_stride),
                 _const_spec((1, DV)), _const_spec((D_MODEL, D_MODEL))]
    args += [o_f, o_b, gate, yb, mod3, og, w_out]
    return pl.pallas_call(
        functools.partial(_outproj_kernel, has_pe=has_pe),
        grid=(n_rows // tm,), in_specs=in_specs, out_specs=full,
        out_shape=jax.ShapeDtypeStruct((n_rows, D_MODEL), F32),
        compiler_params=_params(1), name="outproj",
    )(*args)


def _ffn_kernel(*refs, tm, seq_len, final_norm):
    xp, xm, xn, mod_ref, ng_ref, wg_ref, wu_ref, wd_ref, cf_ref = refs[:9]
    refs = refs[9:]
    if final_norm:
        fg_ref = refs[0]
        refs = refs[1:]
    out_ref = refs[0]
    vp, vn = _halo_valid(tm, seq_len)
    mod = mod_ref[0]
    shift = mod[:, 3 * D_MODEL:4 * D_MODEL]
    scale = mod[:, 4 * D_MODEL:5 * D_MODEL]
    gate2 = mod[:, 5 * D_MODEL:6 * D_MODEL]
    g = ng_ref[...]
    x = xm[...]
    h_main = _norm_mod(x, g, scale, shift)
    h_ext = jnp.concatenate([_norm_mod(xp[...], g, scale, shift) * vp, h_main,
                             _norm_mod(xn[...], g, scale, shift) * vn], axis=0)
    hb_ext = h_ext.astype(BF16)
    hb = h_main.astype(BF16)
    edge_masks = _seq_edge_masks(tm, seq_len, FF_CHUNK)
    acts = []
    for j in range(N_FF_CHUNKS):
        cols = slice(j * FF_CHUNK, (j + 1) * FF_CHUNK)
        u = _dwconv3_ext(_dot(hb_ext, wg_ref[:, cols]), cf_ref[:, cols], tm, edge_masks)
        acts.append((_silu(u) * _dot(hb, wu_ref[:, cols])).astype(BF16))
    y = x + gate2 * _dot(jnp.concatenate(acts, axis=1), wd_ref[...])
    if final_norm:
        ms = jnp.mean(y * y, axis=-1, keepdims=True)
        y = (y * lax.rsqrt(ms + EPS)) * fg_ref[...]
    out_ref[...] = y


def _ffn(x, mod3, row_base, row_stride, seq_len, tm, ng, wts, final_g):
    n_rows = x.shape[0]
    final_norm = final_g is not None
    in_specs = _tile_specs(tm, D_MODEL, n_rows)
    in_specs += [_mod_spec(tm, seq_len, row_base, row_stride), _const_spec((1, D_MODEL)),
                 _const_spec((D_MODEL, D_FF)), _const_spec((D_MODEL, D_FF)),
                 _const_spec((D_FF, D_MODEL)), _const_spec((3, D_FF))]
    args = [x, x, x, mod3, ng, wts["w_gate"], wts["w_up"], wts["w_down"], wts["conv"]]
    if final_norm:
        in_specs.append(_const_spec((1, D_MODEL)))
        args.append(final_g)
    return pl.pallas_call(
        functools.partial(_ffn_kernel, tm=tm, seq_len=seq_len, final_norm=final_norm),
        grid=(n_rows // tm,), in_specs=in_specs,
        out_specs=pl.BlockSpec((tm, D_MODEL), lambda i: (i, 0)),
        out_shape=jax.ShapeDtypeStruct((n_rows, D_MODEL), F32),
        compiler_params=_params(1), name="conv_ffn",
    )(*args)


def _pool_kernel(xp, xm, xn, mod_ref, ng_ref, wp_ref, ps_ref, out_ref, *, tm, seq_len):
    vp, vn = _halo_valid(tm, seq_len)
    mod = mod_ref[0]
    shift = mod[:, 0:D_MODEL]
    scale = mod[:, D_MODEL:2 * D_MODEL]
    gate1 = mod[:, 2 * D_MODEL:3 * D_MODEL]
    g = ng_ref[...]
    x = xm[...]
    h_main = _norm_mod(x, g, scale, shift)
    h_ext = jnp.concatenate([_norm_mod(xp[...], g, scale, shift) * vp, h_main,
                             _norm_mod(xn[...], g, scale, shift) * vn], axis=0)
    i = pl.program_id(0)
    pos = (i * tm + lax.broadcasted_iota(jnp.int32, (tm, POOL_GROUP), 0)) % seq_len
    ys = []
    for gi, win in enumerate(POOL_WINDOWS):
        cols = slice(gi * POOL_GROUP, (gi + 1) * POOL_GROUP)
        s = h_ext[:, cols]
        width = 1
        while width < win:
            s = s + _shift_rows(s, width)
            width *= 2
        start = HALO - win // 2
        if start:
            s = _shift_rows(s, start)
        s = s[0:tm]
        lo = jnp.maximum(pos - win // 2, 0)
        hi = jnp.minimum(pos - win // 2 + win, seq_len)
        mixed = (s / (hi - lo).astype(F32) - h_main[:, cols]).astype(BF16)
        ys.append(_dot(mixed, wp_ref[gi]))
    y = jnp.concatenate(ys, axis=1) * ps_ref[...]
    out_ref[...] = x + gate1 * y


def _pool(x, mod3, row_base, row_stride, seq_len, tm, ng, w_pool, pool_scale):
    n_rows = x.shape[0]
    in_specs = _tile_specs(tm, D_MODEL, n_rows)
    in_specs += [_mod_spec(tm, seq_len, row_base, row_stride), _const_spec((1, D_MODEL)),
                 _const_spec((N_POOL_GROUPS, POOL_GROUP, POOL_GROUP)), _const_spec((1, D_MODEL))]
    return pl.pallas_call(
        functools.partial(_pool_kernel, tm=tm, seq_len=seq_len),
        grid=(n_rows // tm,), in_specs=in_specs,
        out_specs=pl.BlockSpec((tm, D_MODEL), lambda i: (i, 0)),
        out_shape=jax.ShapeDtypeStruct((n_rows, D_MODEL), F32),
        compiler_params=_params(1), name="pool_mixer",
    )(x, x, x, mod3, ng, w_pool, pool_scale)


def _sincos_2d(rows, cols, d):
    quarter = d // 4
    omega = 1.0 / (10000.0 ** (jnp.arange(quarter, dtype=F32) / quarter))
    er = jnp.arange(rows, dtype=F32)[:, None] * omega[None, :]
    ec = jnp.arange(cols, dtype=F32)[:, None] * omega[None, :]
    er = jnp.concatenate([jnp.sin(er), jnp.cos(er)], axis=-1)
    ec = jnp.concatenate([jnp.sin(ec), jnp.cos(ec)], axis=-1)
    pe = jnp.concatenate([jnp.broadcast_to(er[:, None, :], (rows, cols, d // 2)),
                          jnp.broadcast_to(ec[None, :, :], (rows, cols, d // 2))], axis=-1)
    return pe.reshape(rows * cols, d)


def _even_layer_weights(w_in, conv_qkv, a_log, dt_bias, conv_b, w_out):
    small = w_in[:, GATE_END:ALPHA_END]
    zeros = jnp.zeros((N_DIR * H_A,), F32)
    ea = jnp.concatenate([zeros, jnp.exp(a_log.astype(F32)).reshape(-1)])
    dtb = jnp.concatenate([zeros, dt_bias.astype(F32).reshape(-1)])
    p_col = jnp.stack([ea, dtb], axis=0)
    return {
        "w_qkv": w_in[:, :QKV_END].astype(BF16),
        "w_gate": w_in[:, QKV_END:GATE_END].astype(BF16),
        "w_b": w_in[:, ALPHA_END:IN_AB].astype(BF16),
        "w_small_c": small.astype(BF16),
        "w_small_r": small.T.astype(BF16),
        "conv_qkv": conv_qkv, "conv_b": conv_b,
        "p_col": p_col, "p_row": p_col.T,
        "w_out": w_out.astype(BF16),
    }


def _ffn_weights(w_gate, w_up, conv, w_down):
    return {"w_gate": w_gate.astype(BF16), "w_up": w_up.astype(BF16),
            "w_down": w_down.astype(BF16), "conv": conv}


def _trunk(x, pe, mod, row_base, row_stride, n_seq, seq_len, tm, tm_ffn, s0, write_final, p):
    row = lambda a: a.reshape(1, -1)
    finals = None
    depth = mod.shape[0]
    for layer in range(depth):
        mod3 = mod[layer][:, None, :]
        place = (mod3, row_base, row_stride, seq_len, tm)
        place_ffn = (mod3, row_base, row_stride, seq_len, tm_ffn)
        if layer % 2 == 0:
            e = layer // 2
            wts = p["even"][e]
            q, k, v, gate, gbc, gbr, yb = _inproj(x, pe, *place, row(p["norm_mix_g"][layer]), wts)
            local = _chunk_local(q, k, v, gbc, gbr)
            res = _scan(local, None if s0 is None else s0[:, e], n_seq, seq_len, write_final)
            o_f = res[0].reshape(-1, A_WIDTH)
            o_b = res[1].reshape(-1, A_WIDTH)
            if write_final:
                finals = res[2]
            x = _outproj(x, pe, o_f, o_b, gate, yb, *place, row(p["o_norm_g"][e]), wts["w_out"])
        else:
            o = layer // 2
            x = _pool(x, *place, row(p["norm_mix_g"][layer]), p["w_pool"][o],
                      row(p["pool_scale"][o]))
        final_g = row(p["final_norm_g"]) if layer == depth - 1 else None
        x = _ffn(x, *place_ffn, row(p["norm_ffn_g"][layer]), p["ffn"][layer], final_g)
    return x, finals


def kernel(x_prompt, x_sample, state_delta, c, c_ctx, norm_mix_g, norm_ffn_g, w_ada, b_ada,
           w_in_ab, conv_qkv, a_log, dt_bias, o_norm_g, conv_b, w_out_ab, w_pool, pool_scale,
           w_ffn_gate, w_ffn_up, ffn_conv, w_ffn_down, final_norm_g):
    batch, seq, d = x_prompt.shape
    dec_batch, dec_seq, _ = x_sample.shape
    depth = w_ada.shape[0]
    n_even = w_in_ab.shape[0]
    assert d == D_MODEL and depth == 2 and n_even == 1

    p = {
        "norm_mix_g": norm_mix_g, "norm_ffn_g": norm_ffn_g, "o_norm_g": o_norm_g,
        "pool_scale": pool_scale, "final_norm_g": final_norm_g,
        "w_pool": w_pool.astype(BF16),
        "even": [_even_layer_weights(w_in_ab[e], conv_qkv[e], a_log[e], dt_bias[e], conv_b[e],
                                     w_out_ab[e]) for e in range(n_even)],
        "ffn": [_ffn_weights(w_ffn_gate[l], w_ffn_up[l], ffn_conv[l], w_ffn_down[l])
                for l in range(depth)],
    }

    n_cond = 1 + dec_batch
    pad = (-n_cond) % HALO
    cond = jnp.concatenate([c_ctx[None, :], c, jnp.zeros((pad, d), F32)], axis=0)
    mod = _modulation(cond, w_ada, b_ada)

    y_prompt, finals = _trunk(x_prompt.reshape(batch * seq, d), None, mod, 0, 0, batch, seq,
                              min(seq, 256), 1024, None, True, p)
    pe = _sincos_2d(dec_seq // GRID_W, GRID_W, d)
    y_sample, _ = _trunk(x_sample.reshape(dec_batch * dec_seq, d), pe, mod, 1, 1, dec_batch,
                         dec_seq, 512, 1024, state_delta, False, p)
    return (y_prompt.reshape(batch, seq, d), y_sample.reshape(dec_batch, dec_seq, d),
            finals[:, None])
```

```python
import functools

import jax
import jax.numpy as jnp
from jax import lax
from jax.experimental import pallas as pl
from jax.experimental.pallas import tpu as pltpu

F32 = jnp.float32
BF16 = jnp.bfloat16

D_MODEL = 1024
GRID_W = 64
A_WIDTH = 512
B_WIDTH = 512
DK = 128
DV = 128
H_A = 4
N_DIR = 2
N_POOL_GROUPS = 4
POOL_GROUP = D_MODEL // N_POOL_GROUPS
POOL_WINDOWS = (2, 4, 8, 16)
D_FF = 2816
N_MOD = 6
EPS = 1e-6
QKV_END = 3 * A_WIDTH
GATE_END = QKV_END + A_WIDTH
BETA_END = GATE_END + N_DIR * H_A
ALPHA_END = BETA_END + N_DIR * H_A
BG_END = ALPHA_END + B_WIDTH
CG_END = BG_END + B_WIDTH
IN_AB = CG_END + B_WIDTH

SUBLANES = 8
HALO = SUBLANES
CHUNK = 128
FF_CHUNK = 256
N_FF_CHUNKS = D_FF // FF_CHUNK
N_SMALL = 2 * N_DIR * H_A
SEQ_GROUP = 4
VMEM_LIMIT = 56 * 1024 * 1024
NEG_BIG = -1e30


def _sigmoid(x):
    return 1.0 / (1.0 + jnp.exp(-x))


def _silu(x):
    return x * _sigmoid(x)


def _softplus(x):
    return jnp.maximum(x, 0.0) + jnp.log1p(jnp.exp(-jnp.abs(x)))


def _dot(a, b):
    return jnp.dot(a, b, preferred_element_type=F32)


def _dot_nt(a, b):
    return lax.dot_general(a, b, (((1,), (1,)), ((), ())), preferred_element_type=F32)


def _dot_tn(a, b):
    return lax.dot_general(a, b, (((0,), (0,)), ((), ())), preferred_element_type=F32)


def _split3(a):
    a1 = a.astype(BF16)
    r1 = a - a1.astype(F32)
    a2 = r1.astype(BF16)
    a3 = (r1 - a2.astype(F32)).astype(BF16)
    return a1, a2, a3


def _norm_mod(x, g, scale, shift):
    ms = jnp.mean(x * x, axis=-1, keepdims=True)
    return (x * lax.rsqrt(ms + EPS)) * g * (1.0 + scale) + shift


def _shift_rows(a, k):
    n = a.shape[0]
    return pltpu.roll(a, (-k) % n, axis=0)


def _dwconv3_ext(p_ext, w, tm, edge_masks=None):
    prev = _shift_rows(p_ext, -1)[HALO:HALO + tm]
    cur = p_ext[HALO:HALO + tm]
    nxt = _shift_rows(p_ext, 1)[HALO:HALO + tm]
    if edge_masks is not None:
        prev = prev * edge_masks[0]
        nxt = nxt * edge_masks[1]
    return prev * w[0:1] + cur * w[1:2] + nxt * w[2:3]


def _seq_edge_masks(tm, seq_len, width):
    if tm <= seq_len:
        return None
    pos = lax.broadcasted_iota(jnp.int32, (tm, width), 0) % seq_len
    return (jnp.where(pos != 0, 1.0, 0.0).astype(F32),
            jnp.where(pos != seq_len - 1, 1.0, 0.0).astype(F32))


def _halo_valid(tm, seq_len):
    i = pl.program_id(0)
    vp = jnp.where((i * tm) % seq_len != 0, 1.0, 0.0).astype(F32)
    vn = jnp.where(((i + 1) * tm) % seq_len != 0, 1.0, 0.0).astype(F32)
    return vp, vn


def _const_spec(shape):
    nd = len(shape)
    return pl.BlockSpec(shape, lambda *_: (0,) * nd)


def _tile_specs(tm, width, n_rows):
    per = tm // HALO
    last = n_rows // HALO - 1
    prev = pl.BlockSpec((HALO, width), lambda i: (jnp.maximum(i * per - 1, 0), 0))
    main = pl.BlockSpec((tm, width), lambda i: (i, 0))
    nxt = pl.BlockSpec((HALO, width), lambda i: (jnp.minimum((i + 1) * per, last), 0))
    return [prev, main, nxt]


def _pe_specs(tm, width, seq_len):
    per = tm // HALO
    tiles = seq_len // tm
    last = seq_len // HALO - 1
    prev = pl.BlockSpec((HALO, width), lambda i: (jnp.maximum((i % tiles) * per - 1, 0), 0))
    main = pl.BlockSpec((tm, width), lambda i: (i % tiles, 0))
    nxt = pl.BlockSpec((HALO, width), lambda i: (jnp.minimum((i % tiles + 1) * per, last), 0))
    return [prev, main, nxt]


def _mod_spec(tm, seq_len, row_base, row_stride):
    return pl.BlockSpec((1, 1, N_MOD * D_MODEL),
                        lambda i: (row_base + ((i * tm) // seq_len) * row_stride, 0, 0))


def _params(n_axes=1, semantics=None):
    return pltpu.CompilerParams(
        dimension_semantics=semantics or ("parallel",) * n_axes,
        vmem_limit_bytes=VMEM_LIMIT)


def _mod_kernel(c_ref, w_ref, b_ref, o_ref):
    c = c_ref[...]
    s1, s2, _ = _split3(_silu(c))
    w = w_ref[0]
    w1 = w.astype(BF16)
    w2 = (w - w1.astype(F32)).astype(BF16)
    o_ref[0] = _dot(s1, w1) + (_dot(s2, w1) + _dot(s1, w2)) + b_ref[0]


def _modulation(cond, w_ada, b_ada):
    depth, _, width = w_ada.shape
    rows = cond.shape[0]
    tn = 1536
    return pl.pallas_call(
        _mod_kernel,
        grid=(depth, width // tn),
        in_specs=[pl.BlockSpec((rows, D_MODEL), lambda l, j: (0, 0)),
                  pl.BlockSpec((1, D_MODEL, tn), lambda l, j: (l, 0, j)),
                  pl.BlockSpec((1, 1, tn), lambda l, j: (l, 0, j))],
        out_specs=pl.BlockSpec((1, rows, tn), lambda l, j: (l, 0, j)),
        out_shape=jax.ShapeDtypeStruct((depth, rows, width), F32),
        compiler_params=_params(2),
        name="modulation",
    )(cond, w_ada, b_ada.reshape(depth, 1, width))


def _inproj_kernel(*refs, tm, seq_len, has_pe):
    if has_pe:
        xp, xm, xn, pp, pm, pn = refs[:6]
        refs = refs[6:]
    else:
        xp, xm, xn = refs[:3]
        pp = pm = pn = None
        refs = refs[3:]
    (mod_ref, ng_ref, wqkv_ref, wgate_ref, wb_ref, wsc_ref, wsr_ref, cq_ref, cb_ref,
     pc_ref, pr_ref, q_ref, k_ref, v_ref, gate_ref, gbc_ref, gbr_ref, yb_ref) = refs

    vp, vn = _halo_valid(tm, seq_len)
    mod = mod_ref[0]
    shift = mod[:, 0:D_MODEL]
    scale = mod[:, D_MODEL:2 * D_MODEL]
    g = ng_ref[...]

    def prep(x_ref, pe_ref):
        x = x_ref[...]
        if has_pe:
            x = x + pe_ref[...]
        return _norm_mod(x, g, scale, shift)

    h_main = prep(xm, pm)
    h_ext = jnp.concatenate([prep(xp, pp) * vp, h_main, prep(xn, pn) * vn], axis=0)
    hb_ext = h_ext.astype(BF16)
    hb = h_main.astype(BF16)

    cq = cq_ref[...]
    outs = (q_ref, k_ref, v_ref)
    projected = [_dot(hb_ext, wqkv_ref[:, part * A_WIDTH:(part + 1) * A_WIDTH])
                 for part in range(3)]
    for part in range(3):
        cols = slice(part * A_WIDTH, (part + 1) * A_WIDTH)
        a = _silu(_dwconv3_ext(projected[part], cq[:, cols], tm))
        if part < 2:
            heads = []
            for h in range(H_A):
                ah = a[:, h * DK:(h + 1) * DK]
                ss = jnp.sum(ah * ah, axis=-1, keepdims=True)
                nrm = lax.rsqrt(ss + EPS)
                if part == 0:
                    nrm = nrm * (DK ** -0.5)
                heads.append(ah * nrm)
            a = jnp.concatenate(heads, axis=1)
        outs[part][...] = a.astype(BF16)

    gate_ref[...] = _dot(hb, wgate_ref[...]).astype(BF16)

    pc = pc_ref[...]
    ba = _dot(hb, wsc_ref[...])
    lane = lax.broadcasted_iota(jnp.int32, ba.shape, 1)
    gbc_ref[...] = jnp.where(lane < N_DIR * H_A, _sigmoid(ba),
                             -pc[0:1] * _softplus(ba + pc[1:2]))
    pr = pr_ref[...]
    bar = _dot_nt(wsr_ref[...], hb)
    sub = lax.broadcasted_iota(jnp.int32, bar.shape, 0)
    gbr_ref[...] = jnp.where(sub < N_DIR * H_A, _sigmoid(bar),
                             -pr[:, 0:1] * _softplus(bar + pr[:, 1:2]))

    pb = _dot(hb_ext, wb_ref[...])
    bg = pb[HALO:HALO + tm, 0:B_WIDTH]
    cghx = pb[:, B_WIDTH:2 * B_WIDTH] * pb[:, 2 * B_WIDTH:3 * B_WIDTH]
    yb_ref[...] = (bg * _dwconv3_ext(cghx, cb_ref[...], tm)).astype(BF16)


def _inproj(x, pe, mod3, row_base, row_stride, seq_len, tm, ng, wts):
    n_rows = x.shape[0]
    has_pe = pe is not None
    in_specs = _tile_specs(tm, D_MODEL, n_rows)
    args = [x, x, x]
    if has_pe:
        in_specs += _pe_specs(tm, D_MODEL, seq_len)
        args += [pe, pe, pe]
    in_specs += [_mod_spec(tm, seq_len, row_base, row_stride), _const_spec((1, D_MODEL)),
                 _const_spec((D_MODEL, QKV_END)), _const_spec((D_MODEL, A_WIDTH)),
                 _const_spec((D_MODEL, 3 * B_WIDTH)), _const_spec((D_MODEL, N_SMALL)),
                 _const_spec((N_SMALL, D_MODEL)), _const_spec((3, QKV_END)),
                 _const_spec((3, B_WIDTH)), _const_spec((2, N_SMALL)), _const_spec((N_SMALL, 2))]
    args += [mod3, ng, wts["w_qkv"], wts["w_gate"], wts["w_b"], wts["w_small_c"], wts["w_small_r"],
             wts["conv_qkv"], wts["conv_b"], wts["p_col"], wts["p_row"]]
    wide = pl.BlockSpec((tm, A_WIDTH), lambda i: (i, 0))
    out_specs = [wide, wide, wide, wide,
                 pl.BlockSpec((tm, N_SMALL), lambda i: (i, 0)),
                 pl.BlockSpec((N_SMALL, tm), lambda i: (0, i)),
                 wide]
    wide_shape = jax.ShapeDtypeStruct((n_rows, A_WIDTH), BF16)
    out_shape = [wide_shape] * 4 + [jax.ShapeDtypeStruct((n_rows, N_SMALL), F32),
                                    jax.ShapeDtypeStruct((N_SMALL, n_rows), F32), wide_shape]
    return pl.pallas_call(
        functools.partial(_inproj_kernel, tm=tm, seq_len=seq_len, has_pe=has_pe),
        grid=(n_rows // tm,), in_specs=in_specs, out_specs=out_specs, out_shape=out_shape,
        compiler_params=_params(1), name="inproj",
    )(*args)


def _dot3_right(a_exact, b):
    b1, b2, b3 = _split3(b)
    return _dot(a_exact, b1) + (_dot(a_exact, b2) + _dot(a_exact, b3))


def _dot3_left(a, b_exact):
    a1, a2, a3 = _split3(a)
    return _dot(a1, b_exact) + (_dot(a2, b_exact) + _dot(a3, b_exact))


def _chunk_setup(q_ref, k_ref, v_ref, gbc_ref, gbr_ref, qg_ref, kg_ref, at_ref, gl_ref,
                 m_scr, rhs_scr, *, n_sub):
    c = CHUNK
    n_beta = N_DIR * H_A
    r1 = lax.broadcasted_iota(jnp.int32, (c, c), 0)
    c1 = lax.broadcasted_iota(jnp.int32, (c, c), 1)
    tri_lo = jnp.where(r1 >= c1, 1.0, 0.0).astype(BF16)
    tri_up = jnp.where(r1 <= c1, 1.0, 0.0).astype(BF16)
    r2 = lax.broadcasted_iota(jnp.int32, (2 * c, 2 * c), 0)
    c2 = lax.broadcasted_iota(jnp.int32, (2 * c, 2 * c), 1)
    same_head = (r2 ^ c2) < c

    def col_pair(a, j):
        col = jnp.concatenate([a[:, j:j + 1], a[:, j + 1:j + 2]], axis=0)
        return jnp.broadcast_to(col, (2 * c, c))

    def row_pair(a, j):
        return jnp.concatenate([a[j:j + 1, :], a[j + 1:j + 2, :]], axis=1)

    def twice(a):
        return jnp.concatenate([a, a], axis=1)

    n = 0
    for ci in range(n_sub):
        rows_ci = slice(ci * c, (ci + 1) * c)
        gbc = gbc_ref[rows_ci, :]
        gbr = gbr_ref[:, rows_ci]
        lane = lax.broadcasted_iota(jnp.int32, gbc.shape, 1)
        gc_c = jnp.where(lane < n_beta + H_A, _dot3_right(tri_lo, gbc), _dot3_right(tri_up, gbc))
        sub = lax.broadcasted_iota(jnp.int32, gbr.shape, 0)
        gc_r = jnp.where(sub < n_beta + H_A, _dot3_left(gbr, tri_up), _dot3_left(gbr, tri_lo))
        total = jnp.broadcast_to(jnp.sum(gbr, axis=1, keepdims=True), gbr.shape)
        gl_ref[ci] = jnp.exp(total)[n_beta:, :]

        q = q_ref[rows_ci, :]
        k = k_ref[rows_ci, :]
        v = v_ref[rows_ci, :]
        for hp in range(H_A // 2):
            h0 = 2 * hp

            def stack(a):
                return jnp.concatenate([a[:, h0 * DK:(h0 + 1) * DK],
                                        a[:, (h0 + 1) * DK:(h0 + 2) * DK]], axis=0)

            q2b, k2b, v2b = stack(q), stack(k), stack(v)
            kk = _dot_nt(k2b, k2b)
            qk = _dot_nt(q2b, k2b)
            for d in range(N_DIR):
                jb = d * H_A + h0
                jg = n_beta + jb
                gcb = col_pair(gc_c, jg)
                beta = col_pair(gbc, jb)
                tot = jnp.concatenate([jnp.broadcast_to(total[jg:jg + 1, :], (c, c)),
                                       jnp.broadcast_to(total[jg + 1:jg + 2, :], (c, c))], axis=0)
                eg = jnp.exp(gcb)
                er = jnp.exp(tot - gcb)
                tri = (r2 >= c2) if d == 0 else (r2 <= c2)
                decay = jnp.exp(jnp.where(same_head & tri, twice(gcb) - row_pair(gc_r, jg), NEG_BIG))
                attn = qk * decay
                m_scr[n] = jnp.where(r2 != c2, (twice(beta) * kk) * decay, 0.0)
                rhs_scr[n] = jnp.concatenate([v2b * beta.astype(BF16),
                                              k2b * (beta * eg).astype(BF16)], axis=1)
                qg = q2b * eg.astype(BF16)
                kg = k2b * er.astype(BF16)
                for hh in range(2):
                    rows = slice(hh * c, (hh + 1) * c)
                    cols = slice((h0 + hh) * DK, (h0 + hh + 1) * DK)
                    qg_ref[d, rows_ci, cols] = qg[rows]
                    kg_ref[d, rows_ci, cols] = kg[rows]
                    at_ref[d, rows_ci, cols] = attn[rows, hh * c:(hh + 1) * c].astype(BF16)
                n += 1


def _chunk_solve(m_scr, rhs_scr, gsel_ref, u_ref, w_ref, *, n_sub):
    c = CHUNK
    chains = [{"ci": ci, "h0": 2 * hp, "d": d}
              for ci in range(n_sub) for hp in range(H_A // 2) for d in range(N_DIR)]
    for n, ch in enumerate(chains):
        ch["m"] = m_scr[n]
    r2 = lax.broadcasted_iota(jnp.int32, (2 * c, 2 * c), 0)
    c2 = lax.broadcasted_iota(jnp.int32, (2 * c, 2 * c), 1)
    x2 = r2 ^ c2

    sub8 = SUBLANES
    per_half = c // sub8
    lane8 = lax.broadcasted_iota(jnp.int32, (sub8, c), 1)
    row8 = lax.broadcasted_iota(jnp.int32, (sub8, c), 0)
    in_blk = [(lane8 >> 3) == i for i in range(per_half)]
    zeros8 = jnp.zeros((sub8, c), F32)
    packed = []
    for ch in chains:
        halves = []
        for half in range(2):
            pm = zeros8
            for i in range(per_half):
                blk = half * per_half + i
                pm = jnp.where(in_blk[i], ch["m"][blk * sub8:(blk + 1) * sub8,
                                                  half * c:(half + 1) * c], pm)
            halves.append(pm)
        packed.append(jnp.concatenate(halves, axis=1))
    stacked = jnp.concatenate(packed, axis=0).astype(BF16)
    spread = [_dot(stacked, gsel_ref[k]) for k in range(sub8)]
    eye8 = jnp.where((lane8 & (sub8 - 1)) == row8, 1.0, 0.0).astype(F32)
    for n, ch in enumerate(chains):
        rows_n = slice(n * sub8, (n + 1) * sub8)
        tp = jnp.concatenate([eye8, eye8], axis=1)
        order = range(sub8 - 1) if ch["d"] == 0 else range(sub8 - 1, 0, -1)
        for k in order:
            tp = tp - spread[k][rows_n] * tp[k:k + 1, :]
        row_groups = []
        for half in range(2):
            tph = tp[:, half * c:(half + 1) * c]
            for i in range(per_half):
                own = jnp.where(in_blk[i], tph, 0.0)
                row_groups.append(jnp.concatenate([own, zeros8] if half == 0 else [zeros8, own],
                                                  axis=1))
        ch["t"] = jnp.concatenate(row_groups, axis=0)

    def pick(a, b, parity):
        return jnp.concatenate([a[i * b:(i + 1) * b] for i in range(a.shape[0] // b)
                                if i % 2 == parity], axis=0)

    def weave(sel, rest, b, parity):
        blocks = []
        for i in range(2 * sel.shape[0] // b):
            src = sel if i % 2 == parity else rest
            blocks.append(src[(i // 2) * b:(i // 2 + 1) * b])
        return jnp.concatenate(blocks, axis=0)

    par = [1 - ch["d"] for ch in chains]
    for lvl in range(3, 7):
        b = 1 << lvl
        xs = []
        for ch, p in zip(chains, par):
            off = jnp.where((pick(x2, b, p) >> lvl) == 1, pick(ch["m"], b, p), 0.0)
            xs.append(_dot(off.astype(BF16), ch["t"].astype(BF16)))
        for ch, x, p in zip(chains, xs, par):
            x_full = weave(x, jnp.zeros_like(x), b, p).astype(BF16)
            t_sel = pick(ch["t"], b, p)
            t_new = t_sel - _dot(t_sel.astype(BF16), x_full)
            ch["t"] = weave(t_new, pick(ch["t"], b, 1 - p), b, p)

    for n, ch in enumerate(chains):
        uw = _dot(ch["t"].astype(BF16), rhs_scr[n])
        rows_ci = slice(ch["ci"] * c, (ch["ci"] + 1) * c)
        for hh in range(2):
            rows = slice(hh * c, (hh + 1) * c)
            cols = slice((ch["h0"] + hh) * DK, (ch["h0"] + hh + 1) * DK)
            u_ref[ch["d"], rows_ci, cols] = uw[rows, 0:DV].astype(BF16)
            w_ref[ch["d"], rows_ci, cols] = uw[rows, DV:2 * DV].astype(BF16)


def _chunk_local_kernel(q_ref, k_ref, v_ref, gbc_ref, gbr_ref, gsel_ref,
                        u_ref, w_ref, qg_ref, kg_ref, at_ref, gl_ref,
                        m_a, rhs_a, m_b, rhs_b, *, n_sub):
    setup = functools.partial(_chunk_setup, q_ref, k_ref, v_ref, gbc_ref, gbr_ref,
                              qg_ref, kg_ref, at_ref, gl_ref, n_sub=n_sub)
    solve = functools.partial(_chunk_solve, gsel_ref=gsel_ref, u_ref=u_ref, w_ref=w_ref,
                              n_sub=n_sub)
    j = pl.program_id(0)

    @pl.when(j == 0)
    def _():
        u_ref[...] = jnp.zeros(u_ref.shape, u_ref.dtype)
        w_ref[...] = jnp.zeros(w_ref.shape, w_ref.dtype)
        setup(m_a, rhs_a)

    @pl.when(j % 2 == 1)
    def _():
        solve(m_a, rhs_a)
        setup(m_b, rhs_b)

    @pl.when((j > 0) & (j % 2 == 0))
    def _():
        solve(m_b, rhs_b)
        setup(m_a, rhs_a)


def _chunk_local(q, k, v, gbc, gbr, n_sub=2):
    n_rows = q.shape[0]
    n_chunks = n_rows // CHUNK
    rows = n_sub * CHUNK
    n_blocks = n_chunks // n_sub
    n_sys = n_sub * (H_A // 2) * N_DIR
    cur = lambda j: jnp.minimum(j, n_blocks - 1)
    prev = lambda j: jnp.maximum(j - 1, 0)
    wide = pl.BlockSpec((rows, A_WIDTH), lambda j: (cur(j), 0))
    out_cur = pl.BlockSpec((N_DIR, rows, A_WIDTH), lambda j: (0, cur(j), 0))
    out_prev = pl.BlockSpec((N_DIR, rows, A_WIDTH), lambda j: (0, prev(j), 0))
    big = lambda dt: jax.ShapeDtypeStruct((N_DIR, n_rows, A_WIDTH), dt)
    idx = jnp.arange(2 * CHUNK)
    gsel = ((idx[None, :, None] // SUBLANES == idx[None, None, :] // SUBLANES)
            & (idx[None, :, None] % SUBLANES == jnp.arange(SUBLANES)[:, None, None])).astype(BF16)
    return pl.pallas_call(
        functools.partial(_chunk_local_kernel, n_sub=n_sub),
        grid=(n_blocks + 1,),
        in_specs=[wide, wide, wide,
                  pl.BlockSpec((rows, N_SMALL), lambda j: (cur(j), 0)),
                  pl.BlockSpec((N_SMALL, rows), lambda j: (0, cur(j))),
                  _const_spec((SUBLANES, 2 * CHUNK, 2 * CHUNK))],
        out_specs=[out_prev, out_prev, out_cur, out_cur, out_cur,
                   pl.BlockSpec((n_sub, N_DIR * H_A, CHUNK), lambda j: (cur(j), 0, 0))],
        out_shape=[big(BF16)] * 5 + [jax.ShapeDtypeStruct((n_chunks, N_DIR * H_A, CHUNK), F32)],
        scratch_shapes=[pltpu.VMEM((n_sys, 2 * CHUNK, 2 * CHUNK), F32),
                        pltpu.VMEM((n_sys, 2 * CHUNK, 2 * CHUNK), BF16),
                        pltpu.VMEM((n_sys, 2 * CHUNK, 2 * CHUNK), F32),
                        pltpu.VMEM((n_sys, 2 * CHUNK, 2 * CHUNK), BF16)],
        compiler_params=_params(1, ("arbitrary",)), name="chunk_local",
    )(q, k, v, gbc, gbr, gsel)


def _scan_kernel(*refs, n_chunks, zero_init, write_final):
    fwd = refs[0:6]
    bwd = refs[6:12]
    refs = refs[12:]
    if not zero_init:
        s0_ref = refs[0]
        refs = refs[1:]
    of_ref, ob_ref = refs[0:2]
    refs = refs[2:]
    if write_final:
        sfin_ref = refs[0]
        refs = refs[1:]
    s_scr = refs[0]
    i = pl.program_id(1)

    @pl.when(i == 0)
    def _():
        if zero_init:
            s_scr[...] = jnp.zeros(s_scr.shape, F32)
        else:
            s_scr[...] = s0_ref[...]

    problems = [(s, d, h) for s in range(SEQ_GROUP) for d in range(N_DIR) for h in range(H_A)]
    c = CHUNK
    stage1 = []
    for s, d, h in problems:
        u_ref, w_ref, qg_ref, _, _, _ = fwd if d == 0 else bwd
        cols = slice(h * DV, (h + 1) * DV)
        sb = s_scr[s, d, h].astype(BF16)
        lhs = jnp.concatenate([w_ref[0, s, :, cols], qg_ref[0, s, :, cols]], axis=0)
        stage1.append(_dot(lhs, sb))
    for (s, d, h), ws_qs in zip(problems, stage1):
        u_ref, _, _, kg_ref, at_ref, gl_ref = fwd if d == 0 else bwd
        o_ref = of_ref if d == 0 else ob_ref
        cols = slice(h * DV, (h + 1) * DV)
        vb = (u_ref[0, s, :, cols].astype(F32) - ws_qs[0:c]).astype(BF16)
        o_ref[s, :, cols] = (ws_qs[c:2 * c] + _dot(at_ref[0, s, :, cols], vb)).astype(BF16)
        gl = gl_ref[s, 0, d * H_A + h:d * H_A + h + 1, :]
        s_scr[s, d, h] = s_scr[s, d, h] * gl + _dot_tn(kg_ref[0, s, :, cols], vb)

    if write_final:
        @pl.when(i == n_chunks - 1)
        def _():
            sfin_ref[...] = s_scr[...]


def _scan(local, s0, n_seq, seq_len, write_final):
    u, w, qg, kg, at, gl = local
    n_chunks = seq_len // CHUNK
    zero_init = s0 is None
    shp4 = (N_DIR, n_seq, seq_len, A_WIDTH)
    arrs = [a.reshape(shp4) for a in (u, w, qg, kg, at)]
    gl4 = gl.reshape(n_seq, n_chunks, N_DIR * H_A, CHUNK)
    sg = SEQ_GROUP

    def specs(d, chunk_of):
        big = pl.BlockSpec((1, sg, CHUNK, A_WIDTH), lambda g, i: (d, g, chunk_of(i), 0))
        return [big] * 5 + [pl.BlockSpec((sg, 1, N_DIR * H_A, CHUNK),
                                         lambda g, i: (g, chunk_of(i), 0, 0))]

    rev = lambda i: n_chunks - 1 - i
    in_specs = specs(0, lambda i: i) + specs(1, rev)
    args = arrs + [gl4] + arrs + [gl4]
    state_spec = pl.BlockSpec((sg, N_DIR, H_A, DK, DV), lambda g, i: (g, 0, 0, 0, 0))
    if not zero_init:
        in_specs.append(state_spec)
        args.append(s0)
    o_shape = jax.ShapeDtypeStruct((n_seq, seq_len, A_WIDTH), BF16)
    out_specs = [pl.BlockSpec((sg, CHUNK, A_WIDTH), lambda g, i: (g, i, 0)),
                 pl.BlockSpec((sg, CHUNK, A_WIDTH), lambda g, i: (g, rev(i), 0))]
    out_shape = [o_shape, o_shape]
    if write_final:
        out_specs.append(state_spec)
        out_shape.append(jax.ShapeDtypeStruct((n_seq, N_DIR, H_A, DK, DV), F32))
    return pl.pallas_call(
        functools.partial(_scan_kernel, n_chunks=n_chunks, zero_init=zero_init,
                          write_final=write_final),
        grid=(n_seq // sg, n_chunks), in_specs=in_specs, out_specs=out_specs, out_shape=out_shape,
        scratch_shapes=[pltpu.VMEM((sg, N_DIR, H_A, DK, DV), F32)],
        compiler_params=_params(2, ("parallel", "arbitrary")), name="delta_scan",
    )(*args)


def _outproj_kernel(*refs, has_pe):
    if has_pe:
        x_ref, pe_ref = refs[:2]
        refs = refs[2:]
    else:
        x_ref = refs[0]
        refs = refs[1:]
    of_ref, ob_ref, gate_ref, yb_ref, mod_ref, og_ref, wout_ref, out_ref = refs
    o = of_ref[...].astype(F32) + ob_ref[...].astype(F32)
    sg = _silu(gate_ref[...].astype(F32))
    og = og_ref[...]
    parts = []
    for h in range(H_A):
        cols = slice(h * DV, (h + 1) * DV)
        oh = o[:, cols]
        ms = jnp.mean(oh * oh, axis=-1, keepdims=True)
        parts.append((oh * lax.rsqrt(ms + EPS)) * og * sg[:, cols])
    a = jnp.concatenate([part.astype(BF16) for part in parts] + [yb_ref[...]], axis=1)
    y = _dot(a, wout_ref[...])
    x = x_ref[...]
    if has_pe:
        x = x + pe_ref[...]
    gate1 = mod_ref[0][:, 2 * D_MODEL:3 * D_MODEL]
    out_ref[...] = x + gate1 * y


def _outproj(x, pe, o_f, o_b, gate, yb, mod3, row_base, row_stride, seq_len, tm, og, w_out):
    n_rows = x.shape[0]
    has_pe = pe is not None
    full = pl.BlockSpec((tm, D_MODEL), lambda i: (i, 0))
    half = pl.BlockSpec((tm, A_WIDTH), lambda i: (i, 0))
    in_specs = [full]
    args = [x]
    if has_pe:
        tiles = seq_len // tm
        in_specs.append(pl.BlockSpec((tm, D_MODEL), lambda i: (i % tiles, 0)))
        args.append(pe)
    in_specs += [half, half, half, half, _mod_spec(tm, seq_len, row_base, row_stride),
                 _const_spec((1, DV)), _const_spec((D_MODEL, D_MODEL))]
    args += [o_f, o_b, gate, yb, mod3, og, w_out]
    return pl.pallas_call(
        functools.partial(_outproj_kernel, has_pe=has_pe),
        grid=(n_rows // tm,), in_specs=in_specs, out_specs=full,
        out_shape=jax.ShapeDtypeStruct((n_rows, D_MODEL), F32),
        compiler_params=_params(1), name="outproj",
    )(*args)


def _ffn_kernel(*refs, tm, seq_len, final_norm):
    xp, xm, xn, mod_ref, ng_ref, wg_ref, wu_ref, wd_ref, cf_ref = refs[:9]
    refs = refs[9:]
    if final_norm:
        fg_ref = refs[0]
        refs = refs[1:]
    out_ref = refs[0]
    vp, vn = _halo_valid(tm, seq_len)
    mod = mod_ref[0]
    shift = mod[:, 3 * D_MODEL:4 * D_MODEL]
    scale = mod[:, 4 * D_MODEL:5 * D_MODEL]
    gate2 = mod[:, 5 * D_MODEL:6 * D_MODEL]
    g = ng_ref[...]
    x = xm[...]
    h_main = _norm_mod(x, g, scale, shift)
    h_ext = jnp.concatenate([_norm_mod(xp[...], g, scale, shift) * vp, h_main,
                             _norm_mod(xn[...], g, scale, shift) * vn], axis=0)
    hb_ext = h_ext.astype(BF16)
    hb = h_main.astype(BF16)
    edge_masks = _seq_edge_masks(tm, seq_len, FF_CHUNK)
    acts = []
    for j in range(N_FF_CHUNKS):
        cols = slice(j * FF_CHUNK, (j + 1) * FF_CHUNK)
        u = _dwconv3_ext(_dot(hb_ext, wg_ref[:, cols]), cf_ref[:, cols], tm, edge_masks)
        acts.append((_silu(u) * _dot(hb, wu_ref[:, cols])).astype(BF16))
    y = x + gate2 * _dot(jnp.concatenate(acts, axis=1), wd_ref[...])
    if final_norm:
        ms = jnp.mean(y * y, axis=-1, keepdims=True)
        y = (y * lax.rsqrt(ms + EPS)) * fg_ref[...]
    out_ref[...] = y


def _ffn(x, mod3, row_base, row_stride, seq_len, tm, ng, wts, final_g):
    n_rows = x.shape[0]
    final_norm = final_g is not None
    in_specs = _tile_specs(tm, D_MODEL, n_rows)
    in_specs += [_mod_spec(tm, seq_len, row_base, row_stride), _const_spec((1, D_MODEL)),
                 _const_spec((D_MODEL, D_FF)), _const_spec((D_MODEL, D_FF)),
                 _const_spec((D_FF, D_MODEL)), _const_spec((3, D_FF))]
    args = [x, x, x, mod3, ng, wts["w_gate"], wts["w_up"], wts["w_down"], wts["conv"]]
    if final_norm:
        in_specs.append(_const_spec((1, D_MODEL)))
        args.append(final_g)
    return pl.pallas_call(
        functools.partial(_ffn_kernel, tm=tm, seq_len=seq_len, final_norm=final_norm),
        grid=(n_rows // tm,), in_specs=in_specs,
        out_specs=pl.BlockSpec((tm, D_MODEL), lambda i: (i, 0)),
        out_shape=jax.ShapeDtypeStruct((n_rows, D_MODEL), F32),
        compiler_params=_params(1), name="conv_ffn",
    )(*args)


def _pool_kernel(xp, xm, xn, mod_ref, ng_ref, wp_ref, ps_ref, out_ref, *, tm, seq_len):
    vp, vn = _halo_valid(tm, seq_len)
    mod = mod_ref[0]
    shift = mod[:, 0:D_MODEL]
    scale = mod[:, D_MODEL:2 * D_MODEL]
    gate1 = mod[:, 2 * D_MODEL:3 * D_MODEL]
    g = ng_ref[...]
    x = xm[...]
    h_main = _norm_mod(x, g, scale, shift)
    h_ext = jnp.concatenate([_norm_mod(xp[...], g, scale, shift) * vp, h_main,
                             _norm_mod(xn[...], g, scale, shift) * vn], axis=0)
    i = pl.program_id(0)
    pos = (i * tm + lax.broadcasted_iota(jnp.int32, (tm, POOL_GROUP), 0)) % seq_len
    ys = []
    for gi, win in enumerate(POOL_WINDOWS):
        cols = slice(gi * POOL_GROUP, (gi + 1) * POOL_GROUP)
        s = h_ext[:, cols]
        width = 1
        while width < win:
            s = s + _shift_rows(s, width)
            width *= 2
        start = HALO - win // 2
        if start:
            s = _shift_rows(s, start)
        s = s[0:tm]
        lo = jnp.maximum(pos - win // 2, 0)
        hi = jnp.minimum(pos - win // 2 + win, seq_len)
        mixed = (s / (hi - lo).astype(F32) - h_main[:, cols]).astype(BF16)
        ys.append(_dot(mixed, wp_ref[gi]))
    y = jnp.concatenate(ys, axis=1) * ps_ref[...]
    out_ref[...] = x + gate1 * y


def _pool(x, mod3, row_base, row_stride, seq_len, tm, ng, w_pool, pool_scale):
    n_rows = x.shape[0]
    in_specs = _tile_specs(tm, D_MODEL, n_rows)
    in_specs += [_mod_spec(tm, seq_len, row_base, row_stride), _const_spec((1, D_MODEL)),
                 _const_spec((N_POOL_GROUPS, POOL_GROUP, POOL_GROUP)), _const_spec((1, D_MODEL))]
    return pl.pallas_call(
        functools.partial(_pool_kernel, tm=tm, seq_len=seq_len),
        grid=(n_rows // tm,), in_specs=in_specs,
        out_specs=pl.BlockSpec((tm, D_MODEL), lambda i: (i, 0)),
        out_shape=jax.ShapeDtypeStruct((n_rows, D_MODEL), F32),
        compiler_params=_params(1), name="pool_mixer",
    )(x, x, x, mod3, ng, w_pool, pool_scale)


def _sincos_2d(rows, cols, d):
    quarter = d // 4
    omega = 1.0 / (10000.0 ** (jnp.arange(quarter, dtype=F32) / quarter))
    er = jnp.arange(rows, dtype=F32)[:, None] * omega[None, :]
    ec = jnp.arange(cols, dtype=F32)[:, None] * omega[None, :]
    er = jnp.concatenate([jnp.sin(er), jnp.cos(er)], axis=-1)
    ec = jnp.concatenate([jnp.sin(ec), jnp.cos(ec)], axis=-1)
    pe = jnp.concatenate([jnp.broadcast_to(er[:, None, :], (rows, cols, d // 2)),
                          jnp.broadcast_to(ec[None, :, :], (rows, cols, d // 2))], axis=-1)
    return pe.reshape(rows * cols, d)


def _even_layer_weights(w_in, conv_qkv, a_log, dt_bias, conv_b, w_out):
    small = w_in[:, GATE_END:ALPHA_END]
    zeros = jnp.zeros((N_DIR * H_A,), F32)
    ea = jnp.concatenate([zeros, jnp.exp(a_log.astype(F32)).reshape(-1)])
    dtb = jnp.concatenate([zeros, dt_bias.astype(F32).reshape(-1)])
    p_col = jnp.stack([ea, dtb], axis=0)
    return {
        "w_qkv": w_in[:, :QKV_END].astype(BF16),
        "w_gate": w_in[:, QKV_END:GATE_END].astype(BF16),
        "w_b": w_in[:, ALPHA_END:IN_AB].astype(BF16),
        "w_small_c": small.astype(BF16),
        "w_small_r": small.T.astype(BF16),
        "conv_qkv": conv_qkv, "conv_b": conv_b,
        "p_col": p_col, "p_row": p_col.T,
        "w_out": w_out.astype(BF16),
    }


def _ffn_weights(w_gate, w_up, conv, w_down):
    return {"w_gate": w_gate.astype(BF16), "w_up": w_up.astype(BF16),
            "w_down": w_down.astype(BF16), "conv": conv}


def _trunk(x, pe, mod, row_base, row_stride, n_seq, seq_len, tm, tm_ffn, s0, write_final, p):
    row = lambda a: a.reshape(1, -1)
    finals = None
    depth = mod.shape[0]
    for layer in range(depth):
        mod3 = mod[layer][:, None, :]
        place = (mod3, row_base, row_stride, seq_len, tm)
        place_ffn = (mod3, row_base, row_stride, seq_len, tm_ffn)
        if layer % 2 == 0:
            e = layer // 2
            wts = p["even"][e]
            q, k, v, gate, gbc, gbr, yb = _inproj(x, pe, *place, row(p["norm_mix_g"][layer]), wts)
            local = _chunk_local(q, k, v, gbc, gbr)
            res = _scan(local, None if s0 is None else s0[:, e], n_seq, seq_len, write_final)
            o_f = res[0].reshape(-1, A_WIDTH)
            o_b = res[1].reshape(-1, A_WIDTH)
            if write_final:
                finals = res[2]
            x = _outproj(x, pe, o_f, o_b, gate, yb, *place, row(p["o_norm_g"][e]), wts["w_out"])
        else:
            o = layer // 2
            x = _pool(x, *place, row(p["norm_mix_g"][layer]), p["w_pool"][o],
                      row(p["pool_scale"][o]))
        final_g = row(p["final_norm_g"]) if layer == depth - 1 else None
        x = _ffn(x, *place_ffn, row(p["norm_ffn_g"][layer]), p["ffn"][layer], final_g)
    return x, finals


def kernel(x_prompt, x_sample, state_delta, c, c_ctx, norm_mix_g, norm_ffn_g, w_ada, b_ada,
           w_in_ab, conv_qkv, a_log, dt_bias, o_norm_g, conv_b, w_out_ab, w_pool, pool_scale,
           w_ffn_gate, w_ffn_up, ffn_conv, w_ffn_down, final_norm_g):
    batch, seq, d = x_prompt.shape
    dec_batch, dec_seq, _ = x_sample.shape
    depth = w_ada.shape[0]
    n_even = w_in_ab.shape[0]
    assert d == D_MODEL and depth == 2 and n_even == 1

    p = {
        "norm_mix_g": norm_mix_g, "norm_ffn_g": norm_ffn_g, "o_norm_g": o_norm_g,
        "pool_scale": pool_scale, "final_norm_g": final_norm_g,
        "w_pool": w_pool.astype(BF16),
        "even": [_even_layer_weights(w_in_ab[e], conv_qkv[e], a_log[e], dt_bias[e], conv_b[e],
                                     w_out_ab[e]) for e in range(n_even)],
        "ffn": [_ffn_weights(w_ffn_gate[l], w_ffn_up[l], ffn_conv[l], w_ffn_down[l])
                for l in range(depth)],
    }

    n_cond = 1 + dec_batch
    pad = (-n_cond) % HALO
    cond = jnp.concatenate([c_ctx[None, :], c, jnp.zeros((pad, d), F32)], axis=0)
    mod = _modulation(cond, w_ada, b_ada)

    y_prompt, finals = _trunk(x_prompt.reshape(batch * seq, d), None, mod, 0, 0, batch, seq,
                              min(seq, 256), 1024, None, True, p)
    pe = _sincos_2d(dec_seq // GRID_W, GRID_W, d)
    y_sample, _ = _trunk(x_sample.reshape(dec_batch * dec_seq, d), pe, mod, 1, 1, dec_batch,
                         dec_seq, 512, 1024, state_delta, False, p)
    return (y_prompt.reshape(batch, seq, d), y_sample.reshape(dec_batch, dec_seq, d),
            finals[:, None])
```

```python
import functools

import jax
import jax.numpy as jnp
from jax import lax
from jax.experimental import pallas as pl
from jax.experimental.pallas import tpu as pltpu

F32 = jnp.float32
BF16 = jnp.bfloat16

D_MODEL = 1024
GRID_W = 64
A_WIDTH = 512
B_WIDTH = 512
DK = 128
DV = 128
H_A = 4
N_DIR = 2
N_POOL_GROUPS = 4
POOL_GROUP = D_MODEL // N_POOL_GROUPS
POOL_WINDOWS = (2, 4, 8, 16)
D_FF = 2816
N_MOD = 6
EPS = 1e-6
QKV_END = 3 * A_WIDTH
GATE_END = QKV_END + A_WIDTH
BETA_END = GATE_END + N_DIR * H_A
ALPHA_END = BETA_END + N_DIR * H_A
BG_END = ALPHA_END + B_WIDTH
CG_END = BG_END + B_WIDTH
IN_AB = CG_END + B_WIDTH

SUBLANES = 8
HALO = SUBLANES
CHUNK = 128
FF_CHUNK = 256
N_FF_CHUNKS = D_FF // FF_CHUNK
N_SMALL = 2 * N_DIR * H_A
SEQ_GROUP = 4
VMEM_LIMIT = 56 * 1024 * 1024
NEG_BIG = -1e30


def _sigmoid(x):
    return 1.0 / (1.0 + jnp.exp(-x))


def _silu(x):
    return x * _sigmoid(x)


def _softplus(x):
    return jnp.maximum(x, 0.0) + jnp.log1p(jnp.exp(-jnp.abs(x)))


def _dot(a, b):
    return jnp.dot(a, b, preferred_element_type=F32)


def _dot_nt(a, b):
    return lax.dot_general(a, b, (((1,), (1,)), ((), ())), preferred_element_type=F32)


def _dot_tn(a, b):
    return lax.dot_general(a, b, (((0,), (0,)), ((), ())), preferred_element_type=F32)


def _split3(a):
    a1 = a.astype(BF16)
    r1 = a - a1.astype(F32)
    a2 = r1.astype(BF16)
    a3 = (r1 - a2.astype(F32)).astype(BF16)
    return a1, a2, a3


def _norm_mod(x, g, scale, shift):
    ms = jnp.mean(x * x, axis=-1, keepdims=True)
    return (x * lax.rsqrt(ms + EPS)) * g * (1.0 + scale) + shift


def _shift_rows(a, k):
    n = a.shape[0]
    return pltpu.roll(a, (-k) % n, axis=0)


def _dwconv3_ext(p_ext, w, tm, edge_masks=None):
    prev = _shift_rows(p_ext, -1)[HALO:HALO + tm]
    cur = p_ext[HALO:HALO + tm]
    nxt = _shift_rows(p_ext, 1)[HALO:HALO + tm]
    if edge_masks is not None:
        prev = prev * edge_masks[0]
        nxt = nxt * edge_masks[1]
    return prev * w[0:1] + cur * w[1:2] + nxt * w[2:3]


def _seq_edge_masks(tm, seq_len, width):
    if tm <= seq_len:
        return None
    pos = lax.broadcasted_iota(jnp.int32, (tm, width), 0) % seq_len
    return (jnp.where(pos != 0, 1.0, 0.0).astype(F32),
            jnp.where(pos != seq_len - 1, 1.0, 0.0).astype(F32))


def _halo_valid(tm, seq_len):
    i = pl.program_id(0)
    vp = jnp.where((i * tm) % seq_len != 0, 1.0, 0.0).astype(F32)
    vn = jnp.where(((i + 1) * tm) % seq_len != 0, 1.0, 0.0).astype(F32)
    return vp, vn


def _const_spec(shape):
    nd = len(shape)
    return pl.BlockSpec(shape, lambda *_: (0,) * nd)


def _tile_specs(tm, width, n_rows):
    per = tm // HALO
    last = n_rows // HALO - 1
    prev = pl.BlockSpec((HALO, width), lambda i: (jnp.maximum(i * per - 1, 0), 0))
    main = pl.BlockSpec((tm, width), lambda i: (i, 0))
    nxt = pl.BlockSpec((HALO, width), lambda i: (jnp.minimum((i + 1) * per, last), 0))
    return [prev, main, nxt]


def _pe_specs(tm, width, seq_len):
    per = tm // HALO
    tiles = seq_len // tm
    last = seq_len // HALO - 1
    prev = pl.BlockSpec((HALO, width), lambda i: (jnp.maximum((i % tiles) * per - 1, 0), 0))
    main = pl.BlockSpec((tm, width), lambda i: (i % tiles, 0))
    nxt = pl.BlockSpec((HALO, width), lambda i: (jnp.minimum((i % tiles + 1) * per, last), 0))
    return [prev, main, nxt]


def _mod_spec(tm, seq_len, row_base, row_stride):
    return pl.BlockSpec((1, 1, N_MOD * D_MODEL),
                        lambda i: (row_base + ((i * tm) // seq_len) * row_stride, 0, 0))


def _params(n_axes=1, semantics=None):
    return pltpu.CompilerParams(
        dimension_semantics=semantics or ("parallel",) * n_axes,
        vmem_limit_bytes=VMEM_LIMIT)


def _mod_kernel(c_ref, w_ref, b_ref, o_ref):
    c = c_ref[...]
    s1, s2, _ = _split3(_silu(c))
    w = w_ref[0]
    w1 = w.astype(BF16)
    w2 = (w - w1.astype(F32)).astype(BF16)
    o_ref[0] = _dot(s1, w1) + (_dot(s2, w1) + _dot(s1, w2)) + b_ref[0]


def _modulation(cond, w_ada, b_ada):
    depth, _, width = w_ada.shape
    rows = cond.shape[0]
    tn = 1536
    return pl.pallas_call(
        _mod_kernel,
        grid=(depth, width // tn),
        in_specs=[pl.BlockSpec((rows, D_MODEL), lambda l, j: (0, 0)),
                  pl.BlockSpec((1, D_MODEL, tn), lambda l, j: (l, 0, j)),
                  pl.BlockSpec((1, 1, tn), lambda l, j: (l, 0, j))],
        out_specs=pl.BlockSpec((1, rows, tn), lambda l, j: (l, 0, j)),
        out_shape=jax.ShapeDtypeStruct((depth, rows, width), F32),
        compiler_params=_params(2),
        name="modulation",
    )(cond, w_ada, b_ada.reshape(depth, 1, width))


def _inproj_kernel(*refs, tm, seq_len, has_pe):
    if has_pe:
        xp, xm, xn, pp, pm, pn = refs[:6]
        refs = refs[6:]
    else:
        xp, xm, xn = refs[:3]
        pp = pm = pn = None
        refs = refs[3:]
    (mod_ref, ng_ref, w_ref, wsr_ref, cq_ref, cb_ref, pr_ref,
     q_ref, k_ref, v_ref, gate_ref, gbr_ref, yb_ref) = refs

    vp, vn = _halo_valid(tm, seq_len)
    mod = mod_ref[0]
    shift = mod[:, 0:D_MODEL]
    scale = mod[:, D_MODEL:2 * D_MODEL]
    g = ng_ref[...]

    def prep(x_ref, pe_ref):
        x = x_ref[...]
        if has_pe:
            x = x + pe_ref[...]
        return _norm_mod(x, g, scale, shift)

    h_main = prep(xm, pm)
    h_ext = jnp.concatenate([prep(xp, pp) * vp, h_main, prep(xn, pn) * vn], axis=0)
    hb_ext = h_ext.astype(BF16)
    hb = h_main.astype(BF16)

    cq = cq_ref[...]
    outs = (q_ref, k_ref, v_ref)
    projected = [_dot(hb_ext, w_ref[:, part * A_WIDTH:(part + 1) * A_WIDTH])
                 for part in range(3)]
    for part in range(3):
        cols = slice(part * A_WIDTH, (part + 1) * A_WIDTH)
        a = _silu(_dwconv3_ext(projected[part], cq[:, cols], tm))
        if part < 2:
            heads = []
            for h in range(H_A):
                ah = a[:, h * DK:(h + 1) * DK]
                ss = jnp.sum(ah * ah, axis=-1, keepdims=True)
                nrm = lax.rsqrt(ss + EPS)
                if part == 0:
                    nrm = nrm * (DK ** -0.5)
                heads.append(ah * nrm)
            a = jnp.concatenate(heads, axis=1)
        outs[part][...] = a.astype(BF16)

    gate_ref[...] = _dot(hb, w_ref[:, QKV_END:GATE_END]).astype(BF16)

    pr = pr_ref[...]
    bar = _dot_nt(wsr_ref[...], hb)
    sub = lax.broadcasted_iota(jnp.int32, bar.shape, 0)
    gbr_ref[...] = jnp.where(sub < N_DIR * H_A, _sigmoid(bar),
                             -pr[:, 0:1] * _softplus(bar + pr[:, 1:2]))

    pb = _dot(hb_ext, w_ref[:, GATE_END:])
    bg = pb[HALO:HALO + tm, 0:B_WIDTH]
    cghx = pb[:, B_WIDTH:2 * B_WIDTH] * pb[:, 2 * B_WIDTH:3 * B_WIDTH]
    yb_ref[...] = (bg * _dwconv3_ext(cghx, cb_ref[...], tm)).astype(BF16)


def _inproj(x, pe, mod3, row_base, row_stride, seq_len, tm, ng, wts):
    n_rows = x.shape[0]
    has_pe = pe is not None
    in_specs = _tile_specs(tm, D_MODEL, n_rows)
    args = [x, x, x]
    if has_pe:
        in_specs += _pe_specs(tm, D_MODEL, seq_len)
        args += [pe, pe, pe]
    in_specs += [_mod_spec(tm, seq_len, row_base, row_stride), _const_spec((1, D_MODEL)),
                 _const_spec((D_MODEL, GATE_END + 3 * B_WIDTH)), _const_spec((N_SMALL, D_MODEL)),
                 _const_spec((3, QKV_END)), _const_spec((3, B_WIDTH)), _const_spec((N_SMALL, 2))]
    args += [mod3, ng, wts["w_main"], wts["w_small_r"], wts["conv_qkv"], wts["conv_b"],
             wts["p_row"]]
    wide = pl.BlockSpec((tm, A_WIDTH), lambda i: (i, 0))
    out_specs = [wide, wide, wide, wide, pl.BlockSpec((N_SMALL, tm), lambda i: (0, i)), wide]
    wide_shape = jax.ShapeDtypeStruct((n_rows, A_WIDTH), BF16)
    out_shape = [wide_shape] * 4 + [jax.ShapeDtypeStruct((N_SMALL, n_rows), F32), wide_shape]
    return pl.pallas_call(
        functools.partial(_inproj_kernel, tm=tm, seq_len=seq_len, has_pe=has_pe),
        grid=(n_rows // tm,), in_specs=in_specs, out_specs=out_specs, out_shape=out_shape,
        compiler_params=_params(1), name="inproj",
    )(*args)


def _dot3_left(a, b_exact):
    a1, a2, a3 = _split3(a)
    return _dot(a1, b_exact) + (_dot(a2, b_exact) + _dot(a3, b_exact))


def _chunk_setup(q_ref, k_ref, v_ref, gbr_ref, qg_ref, kg_ref, at_ref, gl_ref,
                 m_scr, rhs_scr, *, n_sub):
    c = CHUNK
    n_beta = N_DIR * H_A
    r1 = lax.broadcasted_iota(jnp.int32, (c, c), 0)
    c1 = lax.broadcasted_iota(jnp.int32, (c, c), 1)
    tri_lo = jnp.where(r1 >= c1, 1.0, 0.0).astype(BF16)
    tri_up = jnp.where(r1 <= c1, 1.0, 0.0).astype(BF16)
    r2 = lax.broadcasted_iota(jnp.int32, (2 * c, 2 * c), 0)
    c2 = lax.broadcasted_iota(jnp.int32, (2 * c, 2 * c), 1)
    same_head = (r2 ^ c2) < c

    def col_pair(a, j):
        col = jnp.concatenate([a[:, j:j + 1], a[:, j + 1:j + 2]], axis=0)
        return jnp.broadcast_to(col, (2 * c, c))

    def row_pair(a, j):
        return jnp.concatenate([a[j:j + 1, :], a[j + 1:j + 2, :]], axis=1)

    def twice(a):
        return jnp.concatenate([a, a], axis=1)

    n = 0
    for ci in range(n_sub):
        rows_ci = slice(ci * c, (ci + 1) * c)
        gbr = gbr_ref[:, rows_ci]
        sub = lax.broadcasted_iota(jnp.int32, gbr.shape, 0)
        gc_r = jnp.where(sub < n_beta + H_A, _dot3_left(gbr, tri_up), _dot3_left(gbr, tri_lo))
        total = jnp.broadcast_to(jnp.sum(gbr, axis=1, keepdims=True), gbr.shape)
        gl_ref[ci] = jnp.exp(total)[n_beta:, :]
        pad = jnp.zeros((c - 2 * N_SMALL, c), F32)
        cols_c = jnp.concatenate([gbr, gc_r, pad], axis=0).T
        gbc = cols_c[:, 0:N_SMALL]
        gc_c = cols_c[:, N_SMALL:2 * N_SMALL]

        q = q_ref[rows_ci, :]
        k = k_ref[rows_ci, :]
        v = v_ref[rows_ci, :]
        for hp in range(H_A // 2):
            h0 = 2 * hp

            def stack(a):
                return jnp.concatenate([a[:, h0 * DK:(h0 + 1) * DK],
                                        a[:, (h0 + 1) * DK:(h0 + 2) * DK]], axis=0)

            q2b, k2b, v2b = stack(q), stack(k), stack(v)
            kk = _dot_nt(k2b, k2b)
            qk = _dot_nt(q2b, k2b)
            for d in range(N_DIR):
                jb = d * H_A + h0
                jg = n_beta + jb
                gcb = col_pair(gc_c, jg)
                beta = col_pair(gbc, jb)
                tot = jnp.concatenate([jnp.broadcast_to(total[jg:jg + 1, :], (c, c)),
                                       jnp.broadcast_to(total[jg + 1:jg + 2, :], (c, c))], axis=0)
                eg = jnp.exp(gcb)
                er = jnp.exp(tot - gcb)
                tri = (r2 >= c2) if d == 0 else (r2 <= c2)
                decay = jnp.exp(jnp.where(same_head & tri, twice(gcb) - row_pair(gc_r, jg), NEG_BIG))
                attn = qk * decay
                m_scr[n] = jnp.where(r2 != c2, (twice(beta) * kk) * decay, 0.0)
                rhs_scr[n] = jnp.concatenate([v2b * beta.astype(BF16),
                                              k2b * (beta * eg).astype(BF16)], axis=1)
                qg = q2b * eg.astype(BF16)
                kg = k2b * er.astype(BF16)
                for hh in range(2):
                    rows = slice(hh * c, (hh + 1) * c)
                    cols = slice((h0 + hh) * DK, (h0 + hh + 1) * DK)
                    qg_ref[d, rows_ci, cols] = qg[rows]
                    kg_ref[d, rows_ci, cols] = kg[rows]
                    at_ref[d, rows_ci, cols] = attn[rows, hh * c:(hh + 1) * c].astype(BF16)
                n += 1


def _chunk_solve(m_scr, rhs_scr, gsel_ref, u_ref, w_ref, *, n_sub):
    c = CHUNK
    chains = [{"ci": ci, "h0": 2 * hp, "d": d}
              for ci in range(n_sub) for hp in range(H_A // 2) for d in range(N_DIR)]
    for n, ch in enumerate(chains):
        ch["m"] = m_scr[n]
    r2 = lax.broadcasted_iota(jnp.int32, (2 * c, 2 * c), 0)
    c2 = lax.broadcasted_iota(jnp.int32, (2 * c, 2 * c), 1)
    x2 = r2 ^ c2

    sub8 = SUBLANES
    per_half = c // sub8
    lane8 = lax.broadcasted_iota(jnp.int32, (sub8, c), 1)
    row8 = lax.broadcasted_iota(jnp.int32, (sub8, c), 0)
    in_blk = [(lane8 >> 3) == i for i in range(per_half)]
    zeros8 = jnp.zeros((sub8, c), F32)
    packed = []
    for ch in chains:
        halves = []
        for half in range(2):
            pm = zeros8
            for i in range(per_half):
                blk = half * per_half + i
                pm = jnp.where(in_blk[i], ch["m"][blk * sub8:(blk + 1) * sub8,
                                                  half * c:(half + 1) * c], pm)
            halves.append(pm)
        packed.append(jnp.concatenate(halves, axis=1))
    stacked = jnp.concatenate(packed, axis=0).astype(BF16)
    spread = [_dot(stacked, gsel_ref[k]) for k in range(sub8)]
    eye8 = jnp.where((lane8 & (sub8 - 1)) == row8, 1.0, 0.0).astype(F32)
    for n, ch in enumerate(chains):
        rows_n = slice(n * sub8, (n + 1) * sub8)
        tp = jnp.concatenate([eye8, eye8], axis=1)
        order = range(sub8 - 1) if ch["d"] == 0 else range(sub8 - 1, 0, -1)
        for k in order:
            tp = tp - spread[k][rows_n] * tp[k:k + 1, :]
        row_groups = []
        for half in range(2):
            tph = tp[:, half * c:(half + 1) * c]
            for i in range(per_half):
                own = jnp.where(in_blk[i], tph, 0.0)
                row_groups.append(jnp.concatenate([own, zeros8] if half == 0 else [zeros8, own],
                                                  axis=1))
        ch["t"] = jnp.concatenate(row_groups, axis=0)

    def pick(a, b, parity):
        return jnp.concatenate([a[i * b:(i + 1) * b] for i in range(a.shape[0] // b)
                                if i % 2 == parity], axis=0)

    def weave(sel, rest, b, parity):
        blocks = []
        for i in range(2 * sel.shape[0] // b):
            src = sel if i % 2 == parity else rest
            blocks.append(src[(i // 2) * b:(i // 2 + 1) * b])
        return jnp.concatenate(blocks, axis=0)

    par = [1 - ch["d"] for ch in chains]
    for lvl in range(3, 7):
        b = 1 << lvl
        xs = []
        for ch, p in zip(chains, par):
            off = jnp.where((pick(x2, b, p) >> lvl) == 1, pick(ch["m"], b, p), 0.0)
            xs.append(_dot(off.astype(BF16), ch["t"].astype(BF16)))
        for ch, x, p in zip(chains, xs, par):
            x_full = weave(x, jnp.zeros_like(x), b, p).astype(BF16)
            t_sel = pick(ch["t"], b, p)
            t_new = t_sel - _dot(t_sel.astype(BF16), x_full)
            ch["t"] = weave(t_new, pick(ch["t"], b, 1 - p), b, p)

    for n, ch in enumerate(chains):
        uw = _dot(ch["t"].astype(BF16), rhs_scr[n])
        rows_ci = slice(ch["ci"] * c, (ch["ci"] + 1) * c)
        for hh in range(2):
            rows = slice(hh * c, (hh + 1) * c)
            cols = slice((ch["h0"] + hh) * DK, (ch["h0"] + hh + 1) * DK)
            u_ref[ch["d"], rows_ci, cols] = uw[rows, 0:DV].astype(BF16)
            w_ref[ch["d"], rows_ci, cols] = uw[rows, DV:2 * DV].astype(BF16)


def _chunk_local_kernel(q_ref, k_ref, v_ref, gbr_ref, gsel_ref,
                        u_ref, w_ref, qg_ref, kg_ref, at_ref, gl_ref,
                        m_a, rhs_a, m_b, rhs_b, *, n_sub):
    setup = functools.partial(_chunk_setup, q_ref, k_ref, v_ref, gbr_ref,
                              qg_ref, kg_ref, at_ref, gl_ref, n_sub=n_sub)
    solve = functools.partial(_chunk_solve, gsel_ref=gsel_ref, u_ref=u_ref, w_ref=w_ref,
                              n_sub=n_sub)
    j = pl.program_id(0)

    @pl.when(j == 0)
    def _():
        u_ref[...] = jnp.zeros(u_ref.shape, u_ref.dtype)
        w_ref[...] = jnp.zeros(w_ref.shape, w_ref.dtype)
        setup(m_a, rhs_a)

    @pl.when(j % 2 == 1)
    def _():
        solve(m_a, rhs_a)
        setup(m_b, rhs_b)

    @pl.when((j > 0) & (j % 2 == 0))
    def _():
        solve(m_b, rhs_b)
        setup(m_a, rhs_a)


def _chunk_local(q, k, v, gbr, n_sub=4):
    n_rows = q.shape[0]
    n_chunks = n_rows // CHUNK
    rows = n_sub * CHUNK
    n_blocks = n_chunks // n_sub
    n_sys = n_sub * (H_A // 2) * N_DIR
    cur = lambda j: jnp.minimum(j, n_blocks - 1)
    prev = lambda j: jnp.maximum(j - 1, 0)
    wide = pl.BlockSpec((rows, A_WIDTH), lambda j: (cur(j), 0))
    out_cur = pl.BlockSpec((N_DIR, rows, A_WIDTH), lambda j: (0, cur(j), 0))
    out_prev = pl.BlockSpec((N_DIR, rows, A_WIDTH), lambda j: (0, prev(j), 0))
    big = lambda dt: jax.ShapeDtypeStruct((N_DIR, n_rows, A_WIDTH), dt)
    idx = jnp.arange(2 * CHUNK)
    gsel = ((idx[None, :, None] // SUBLANES == idx[None, None, :] // SUBLANES)
            & (idx[None, :, None] % SUBLANES == jnp.arange(SUBLANES)[:, None, None])).astype(BF16)
    return pl.pallas_call(
        functools.partial(_chunk_local_kernel, n_sub=n_sub),
        grid=(n_blocks + 1,),
        in_specs=[wide, wide, wide,
                  pl.BlockSpec((N_SMALL, rows), lambda j: (0, cur(j))),
                  _const_spec((SUBLANES, 2 * CHUNK, 2 * CHUNK))],
        out_specs=[out_prev, out_prev, out_cur, out_cur, out_cur,
                   pl.BlockSpec((n_sub, N_DIR * H_A, CHUNK), lambda j: (cur(j), 0, 0))],
        out_shape=[big(BF16)] * 5 + [jax.ShapeDtypeStruct((n_chunks, N_DIR * H_A, CHUNK), F32)],
        scratch_shapes=[pltpu.VMEM((n_sys, 2 * CHUNK, 2 * CHUNK), F32),
                        pltpu.VMEM((n_sys, 2 * CHUNK, 2 * CHUNK), BF16),
                        pltpu.VMEM((n_sys, 2 * CHUNK, 2 * CHUNK), F32),
                        pltpu.VMEM((n_sys, 2 * CHUNK, 2 * CHUNK), BF16)],
        compiler_params=_params(1, ("arbitrary",)), name="chunk_local",
    )(q, k, v, gbr, gsel)


def _scan_kernel(*refs, n_chunks, zero_init, write_final):
    fwd = refs[0:6]
    bwd = refs[6:12]
    refs = refs[12:]
    if not zero_init:
        s0_ref = refs[0]
        refs = refs[1:]
    of_ref, ob_ref = refs[0:2]
    refs = refs[2:]
    if write_final:
        sfin_ref = refs[0]
        refs = refs[1:]
    s_scr = refs[0]
    i = pl.program_id(1)

    @pl.when(i == 0)
    def _():
        if zero_init:
            s_scr[...] = jnp.zeros(s_scr.shape, F32)
        else:
            s_scr[...] = s0_ref[...]

    problems = [(s, d, h) for s in range(SEQ_GROUP) for d in range(N_DIR) for h in range(H_A)]
    c = CHUNK
    stage1 = []
    for s, d, h in problems:
        u_ref, w_ref, qg_ref, _, _, _ = fwd if d == 0 else bwd
        cols = slice(h * DV, (h + 1) * DV)
        sb = s_scr[s, d, h].astype(BF16)
        lhs = jnp.concatenate([w_ref[0, s, :, cols], qg_ref[0, s, :, cols]], axis=0)
        stage1.append(_dot(lhs, sb))
    for (s, d, h), ws_qs in zip(problems, stage1):
        u_ref, _, _, kg_ref, at_ref, gl_ref = fwd if d == 0 else bwd
        o_ref = of_ref if d == 0 else ob_ref
        cols = slice(h * DV, (h + 1) * DV)
        vb = (u_ref[0, s, :, cols].astype(F32) - ws_qs[0:c]).astype(BF16)
        o_ref[s, :, cols] = (ws_qs[c:2 * c] + _dot(at_ref[0, s, :, cols], vb)).astype(BF16)
        gl = gl_ref[s, 0, d * H_A + h:d * H_A + h + 1, :]
        s_scr[s, d, h] = s_scr[s, d, h] * gl + _dot_tn(kg_ref[0, s, :, cols], vb)

    if write_final:
        @pl.when(i == n_chunks - 1)
        def _():
            sfin_ref[...] = s_scr[...]


def _scan(local, s0, n_seq, seq_len, write_final):
    u, w, qg, kg, at, gl = local
    n_chunks = seq_len // CHUNK
    zero_init = s0 is None
    shp4 = (N_DIR, n_seq, seq_len, A_WIDTH)
    arrs = [a.reshape(shp4) for a in (u, w, qg, kg, at)]
    gl4 = gl.reshape(n_seq, n_chunks, N_DIR * H_A, CHUNK)
    sg = SEQ_GROUP

    def specs(d, chunk_of):
        big = pl.BlockSpec((1, sg, CHUNK, A_WIDTH), lambda g, i: (d, g, chunk_of(i), 0))
        return [big] * 5 + [pl.BlockSpec((sg, 1, N_DIR * H_A, CHUNK),
                                         lambda g, i: (g, chunk_of(i), 0, 0))]

    rev = lambda i: n_chunks - 1 - i
    in_specs = specs(0, lambda i: i) + specs(1, rev)
    args = arrs + [gl4] + arrs + [gl4]
    state_spec = pl.BlockSpec((sg, N_DIR, H_A, DK, DV), lambda g, i: (g, 0, 0, 0, 0))
    if not zero_init:
        in_specs.append(state_spec)
        args.append(s0)
    o_shape = jax.ShapeDtypeStruct((n_seq, seq_len, A_WIDTH), BF16)
    out_specs = [pl.BlockSpec((sg, CHUNK, A_WIDTH), lambda g, i: (g, i, 0)),
                 pl.BlockSpec((sg, CHUNK, A_WIDTH), lambda g, i: (g, rev(i), 0))]
    out_shape = [o_shape, o_shape]
    if write_final:
        out_specs.append(state_spec)
        out_shape.append(jax.ShapeDtypeStruct((n_seq, N_DIR, H_A, DK, DV), F32))
    return pl.pallas_call(
        functools.partial(_scan_kernel, n_chunks=n_chunks, zero_init=zero_init,
                          write_final=write_final),
        grid=(n_seq // sg, n_chunks), in_specs=in_specs, out_specs=out_specs, out_shape=out_shape,
        scratch_shapes=[pltpu.VMEM((sg, N_DIR, H_A, DK, DV), F32)],
        compiler_params=_params(2, ("parallel", "arbitrary")), name="delta_scan",
    )(*args)


def _outproj_kernel(*refs, has_pe):
    if has_pe:
        x_ref, pe_ref = refs[:2]
        refs = refs[2:]
    else:
        x_ref = refs[0]
        refs = refs[1:]
    of_ref, ob_ref, gate_ref, yb_ref, mod_ref, og_ref, wout_ref, out_ref = refs
    o = of_ref[...].astype(F32) + ob_ref[...].astype(F32)
    sg = _silu(gate_ref[...].astype(F32))
    og = og_ref[...]
    parts = []
    for h in range(H_A):
        cols = slice(h * DV, (h + 1) * DV)
        oh = o[:, cols]
        ms = jnp.mean(oh * oh, axis=-1, keepdims=True)
        parts.append((oh * lax.rsqrt(ms + EPS)) * og * sg[:, cols])
    a = jnp.concatenate([part.astype(BF16) for part in parts] + [yb_ref[...]], axis=1)
    y = _dot(a, wout_ref[...])
    x = x_ref[...]
    if has_pe:
        x = x + pe_ref[...]
    gate1 = mod_ref[0][:, 2 * D_MODEL:3 * D_MODEL]
    out_ref[...] = x + gate1 * y


def _outproj(x, pe, o_f, o_b, gate, yb, mod3, row_base, row_stride, seq_len, tm, og, w_out):
    n_rows = x.shape[0]
    has_pe = pe is not None
    full = pl.BlockSpec((tm, D_MODEL), lambda i: (i, 0))
    half = pl.BlockSpec((tm, A_WIDTH), lambda i: (i, 0))
    in_specs = [full]
    args = [x]
    if has_pe:
        tiles = seq_len // tm
        in_specs.append(pl.BlockSpec((tm, D_MODEL), lambda i: (i % tiles, 0)))
        args.append(pe)
    in_specs += [half, half, half, half, _mod_spec(tm, seq_len, row_base, row_stride),
                 _const_spec((1, DV)), _const_spec((D_MODEL, D_MODEL))]
    args += [o_f, o_b, gate, yb, mod3, og, w_out]
    return pl.pallas_call(
        functools.partial(_outproj_kernel, has_pe=has_pe),
        grid=(n_rows // tm,), in_specs=in_specs, out_specs=full,
        out_shape=jax.ShapeDtypeStruct((n_rows, D_MODEL), F32),
        compiler_params=_params(1), name="outproj",
    )(*args)


def _ffn_kernel(*refs, tm, seq_len, final_norm):
    xp, xm, xn, mod_ref, ng_ref, wg_ref, wu_ref, wd_ref, cf_ref = refs[:9]
    refs = refs[9:]
    if final_norm:
        fg_ref = refs[0]
        refs = refs[1:]
    out_ref = refs[0]
    vp, vn = _halo_valid(tm, seq_len)
    mod = mod_ref[0]
    shift = mod[:, 3 * D_MODEL:4 * D_MODEL]
    scale = mod[:, 4 * D_MODEL:5 * D_MODEL]
    gate2 = mod[:, 5 * D_MODEL:6 * D_MODEL]
    g = ng_ref[...]
    x = xm[...]
    h_main = _norm_mod(x, g, scale, shift)
    h_ext = jnp.concatenate([_norm_mod(xp[...], g, scale, shift) * vp, h_main,
                             _norm_mod(xn[...], g, scale, shift) * vn], axis=0)
    hb_ext = h_ext.astype(BF16)
    hb = h_main.astype(BF16)
    edge_masks = _seq_edge_masks(tm, seq_len, FF_CHUNK)
    acts = []
    for j in range(N_FF_CHUNKS):
        cols = slice(j * FF_CHUNK, (j + 1) * FF_CHUNK)
        u = _dwconv3_ext(_dot(hb_ext, wg_ref[:, cols]), cf_ref[:, cols], tm, edge_masks)
        acts.append((_silu(u) * _dot(hb, wu_ref[:, cols])).astype(BF16))
    y = x + gate2 * _dot(jnp.concatenate(acts, axis=1), wd_ref[...])
    if final_norm:
        ms = jnp.mean(y * y, axis=-1, keepdims=True)
        y = (y * lax.rsqrt(ms + EPS)) * fg_ref[...]
    out_ref[...] = y


def _ffn(x, mod3, row_base, row_stride, seq_len, tm, ng, wts, layer, final_g):
    n_rows = x.shape[0]
    final_norm = final_g is not None
    of_layer = lambda *shape: pl.BlockSpec((None,) + shape, lambda i: (layer, 0, 0))
    in_specs = _tile_specs(tm, D_MODEL, n_rows)
    in_specs += [_mod_spec(tm, seq_len, row_base, row_stride), _const_spec((1, D_MODEL)),
                 of_layer(D_MODEL, D_FF), of_layer(D_MODEL, D_FF), of_layer(D_FF, D_MODEL),
                 of_layer(3, D_FF)]
    args = [x, x, x, mod3, ng, wts["w_gate"], wts["w_up"], wts["w_down"], wts["conv"]]
    if final_norm:
        in_specs.append(_const_spec((1, D_MODEL)))
        args.append(final_g)
    return pl.pallas_call(
        functools.partial(_ffn_kernel, tm=tm, seq_len=seq_len, final_norm=final_norm),
        grid=(n_rows // tm,), in_specs=in_specs,
        out_specs=pl.BlockSpec((tm, D_MODEL), lambda i: (i, 0)),
        out_shape=jax.ShapeDtypeStruct((n_rows, D_MODEL), F32),
        compiler_params=_params(1), name="conv_ffn",
    )(*args)


def _pool_kernel(xp, xm, xn, mod_ref, ng_ref, wp_ref, ps_ref, out_ref, *, tm, seq_len):
    vp, vn = _halo_valid(tm, seq_len)
    mod = mod_ref[0]
    shift = mod[:, 0:D_MODEL]
    scale = mod[:, D_MODEL:2 * D_MODEL]
    gate1 = mod[:, 2 * D_MODEL:3 * D_MODEL]
    g = ng_ref[...]
    x = xm[...]
    h_main = _norm_mod(x, g, scale, shift)
    h_ext = jnp.concatenate([_norm_mod(xp[...], g, scale, shift) * vp, h_main,
                             _norm_mod(xn[...], g, scale, shift) * vn], axis=0)
    i = pl.program_id(0)
    pos = (i * tm + lax.broadcasted_iota(jnp.int32, (tm, POOL_GROUP // 2), 0)) % seq_len
    ys = []
    for gi, win in enumerate(POOL_WINDOWS):
        cols = slice(gi * POOL_GROUP, (gi + 1) * POOL_GROUP)
        s = h_ext[:, cols]
        width = 1
        while width < win:
            s = s + _shift_rows(s, width)
            width *= 2
        start = HALO - win // 2
        if start:
            s = _shift_rows(s, start)
        s = s[0:tm]
        lo = jnp.maximum(pos - win // 2, 0)
        hi = jnp.minimum(pos - win // 2 + win, seq_len)
        count = (hi - lo).astype(F32)
        mixed = (s / jnp.concatenate([count, count], axis=1) - h_main[:, cols]).astype(BF16)
        ys.append(_dot(mixed, wp_ref[gi]))
    y = jnp.concatenate(ys, axis=1) * ps_ref[...]
    out_ref[...] = x + gate1 * y


def _pool(x, mod3, row_base, row_stride, seq_len, tm, ng, w_pool, pool_scale):
    n_rows = x.shape[0]
    in_specs = _tile_specs(tm, D_MODEL, n_rows)
    in_specs += [_mod_spec(tm, seq_len, row_base, row_stride), _const_spec((1, D_MODEL)),
                 _const_spec((N_POOL_GROUPS, POOL_GROUP, POOL_GROUP)), _const_spec((1, D_MODEL))]
    return pl.pallas_call(
        functools.partial(_pool_kernel, tm=tm, seq_len=seq_len),
        grid=(n_rows // tm,), in_specs=in_specs,
        out_specs=pl.BlockSpec((tm, D_MODEL), lambda i: (i, 0)),
        out_shape=jax.ShapeDtypeStruct((n_rows, D_MODEL), F32),
        compiler_params=_params(1), name="pool_mixer",
    )(x, x, x, mod3, ng, w_pool, pool_scale)


def _sincos_2d(rows, cols, d):
    quarter = d // 4
    omega = 1.0 / (10000.0 ** (jnp.arange(quarter, dtype=F32) / quarter))
    er = jnp.arange(rows, dtype=F32)[:, None] * omega[None, :]
    ec = jnp.arange(cols, dtype=F32)[:, None] * omega[None, :]
    er = jnp.concatenate([jnp.sin(er), jnp.cos(er)], axis=-1)
    ec = jnp.concatenate([jnp.sin(ec), jnp.cos(ec)], axis=-1)
    pe = jnp.concatenate([jnp.broadcast_to(er[:, None, :], (rows, cols, d // 2)),
                          jnp.broadcast_to(ec[None, :, :], (rows, cols, d // 2))], axis=-1)
    return pe.reshape(rows * cols, d)


def _even_layer_weights(w_in, conv_qkv, a_log, dt_bias, conv_b, w_out):
    small = w_in[:, GATE_END:ALPHA_END]
    zeros = jnp.zeros((N_DIR * H_A,), F32)
    ea = jnp.concatenate([zeros, jnp.exp(a_log.astype(F32)).reshape(-1)])
    dtb = jnp.concatenate([zeros, dt_bias.astype(F32).reshape(-1)])
    return {
        "w_main": jnp.concatenate([w_in[:, :GATE_END], w_in[:, ALPHA_END:IN_AB]],
                                  axis=1).astype(BF16),
        "w_small_r": small.T.astype(BF16),
        "conv_qkv": conv_qkv, "conv_b": conv_b,
        "p_row": jnp.stack([ea, dtb], axis=1),
        "w_out": w_out.astype(BF16),
    }


def _ffn_weights(w_gate, w_up, conv, w_down):
    return {"w_gate": w_gate.astype(BF16), "w_up": w_up.astype(BF16),
            "w_down": w_down.astype(BF16), "conv": conv}


def _trunk(x, pe, mod, row_base, row_stride, n_seq, seq_len, tm, tm_ffn, s0, write_final, p):
    row = lambda a: a.reshape(1, -1)
    finals = None
    depth = mod.shape[0]
    for layer in range(depth):
        mod3 = mod[layer][:, None, :]
        place = (mod3, row_base, row_stride, seq_len, tm)
        place_ffn = (mod3, row_base, row_stride, seq_len, tm_ffn)
        if layer % 2 == 0:
            e = layer // 2
            wts = p["even"][e]
            q, k, v, gate, gbr, yb = _inproj(x, pe, *place, row(p["norm_mix_g"][layer]), wts)
            local = _chunk_local(q, k, v, gbr)
            res = _scan(local, None if s0 is None else s0[:, e], n_seq, seq_len, write_final)
            o_f = res[0].reshape(-1, A_WIDTH)
            o_b = res[1].reshape(-1, A_WIDTH)
            if write_final:
                finals = res[2]
            x = _outproj(x, pe, o_f, o_b, gate, yb, *place, row(p["o_norm_g"][e]), wts["w_out"])
        else:
            o = layer // 2
            x = _pool(x, *place, row(p["norm_mix_g"][layer]), p["w_pool"][o],
                      row(p["pool_scale"][o]))
        final_g = row(p["final_norm_g"]) if layer == depth - 1 else None
        x = _ffn(x, *place_ffn, row(p["norm_ffn_g"][layer]), p["ffn"], layer, final_g)
    return x, finals


def kernel(x_prompt, x_sample, state_delta, c, c_ctx, norm_mix_g, norm_ffn_g, w_ada, b_ada,
           w_in_ab, conv_qkv, a_log, dt_bias, o_norm_g, conv_b, w_out_ab, w_pool, pool_scale,
           w_ffn_gate, w_ffn_up, ffn_conv, w_ffn_down, final_norm_g):
    batch, seq, d = x_prompt.shape
    dec_batch, dec_seq, _ = x_sample.shape
    depth = w_ada.shape[0]
    n_even = w_in_ab.shape[0]
    assert d == D_MODEL and depth == 2 and n_even == 1

    p = {
        "norm_mix_g": norm_mix_g, "norm_ffn_g": norm_ffn_g, "o_norm_g": o_norm_g,
        "pool_scale": pool_scale, "final_norm_g": final_norm_g,
        "w_pool": w_pool.astype(BF16),
        "even": [_even_layer_weights(w_in_ab[e], conv_qkv[e], a_log[e], dt_bias[e], conv_b[e],
                                     w_out_ab[e]) for e in range(n_even)],
        "ffn": _ffn_weights(w_ffn_gate, w_ffn_up, ffn_conv, w_ffn_down),
    }

    n_cond = 1 + dec_batch
    pad = (-n_cond) % HALO
    cond = jnp.concatenate([c_ctx[None, :], c, jnp.zeros((pad, d), F32)], axis=0)
    mod = _modulation(cond, w_ada, b_ada)

    y_prompt, finals = _trunk(x_prompt.reshape(batch * seq, d), None, mod, 0, 0, batch, seq,
                              min(seq, 256), 1024, None, True, p)
    pe = _sincos_2d(dec_seq // GRID_W, GRID_W, d)
    y_sample, _ = _trunk(x_sample.reshape(dec_batch * dec_seq, d), pe, mod, 1, 1, dec_batch,
                         dec_seq, 512, 1024, state_delta, False, p)
    return (y_prompt.reshape(batch, seq, d), y_sample.reshape(dec_batch, dec_seq, d),
            finals[:, None])
```

```python
import functools

import jax
import jax.numpy as jnp
from jax import lax
from jax.experimental import pallas as pl
from jax.experimental.pallas import tpu as pltpu

F32 = jnp.float32
BF16 = jnp.bfloat16

D_MODEL = 1024
GRID_W = 64
A_WIDTH = 512
B_WIDTH = 512
DK = 128
DV = 128
H_A = 4
N_DIR = 2
N_POOL_GROUPS = 4
POOL_GROUP = D_MODEL // N_POOL_GROUPS
POOL_WINDOWS = (2, 4, 8, 16)
D_FF = 2816
N_MOD = 6
EPS = 1e-6
QKV_END = 3 * A_WIDTH
GATE_END = QKV_END + A_WIDTH
BETA_END = GATE_END + N_DIR * H_A
ALPHA_END = BETA_END + N_DIR * H_A
BG_END = ALPHA_END + B_WIDTH
CG_END = BG_END + B_WIDTH
IN_AB = CG_END + B_WIDTH

SUBLANES = 8
HALO = SUBLANES
CHUNK = 128
FF_CHUNK = 256
N_FF_CHUNKS = D_FF // FF_CHUNK
N_SMALL = 2 * N_DIR * H_A
SEQ_GROUP = 4
VMEM_LIMIT = 56 * 1024 * 1024
NEG_BIG = -1e30


def _sigmoid(x):
    return 1.0 / (1.0 + jnp.exp(-x))


def _silu(x):
    return x * _sigmoid(x)


def _softplus(x):
    return jnp.maximum(x, 0.0) + jnp.log1p(jnp.exp(-jnp.abs(x)))


def _dot(a, b):
    return jnp.dot(a, b, preferred_element_type=F32)


def _dot_nt(a, b):
    return lax.dot_general(a, b, (((1,), (1,)), ((), ())), preferred_element_type=F32)


def _dot_tn(a, b):
    return lax.dot_general(a, b, (((0,), (0,)), ((), ())), preferred_element_type=F32)


def _split3(a):
    a1 = a.astype(BF16)
    r1 = a - a1.astype(F32)
    a2 = r1.astype(BF16)
    a3 = (r1 - a2.astype(F32)).astype(BF16)
    return a1, a2, a3


def _norm_mod(x, g, scale, shift):
    ms = jnp.mean(x * x, axis=-1, keepdims=True)
    return (x * lax.rsqrt(ms + EPS)) * g * (1.0 + scale) + shift


def _shift_rows(a, k):
    n = a.shape[0]
    return pltpu.roll(a, (-k) % n, axis=0)


def _dwconv3_ext(p_ext, w, tm, edge_masks=None):
    prev = _shift_rows(p_ext, -1)[HALO:HALO + tm]
    cur = p_ext[HALO:HALO + tm]
    nxt = _shift_rows(p_ext, 1)[HALO:HALO + tm]
    if edge_masks is not None:
        prev = prev * edge_masks[0]
        nxt = nxt * edge_masks[1]
    return prev * w[0:1] + cur * w[1:2] + nxt * w[2:3]


def _seq_edge_masks(tm, seq_len, width):
    if tm <= seq_len:
        return None
    pos = lax.broadcasted_iota(jnp.int32, (tm, width), 0) % seq_len
    return (jnp.where(pos != 0, 1.0, 0.0).astype(F32),
            jnp.where(pos != seq_len - 1, 1.0, 0.0).astype(F32))


def _halo_valid(tm, seq_len):
    i = pl.program_id(0)
    vp = jnp.where((i * tm) % seq_len != 0, 1.0, 0.0).astype(F32)
    vn = jnp.where(((i + 1) * tm) % seq_len != 0, 1.0, 0.0).astype(F32)
    return vp, vn


def _const_spec(shape):
    nd = len(shape)
    return pl.BlockSpec(shape, lambda *_: (0,) * nd)


def _tile_specs(tm, width, n_rows, halo=HALO):
    per = tm // halo
    last = n_rows // halo - 1
    prev = pl.BlockSpec((halo, width), lambda i: (jnp.maximum(i * per - 1, 0), 0))
    main = pl.BlockSpec((tm, width), lambda i: (i, 0))
    nxt = pl.BlockSpec((halo, width), lambda i: (jnp.minimum((i + 1) * per, last), 0))
    return [prev, main, nxt]


def _pe_specs(tm, width, seq_len):
    per = tm // HALO
    tiles = seq_len // tm
    last = seq_len // HALO - 1
    prev = pl.BlockSpec((HALO, width), lambda i: (jnp.maximum((i % tiles) * per - 1, 0), 0))
    main = pl.BlockSpec((tm, width), lambda i: (i % tiles, 0))
    nxt = pl.BlockSpec((HALO, width), lambda i: (jnp.minimum((i % tiles + 1) * per, last), 0))
    return [prev, main, nxt]


def _mod_spec(tm, seq_len, row_base, row_stride):
    return pl.BlockSpec((1, 1, N_MOD * D_MODEL),
                        lambda i: (row_base + ((i * tm) // seq_len) * row_stride, 0, 0))


def _params(n_axes=1, semantics=None):
    return pltpu.CompilerParams(
        dimension_semantics=semantics or ("parallel",) * n_axes,
        vmem_limit_bytes=VMEM_LIMIT)


def _mod_kernel(c_ref, w_ref, b_ref, o_ref):
    c = c_ref[...]
    s1, s2, _ = _split3(_silu(c))
    w = w_ref[0]
    w1 = w.astype(BF16)
    w2 = (w - w1.astype(F32)).astype(BF16)
    o_ref[0] = _dot(s1, w1) + (_dot(s2, w1) + _dot(s1, w2)) + b_ref[0]


def _modulation(cond, w_ada, b_ada):
    depth, _, width = w_ada.shape
    rows = cond.shape[0]
    tn = 1536
    return pl.pallas_call(
        _mod_kernel,
        grid=(depth, width // tn),
        in_specs=[pl.BlockSpec((rows, D_MODEL), lambda l, j: (0, 0)),
                  pl.BlockSpec((1, D_MODEL, tn), lambda l, j: (l, 0, j)),
                  pl.BlockSpec((1, 1, tn), lambda l, j: (l, 0, j))],
        out_specs=pl.BlockSpec((1, rows, tn), lambda l, j: (l, 0, j)),
        out_shape=jax.ShapeDtypeStruct((depth, rows, width), F32),
        compiler_params=_params(2),
        name="modulation",
    )(cond, w_ada, b_ada.reshape(depth, 1, width))


def _inproj_kernel(*refs, tm, seq_len, has_pe):
    if has_pe:
        xp, xm, xn, pp, pm, pn = refs[:6]
        refs = refs[6:]
    else:
        xp, xm, xn = refs[:3]
        pp = pm = pn = None
        refs = refs[3:]
    (mod_ref, ng_ref, w_ref, wsr_ref, cq_ref, cb_ref, pr_ref,
     q_ref, k_ref, v_ref, gate_ref, gbr_ref, yb_ref) = refs

    vp, vn = _halo_valid(tm, seq_len)
    mod = mod_ref[0]
    shift = mod[:, 0:D_MODEL]
    scale = mod[:, D_MODEL:2 * D_MODEL]
    g = ng_ref[...]

    def prep(x_ref, pe_ref):
        x = x_ref[...]
        if has_pe:
            x = x + pe_ref[...]
        return _norm_mod(x, g, scale, shift)

    h_main = prep(xm, pm)
    h_ext = jnp.concatenate([prep(xp, pp) * vp, h_main, prep(xn, pn) * vn], axis=0)
    hb_ext = h_ext.astype(BF16)
    hb = h_main.astype(BF16)

    def project(lhs, group):
        return _dot(lhs, w_ref[:, group * A_WIDTH:(group + 1) * A_WIDTH])

    cq = cq_ref[...]
    outs = (q_ref, k_ref, v_ref)
    short = []
    for part in range(3):
        cols = slice(part * A_WIDTH, (part + 1) * A_WIDTH)
        p = project(hb_ext, part)
        short.append(project(hb_ext, 4 + part))
        a = _silu(_dwconv3_ext(p, cq[:, cols], tm))
        if part < 2:
            heads = []
            for h in range(H_A):
                ah = a[:, h * DK:(h + 1) * DK]
                ss = jnp.sum(ah * ah, axis=-1, keepdims=True)
                nrm = lax.rsqrt(ss + EPS)
                if part == 0:
                    nrm = nrm * (DK ** -0.5)
                heads.append(ah * nrm)
            a = jnp.concatenate(heads, axis=1)
        outs[part][...] = a.astype(BF16)

    gate_ref[...] = project(hb, 3).astype(BF16)
    bg, cg, hx = short
    yb_ref[...] = (bg[HALO:HALO + tm] * _dwconv3_ext(cg * hx, cb_ref[...], tm)).astype(BF16)

    pr = pr_ref[...]
    bar = _dot_nt(wsr_ref[...], hb)
    sub = lax.broadcasted_iota(jnp.int32, bar.shape, 0)
    gbr_ref[...] = jnp.where(sub < N_DIR * H_A, _sigmoid(bar),
                             -pr[:, 0:1] * _softplus(bar + pr[:, 1:2]))


def _inproj(x, pe, mod3, row_base, row_stride, seq_len, tm, ng, wts):
    n_rows = x.shape[0]
    has_pe = pe is not None
    in_specs = _tile_specs(tm, D_MODEL, n_rows)
    args = [x, x, x]
    if has_pe:
        in_specs += _pe_specs(tm, D_MODEL, seq_len)
        args += [pe, pe, pe]
    in_specs += [_mod_spec(tm, seq_len, row_base, row_stride), _const_spec((1, D_MODEL)),
                 _const_spec((D_MODEL, GATE_END + 3 * B_WIDTH)), _const_spec((N_SMALL, D_MODEL)),
                 _const_spec((3, QKV_END)), _const_spec((3, B_WIDTH)), _const_spec((N_SMALL, 2))]
    args += [mod3, ng, wts["w_main"], wts["w_small_r"], wts["conv_qkv"], wts["conv_b"],
             wts["p_row"]]
    wide = pl.BlockSpec((tm, A_WIDTH), lambda i: (i, 0))
    out_specs = [wide, wide, wide, wide, pl.BlockSpec((N_SMALL, tm), lambda i: (0, i)), wide]
    wide_shape = jax.ShapeDtypeStruct((n_rows, A_WIDTH), BF16)
    out_shape = [wide_shape] * 4 + [jax.ShapeDtypeStruct((N_SMALL, n_rows), F32), wide_shape]
    return pl.pallas_call(
        functools.partial(_inproj_kernel, tm=tm, seq_len=seq_len, has_pe=has_pe),
        grid=(n_rows // tm,), in_specs=in_specs, out_specs=out_specs, out_shape=out_shape,
        compiler_params=_params(1), name="inproj",
    )(*args)


def _dot3_left(a, b_exact):
    a1, a2, a3 = _split3(a)
    return _dot(a1, b_exact) + (_dot(a2, b_exact) + _dot(a3, b_exact))


def _chunk_setup(q_ref, k_ref, v_ref, gbr_ref, qg_ref, kg_ref, at_ref, gl_ref,
                 m_scr, rhs_scr, *, n_sub):
    c = CHUNK
    n_beta = N_DIR * H_A
    r1 = lax.broadcasted_iota(jnp.int32, (c, c), 0)
    c1 = lax.broadcasted_iota(jnp.int32, (c, c), 1)
    tri_lo = jnp.where(r1 >= c1, 1.0, 0.0).astype(BF16)
    tri_up = jnp.where(r1 <= c1, 1.0, 0.0).astype(BF16)
    r2 = lax.broadcasted_iota(jnp.int32, (2 * c, 2 * c), 0)
    c2 = lax.broadcasted_iota(jnp.int32, (2 * c, 2 * c), 1)
    same_head = (r2 ^ c2) < c

    def col_pair(a, j):
        col = jnp.concatenate([a[:, j:j + 1], a[:, j + 1:j + 2]], axis=0)
        return jnp.broadcast_to(col, (2 * c, c))

    def row_pair(a, j):
        return jnp.concatenate([a[j:j + 1, :], a[j + 1:j + 2, :]], axis=1)

    def twice(a):
        return jnp.concatenate([a, a], axis=1)

    n = 0
    for ci in range(n_sub):
        rows_ci = slice(ci * c, (ci + 1) * c)
        gbr = gbr_ref[:, rows_ci]
        sub = lax.broadcasted_iota(jnp.int32, gbr.shape, 0)
        gc_r = jnp.where(sub < n_beta + H_A, _dot3_left(gbr, tri_up), _dot3_left(gbr, tri_lo))
        total = jnp.broadcast_to(jnp.sum(gbr, axis=1, keepdims=True), gbr.shape)
        gl_ref[ci] = jnp.exp(total)[n_beta:, :]
        pad = jnp.zeros((c - 2 * N_SMALL, c), F32)
        cols_c = jnp.concatenate([gbr, gc_r, pad], axis=0).T
        gbc = cols_c[:, 0:N_SMALL]
        gc_c = cols_c[:, N_SMALL:2 * N_SMALL]

        q = q_ref[rows_ci, :]
        k = k_ref[rows_ci, :]
        v = v_ref[rows_ci, :]
        for hp in range(H_A // 2):
            h0 = 2 * hp

            def stack(a):
                return jnp.concatenate([a[:, h0 * DK:(h0 + 1) * DK],
                                        a[:, (h0 + 1) * DK:(h0 + 2) * DK]], axis=0)

            q2b, k2b, v2b = stack(q), stack(k), stack(v)
            kk = _dot_nt(k2b, k2b)
            qk = _dot_nt(q2b, k2b)
            for d in range(N_DIR):
                jb = d * H_A + h0
                jg = n_beta + jb
                gcb = col_pair(gc_c, jg)
                beta = col_pair(gbc, jb)
                tot = jnp.concatenate([jnp.broadcast_to(total[jg:jg + 1, :], (c, c)),
                                       jnp.broadcast_to(total[jg + 1:jg + 2, :], (c, c))], axis=0)
                eg = jnp.exp(gcb)
                er = jnp.exp(tot - gcb)
                tri = (r2 >= c2) if d == 0 else (r2 <= c2)
                decay = jnp.exp(jnp.where(same_head & tri, twice(gcb) - row_pair(gc_r, jg), NEG_BIG))
                attn = qk * decay
                m_scr[n] = jnp.where(r2 != c2, (twice(beta) * kk) * decay, 0.0)
                rhs_scr[n] = jnp.concatenate([v2b * beta.astype(BF16),
                                              k2b * (beta * eg).astype(BF16)], axis=1)
                qg = q2b * eg.astype(BF16)
                kg = k2b * er.astype(BF16)
                for hh in range(2):
                    rows = slice(hh * c, (hh + 1) * c)
                    cols = slice((h0 + hh) * DK, (h0 + hh + 1) * DK)
                    qg_ref[d, rows_ci, cols] = qg[rows]
                    kg_ref[d, rows_ci, cols] = kg[rows]
                    at_ref[d, rows_ci, cols] = attn[rows, hh * c:(hh + 1) * c].astype(BF16)
                n += 1


def _chunk_solve(m_scr, rhs_scr, gsel_ref, u_ref, w_ref, *, n_sub):
    c = CHUNK
    chains = [{"ci": ci, "h0": 2 * hp, "d": d}
              for ci in range(n_sub) for hp in range(H_A // 2) for d in range(N_DIR)]
    for n, ch in enumerate(chains):
        ch["m"] = m_scr[n]
    r2 = lax.broadcasted_iota(jnp.int32, (2 * c, 2 * c), 0)
    c2 = lax.broadcasted_iota(jnp.int32, (2 * c, 2 * c), 1)
    x2 = r2 ^ c2

    sub8 = SUBLANES
    per_half = c // sub8
    lane8 = lax.broadcasted_iota(jnp.int32, (sub8, c), 1)
    row8 = lax.broadcasted_iota(jnp.int32, (sub8, c), 0)
    in_blk = [(lane8 >> 3) == i for i in range(per_half)]
    zeros8 = jnp.zeros((sub8, c), F32)
    packed = []
    for ch in chains:
        halves = []
        for half in range(2):
            pm = zeros8
            for i in range(per_half):
                blk = half * per_half + i
                pm = jnp.where(in_blk[i], ch["m"][blk * sub8:(blk + 1) * sub8,
                                                  half * c:(half + 1) * c], pm)
            halves.append(pm)
        packed.append(jnp.concatenate(halves, axis=1))
    stacked = jnp.concatenate(packed, axis=0).astype(BF16)
    spread = [_dot(stacked, gsel_ref[k]) for k in range(sub8)]
    eye8 = jnp.where((lane8 & (sub8 - 1)) == row8, 1.0, 0.0).astype(F32)
    for n, ch in enumerate(chains):
        rows_n = slice(n * sub8, (n + 1) * sub8)
        tp = jnp.concatenate([eye8, eye8], axis=1)
        order = range(sub8 - 1) if ch["d"] == 0 else range(sub8 - 1, 0, -1)
        for k in order:
            tp = tp - spread[k][rows_n] * tp[k:k + 1, :]
        row_groups = []
        for half in range(2):
            tph = tp[:, half * c:(half + 1) * c]
            for i in range(per_half):
                own = jnp.where(in_blk[i], tph, 0.0)
                row_groups.append(jnp.concatenate([own, zeros8] if half == 0 else [zeros8, own],
                                                  axis=1))
        ch["t"] = jnp.concatenate(row_groups, axis=0)

    def pick(a, b, parity):
        return jnp.concatenate([a[i * b:(i + 1) * b] for i in range(a.shape[0] // b)
                                if i % 2 == parity], axis=0)

    def weave(sel, rest, b, parity):
        blocks = []
        for i in range(2 * sel.shape[0] // b):
            src = sel if i % 2 == parity else rest
            blocks.append(src[(i // 2) * b:(i // 2 + 1) * b])
        return jnp.concatenate(blocks, axis=0)

    par = [1 - ch["d"] for ch in chains]
    for lvl in range(3, 7):
        b = 1 << lvl
        xs = []
        for ch, p in zip(chains, par):
            off = jnp.where((pick(x2, b, p) >> lvl) == 1, pick(ch["m"], b, p), 0.0)
            xs.append(_dot(off.astype(BF16), ch["t"].astype(BF16)))
        for ch, x, p in zip(chains, xs, par):
            x_full = weave(x, jnp.zeros_like(x), b, p).astype(BF16)
            t_sel = pick(ch["t"], b, p)
            t_new = t_sel - _dot(t_sel.astype(BF16), x_full)
            ch["t"] = weave(t_new, pick(ch["t"], b, 1 - p), b, p)

    for n, ch in enumerate(chains):
        uw = _dot(ch["t"].astype(BF16), rhs_scr[n])
        rows_ci = slice(ch["ci"] * c, (ch["ci"] + 1) * c)
        for hh in range(2):
            rows = slice(hh * c, (hh + 1) * c)
            cols = slice((ch["h0"] + hh) * DK, (ch["h0"] + hh + 1) * DK)
            u_ref[ch["d"], rows_ci, cols] = uw[rows, 0:DV].astype(BF16)
            w_ref[ch["d"], rows_ci, cols] = uw[rows, DV:2 * DV].astype(BF16)


def _chunk_local_kernel(q_ref, k_ref, v_ref, gbr_ref, gsel_ref,
                        u_ref, w_ref, qg_ref, kg_ref, at_ref, gl_ref,
                        m_a, rhs_a, m_b, rhs_b, *, n_sub):
    setup = functools.partial(_chunk_setup, q_ref, k_ref, v_ref, gbr_ref,
                              qg_ref, kg_ref, at_ref, gl_ref, n_sub=n_sub)
    solve = functools.partial(_chunk_solve, gsel_ref=gsel_ref, u_ref=u_ref, w_ref=w_ref,
                              n_sub=n_sub)
    j = pl.program_id(0)

    @pl.when(j == 0)
    def _():
        u_ref[...] = jnp.zeros(u_ref.shape, u_ref.dtype)
        w_ref[...] = jnp.zeros(w_ref.shape, w_ref.dtype)
        setup(m_a, rhs_a)

    @pl.when(j % 2 == 1)
    def _():
        solve(m_a, rhs_a)
        setup(m_b, rhs_b)

    @pl.when((j > 0) & (j % 2 == 0))
    def _():
        solve(m_b, rhs_b)
        setup(m_a, rhs_a)


def _chunk_local(q, k, v, gbr, n_sub=4):
    n_rows = q.shape[0]
    n_chunks = n_rows // CHUNK
    rows = n_sub * CHUNK
    n_blocks = n_chunks // n_sub
    n_sys = n_sub * (H_A // 2) * N_DIR
    cur = lambda j: jnp.minimum(j, n_blocks - 1)
    prev = lambda j: jnp.maximum(j - 1, 0)
    wide = pl.BlockSpec((rows, A_WIDTH), lambda j: (cur(j), 0))
    out_cur = pl.BlockSpec((N_DIR, rows, A_WIDTH), lambda j: (0, cur(j), 0))
    out_prev = pl.BlockSpec((N_DIR, rows, A_WIDTH), lambda j: (0, prev(j), 0))
    big = lambda dt: jax.ShapeDtypeStruct((N_DIR, n_rows, A_WIDTH), dt)
    idx = jnp.arange(2 * CHUNK)
    gsel = ((idx[None, :, None] // SUBLANES == idx[None, None, :] // SUBLANES)
            & (idx[None, :, None] % SUBLANES == jnp.arange(SUBLANES)[:, None, None])).astype(BF16)
    return pl.pallas_call(
        functools.partial(_chunk_local_kernel, n_sub=n_sub),
        grid=(n_blocks + 1,),
        in_specs=[wide, wide, wide,
                  pl.BlockSpec((N_SMALL, rows), lambda j: (0, cur(j))),
                  _const_spec((SUBLANES, 2 * CHUNK, 2 * CHUNK))],
        out_specs=[out_prev, out_prev, out_cur, out_cur, out_cur,
                   pl.BlockSpec((n_sub, N_DIR * H_A, CHUNK), lambda j: (cur(j), 0, 0))],
        out_shape=[big(BF16)] * 5 + [jax.ShapeDtypeStruct((n_chunks, N_DIR * H_A, CHUNK), F32)],
        scratch_shapes=[pltpu.VMEM((n_sys, 2 * CHUNK, 2 * CHUNK), F32),
                        pltpu.VMEM((n_sys, 2 * CHUNK, 2 * CHUNK), BF16),
                        pltpu.VMEM((n_sys, 2 * CHUNK, 2 * CHUNK), F32),
                        pltpu.VMEM((n_sys, 2 * CHUNK, 2 * CHUNK), BF16)],
        compiler_params=_params(1, ("arbitrary",)), name="chunk_local",
    )(q, k, v, gbr, gsel)


def _scan_kernel(*refs, n_chunks, zero_init, write_final):
    fwd = refs[0:6]
    bwd = refs[6:12]
    refs = refs[12:]
    if not zero_init:
        s0_ref = refs[0]
        refs = refs[1:]
    of_ref, ob_ref = refs[0:2]
    refs = refs[2:]
    if write_final:
        sfin_ref = refs[0]
        refs = refs[1:]
    s_scr = refs[0]
    i = pl.program_id(1)

    @pl.when(i == 0)
    def _():
        if zero_init:
            s_scr[...] = jnp.zeros(s_scr.shape, F32)
        else:
            s_scr[...] = s0_ref[...]

    problems = [(s, d, h) for s in range(SEQ_GROUP) for d in range(N_DIR) for h in range(H_A)]
    c = CHUNK
    stage1 = []
    for s, d, h in problems:
        u_ref, w_ref, qg_ref, _, _, _ = fwd if d == 0 else bwd
        cols = slice(h * DV, (h + 1) * DV)
        sb = s_scr[s, d, h].astype(BF16)
        lhs = jnp.concatenate([w_ref[0, s, :, cols], qg_ref[0, s, :, cols]], axis=0)
        stage1.append(_dot(lhs, sb))
    for (s, d, h), ws_qs in zip(problems, stage1):
        u_ref, _, _, kg_ref, at_ref, gl_ref = fwd if d == 0 else bwd
        o_ref = of_ref if d == 0 else ob_ref
        cols = slice(h * DV, (h + 1) * DV)
        vb = (u_ref[0, s, :, cols].astype(F32) - ws_qs[0:c]).astype(BF16)
        o_ref[s, :, cols] = (ws_qs[c:2 * c] + _dot(at_ref[0, s, :, cols], vb)).astype(BF16)
        gl = gl_ref[s, 0, d * H_A + h:d * H_A + h + 1, :]
        s_scr[s, d, h] = s_scr[s, d, h] * gl + _dot_tn(kg_ref[0, s, :, cols], vb)

    if write_final:
        @pl.when(i == n_chunks - 1)
        def _():
            sfin_ref[...] = s_scr[...]


def _scan(local, s0, n_seq, seq_len, write_final):
    u, w, qg, kg, at, gl = local
    n_chunks = seq_len // CHUNK
    zero_init = s0 is None
    shp4 = (N_DIR, n_seq, seq_len, A_WIDTH)
    arrs = [a.reshape(shp4) for a in (u, w, qg, kg, at)]
    gl4 = gl.reshape(n_seq, n_chunks, N_DIR * H_A, CHUNK)
    sg = SEQ_GROUP

    def specs(d, chunk_of):
        big = pl.BlockSpec((1, sg, CHUNK, A_WIDTH), lambda g, i: (d, g, chunk_of(i), 0))
        return [big] * 5 + [pl.BlockSpec((sg, 1, N_DIR * H_A, CHUNK),
                                         lambda g, i: (g, chunk_of(i), 0, 0))]

    rev = lambda i: n_chunks - 1 - i
    in_specs = specs(0, lambda i: i) + specs(1, rev)
    args = arrs + [gl4] + arrs + [gl4]
    state_spec = pl.BlockSpec((sg, N_DIR, H_A, DK, DV), lambda g, i: (g, 0, 0, 0, 0))
    if not zero_init:
        in_specs.append(state_spec)
        args.append(s0)
    o_shape = jax.ShapeDtypeStruct((n_seq, seq_len, A_WIDTH), BF16)
    out_specs = [pl.BlockSpec((sg, CHUNK, A_WIDTH), lambda g, i: (g, i, 0)),
                 pl.BlockSpec((sg, CHUNK, A_WIDTH), lambda g, i: (g, rev(i), 0))]
    out_shape = [o_shape, o_shape]
    if write_final:
        out_specs.append(state_spec)
        out_shape.append(jax.ShapeDtypeStruct((n_seq, N_DIR, H_A, DK, DV), F32))
    return pl.pallas_call(
        functools.partial(_scan_kernel, n_chunks=n_chunks, zero_init=zero_init,
                          write_final=write_final),
        grid=(n_seq // sg, n_chunks), in_specs=in_specs, out_specs=out_specs, out_shape=out_shape,
        scratch_shapes=[pltpu.VMEM((sg, N_DIR, H_A, DK, DV), F32)],
        compiler_params=_params(2, ("parallel", "arbitrary")), name="delta_scan",
    )(*args)


def _delta_mixer_rows(o_f, o_b, gate, yb, og):
    o = o_f + o_b
    sg = _silu(gate)
    parts = []
    for h in range(H_A):
        cols = slice(h * DV, (h + 1) * DV)
        oh = o[:, cols]
        ms = jnp.mean(oh * oh, axis=-1, keepdims=True)
        parts.append((oh * lax.rsqrt(ms + EPS)) * og * sg[:, cols])
    return jnp.concatenate(parts + [yb], axis=1)


def _ffn_kernel(*refs, tm, seq_len, final_norm, delta, has_pe):
    xp, xm, xn = refs[:3]
    refs = refs[3:]
    if delta:
        if has_pe:
            pe_refs = refs[:3]
            refs = refs[3:]
        mix_refs = refs[:12]
        og_ref, wout_ref = refs[12:14]
        refs = refs[14:]
    mod_ref, ng_ref, wg_ref, wu_ref, wd_ref, cf_ref = refs[:6]
    refs = refs[6:]
    if final_norm:
        fg_ref = refs[0]
        refs = refs[1:]
    out_ref = refs[0]
    vp, vn = _halo_valid(tm, seq_len)
    mod = mod_ref[0]
    shift = mod[:, 3 * D_MODEL:4 * D_MODEL]
    scale = mod[:, 4 * D_MODEL:5 * D_MODEL]
    gate2 = mod[:, 5 * D_MODEL:6 * D_MODEL]
    g = ng_ref[...]
    if delta:
        def mixer_rows(which):
            vals = []
            for a in range(4):
                r = mix_refs[3 * a + which][...].astype(F32)
                if which == 0:
                    r = r[HALO:2 * HALO]
                elif which == 2:
                    r = r[0:HALO]
                vals.append(r)
            return _delta_mixer_rows(*vals, og_ref[...])

        a_ext = jnp.concatenate([mixer_rows(0), mixer_rows(1), mixer_rows(2)], axis=0)
        x_ext = jnp.concatenate([xp[...], xm[...], xn[...]], axis=0)
        if has_pe:
            x_ext = x_ext + jnp.concatenate([r[...] for r in pe_refs], axis=0)
        gate1 = mod[:, 2 * D_MODEL:3 * D_MODEL]
        x_ext = x_ext + gate1 * _dot(a_ext.astype(BF16), wout_ref[...])
        x = x_ext[HALO:HALO + tm]
        h_all = _norm_mod(x_ext, g, scale, shift)
        h_main = h_all[HALO:HALO + tm]
        h_ext = jnp.concatenate([h_all[0:HALO] * vp, h_main, h_all[HALO + tm:] * vn], axis=0)
    else:
        x = xm[...]
        h_main = _norm_mod(x, g, scale, shift)
        h_ext = jnp.concatenate([_norm_mod(xp[...], g, scale, shift) * vp, h_main,
                                 _norm_mod(xn[...], g, scale, shift) * vn], axis=0)
    hb_ext = h_ext.astype(BF16)
    hb = h_main.astype(BF16)
    edge_masks = _seq_edge_masks(tm, seq_len, FF_CHUNK)
    acts = []
    for j in range(N_FF_CHUNKS):
        cols = slice(j * FF_CHUNK, (j + 1) * FF_CHUNK)
        u = _dwconv3_ext(_dot(hb_ext, wg_ref[:, cols]), cf_ref[:, cols], tm, edge_masks)
        acts.append((_silu(u) * _dot(hb, wu_ref[:, cols])).astype(BF16))
    y = x + gate2 * _dot(jnp.concatenate(acts, axis=1), wd_ref[...])
    if final_norm:
        ms = jnp.mean(y * y, axis=-1, keepdims=True)
        y = (y * lax.rsqrt(ms + EPS)) * fg_ref[...]
    out_ref[...] = y


def _ffn(x, mod3, row_base, row_stride, seq_len, tm, ng, wts, layer, final_g, delta=None):
    n_rows = x.shape[0]
    final_norm = final_g is not None
    of_layer = lambda *shape: pl.BlockSpec((None,) + shape, lambda i: (layer, 0, 0))
    in_specs = _tile_specs(tm, D_MODEL, n_rows)
    args = [x, x, x]
    has_pe = False
    if delta is not None:
        pe, o_f, o_b, gate, yb, og, w_out = delta
        has_pe = pe is not None
        if has_pe:
            in_specs += _pe_specs(tm, D_MODEL, seq_len)
            args += [pe, pe, pe]
        for arr in (o_f, o_b, gate, yb):
            in_specs += _tile_specs(tm, A_WIDTH, n_rows, halo=2 * HALO)
            args += [arr, arr, arr]
        in_specs += [_const_spec((1, DV)), _const_spec((D_MODEL, D_MODEL))]
        args += [og, w_out]
    in_specs += [_mod_spec(tm, seq_len, row_base, row_stride), _const_spec((1, D_MODEL)),
                 of_layer(D_MODEL, D_FF), of_layer(D_MODEL, D_FF), of_layer(D_FF, D_MODEL),
                 of_layer(3, D_FF)]
    args += [mod3, ng, wts["w_gate"], wts["w_up"], wts["w_down"], wts["conv"]]
    if final_norm:
        in_specs.append(_const_spec((1, D_MODEL)))
        args.append(final_g)
    return pl.pallas_call(
        functools.partial(_ffn_kernel, tm=tm, seq_len=seq_len, final_norm=final_norm,
                          delta=delta is not None, has_pe=has_pe),
        grid=(n_rows // tm,), in_specs=in_specs,
        out_specs=pl.BlockSpec((tm, D_MODEL), lambda i: (i, 0)),
        out_shape=jax.ShapeDtypeStruct((n_rows, D_MODEL), F32),
        compiler_params=_params(1), name="conv_ffn",
    )(*args)


def _pool_kernel(xp, xm, xn, mod_ref, ng_ref, wp_ref, ps_ref, out_ref, *, tm, seq_len):
    vp, vn = _halo_valid(tm, seq_len)
    mod = mod_ref[0]
    shift = mod[:, 0:D_MODEL]
    scale = mod[:, D_MODEL:2 * D_MODEL]
    gate1 = mod[:, 2 * D_MODEL:3 * D_MODEL]
    g = ng_ref[...]
    x = xm[...]
    h_main = _norm_mod(x, g, scale, shift)
    h_ext = jnp.concatenate([_norm_mod(xp[...], g, scale, shift) * vp, h_main,
                             _norm_mod(xn[...], g, scale, shift) * vn], axis=0)
    i = pl.program_id(0)
    pos = (i * tm + lax.broadcasted_iota(jnp.int32, (tm, POOL_GROUP // 2), 0)) % seq_len
    ys = []
    for gi, win in enumerate(POOL_WINDOWS):
        cols = slice(gi * POOL_GROUP, (gi + 1) * POOL_GROUP)
        s = h_ext[:, cols]
        width = 1
        while width < win:
            s = s + _shift_rows(s, width)
            width *= 2
        start = HALO - win // 2
        if start:
            s = _shift_rows(s, start)
        s = s[0:tm]
        lo = jnp.maximum(pos - win // 2, 0)
        hi = jnp.minimum(pos - win // 2 + win, seq_len)
        count = (hi - lo).astype(F32)
        mixed = (s / jnp.concatenate([count, count], axis=1) - h_main[:, cols]).astype(BF16)
        ys.append(_dot(mixed, wp_ref[gi]))
    y = jnp.concatenate(ys, axis=1) * ps_ref[...]
    out_ref[...] = x + gate1 * y


def _pool(x, mod3, row_base, row_stride, seq_len, tm, ng, w_pool, pool_scale):
    n_rows = x.shape[0]
    in_specs = _tile_specs(tm, D_MODEL, n_rows)
    in_specs += [_mod_spec(tm, seq_len, row_base, row_stride), _const_spec((1, D_MODEL)),
                 _const_spec((N_POOL_GROUPS, POOL_GROUP, POOL_GROUP)), _const_spec((1, D_MODEL))]
    return pl.pallas_call(
        functools.partial(_pool_kernel, tm=tm, seq_len=seq_len),
        grid=(n_rows // tm,), in_specs=in_specs,
        out_specs=pl.BlockSpec((tm, D_MODEL), lambda i: (i, 0)),
        out_shape=jax.ShapeDtypeStruct((n_rows, D_MODEL), F32),
        compiler_params=_params(1), name="pool_mixer",
    )(x, x, x, mod3, ng, w_pool, pool_scale)


def _sincos_2d(rows, cols, d):
    quarter = d // 4
    omega = 1.0 / (10000.0 ** (jnp.arange(quarter, dtype=F32) / quarter))
    er = jnp.arange(rows, dtype=F32)[:, None] * omega[None, :]
    ec = jnp.arange(cols, dtype=F32)[:, None] * omega[None, :]
    er = jnp.concatenate([jnp.sin(er), jnp.cos(er)], axis=-1)
    ec = jnp.concatenate([jnp.sin(ec), jnp.cos(ec)], axis=-1)
    pe = jnp.concatenate([jnp.broadcast_to(er[:, None, :], (rows, cols, d // 2)),
                          jnp.broadcast_to(ec[None, :, :], (rows, cols, d // 2))], axis=-1)
    return pe.reshape(rows * cols, d)


def _even_layer_weights(w_in, conv_qkv, a_log, dt_bias, conv_b, w_out):
    small = w_in[:, GATE_END:ALPHA_END]
    zeros = jnp.zeros((N_DIR * H_A,), F32)
    ea = jnp.concatenate([zeros, jnp.exp(a_log.astype(F32)).reshape(-1)])
    dtb = jnp.concatenate([zeros, dt_bias.astype(F32).reshape(-1)])
    return {
        "w_main": jnp.concatenate([w_in[:, :GATE_END], w_in[:, ALPHA_END:IN_AB]],
                                  axis=1).astype(BF16),
        "w_small_r": small.T.astype(BF16),
        "conv_qkv": conv_qkv, "conv_b": conv_b,
        "p_row": jnp.stack([ea, dtb], axis=1),
        "w_out": w_out.astype(BF16),
    }


def _ffn_weights(w_gate, w_up, conv, w_down):
    return {"w_gate": w_gate.astype(BF16), "w_up": w_up.astype(BF16),
            "w_down": w_down.astype(BF16), "conv": conv}


def _trunk(x, pe, mod, row_base, row_stride, n_seq, seq_len, tm, tm_ffn, s0, write_final, p):
    row = lambda a: a.reshape(1, -1)
    finals = None
    depth = mod.shape[0]
    for layer in range(depth):
        mod3 = mod[layer][:, None, :]
        place = (mod3, row_base, row_stride, seq_len, tm)
        place_ffn = (mod3, row_base, row_stride, seq_len, tm_ffn)
        if layer % 2 == 0:
            e = layer // 2
            wts = p["even"][e]
            q, k, v, gate, gbr, yb = _inproj(x, pe, *place, row(p["norm_mix_g"][layer]), wts)
            local = _chunk_local(q, k, v, gbr)
            res = _scan(local, None if s0 is None else s0[:, e], n_seq, seq_len, write_final)
            o_f = res[0].reshape(-1, A_WIDTH)
            o_b = res[1].reshape(-1, A_WIDTH)
            if write_final:
                finals = res[2]
            delta = (pe, o_f, o_b, gate, yb, row(p["o_norm_g"][e]), wts["w_out"])
            place_ffn = (mod3, row_base, row_stride, seq_len, tm_ffn // 2)
        else:
            o = layer // 2
            x = _pool(x, *place, row(p["norm_mix_g"][layer]), p["w_pool"][o],
                      row(p["pool_scale"][o]))
            delta = None
        final_g = row(p["final_norm_g"]) if layer == depth - 1 else None
        x = _ffn(x, *place_ffn, row(p["norm_ffn_g"][layer]), p["ffn"], layer, final_g, delta)
    return x, finals


def kernel(x_prompt, x_sample, state_delta, c, c_ctx, norm_mix_g, norm_ffn_g, w_ada, b_ada,
           w_in_ab, conv_qkv, a_log, dt_bias, o_norm_g, conv_b, w_out_ab, w_pool, pool_scale,
           w_ffn_gate, w_ffn_up, ffn_conv, w_ffn_down, final_norm_g):
    batch, seq, d = x_prompt.shape
    dec_batch, dec_seq, _ = x_sample.shape
    depth = w_ada.shape[0]
    n_even = w_in_ab.shape[0]
    assert d == D_MODEL and depth == 2 and n_even == 1

    p = {
        "norm_mix_g": norm_mix_g, "norm_ffn_g": norm_ffn_g, "o_norm_g": o_norm_g,
        "pool_scale": pool_scale, "final_norm_g": final_norm_g,
        "w_pool": w_pool.astype(BF16),
        "even": [_even_layer_weights(w_in_ab[e], conv_qkv[e], a_log[e], dt_bias[e], conv_b[e],
                                     w_out_ab[e]) for e in range(n_even)],
        "ffn": _ffn_weights(w_ffn_gate, w_ffn_up, ffn_conv, w_ffn_down),
    }

    n_cond = 1 + dec_batch
    pad = (-n_cond) % HALO
    cond = jnp.concatenate([c_ctx[None, :], c, jnp.zeros((pad, d), F32)], axis=0)
    mod = _modulation(cond, w_ada, b_ada)

    y_prompt, finals = _trunk(x_prompt.reshape(batch * seq, d), None, mod, 0, 0, batch, seq,
                              min(seq, 256), 1024, None, True, p)
    pe = _sincos_2d(dec_seq // GRID_W, GRID_W, d)
    y_sample, _ = _trunk(x_sample.reshape(dec_batch * dec_seq, d), pe, mod, 1, 1, dec_batch,
                         dec_seq, 512, 1024, state_delta, False, p)
    return (y_prompt.reshape(batch, seq, d), y_sample.reshape(dec_batch, dec_seq, d),
            finals[:, None])
```

```python
import functools

import jax
import jax.numpy as jnp
from jax import lax
from jax.experimental import pallas as pl
from jax.experimental.pallas import tpu as pltpu

F32 = jnp.float32
BF16 = jnp.bfloat16

D_MODEL = 1024
GRID_W = 64
A_WIDTH = 512
B_WIDTH = 512
DK = 128
DV = 128
H_A = 4
N_DIR = 2
N_POOL_GROUPS = 4
POOL_GROUP = D_MODEL // N_POOL_GROUPS
POOL_WINDOWS = (2, 4, 8, 16)
D_FF = 2816
N_MOD = 6
EPS = 1e-6
QKV_END = 3 * A_WIDTH
GATE_END = QKV_END + A_WIDTH
BETA_END = GATE_END + N_DIR * H_A
ALPHA_END = BETA_END + N_DIR * H_A
BG_END = ALPHA_END + B_WIDTH
CG_END = BG_END + B_WIDTH
IN_AB = CG_END + B_WIDTH

SUBLANES = 8
HALO = SUBLANES
CHUNK = 128
FF_CHUNK = 256
N_FF_CHUNKS = D_FF // FF_CHUNK
N_SMALL = 2 * N_DIR * H_A
SEQ_GROUP = 4
VMEM_LIMIT = 56 * 1024 * 1024
NEG_BIG = -1e30


def _sigmoid(x):
    return 1.0 / (1.0 + jnp.exp(-x))


def _silu(x):
    return x * _sigmoid(x)


def _softplus(x):
    return jnp.maximum(x, 0.0) + jnp.log1p(jnp.exp(-jnp.abs(x)))


def _dot(a, b):
    return jnp.dot(a, b, preferred_element_type=F32)


def _dot_nt(a, b):
    return lax.dot_general(a, b, (((1,), (1,)), ((), ())), preferred_element_type=F32)


def _dot_tn(a, b):
    return lax.dot_general(a, b, (((0,), (0,)), ((), ())), preferred_element_type=F32)


def _split3(a):
    a1 = a.astype(BF16)
    r1 = a - a1.astype(F32)
    a2 = r1.astype(BF16)
    a3 = (r1 - a2.astype(F32)).astype(BF16)
    return a1, a2, a3


def _norm_mod(x, g, scale, shift):
    ms = jnp.mean(x * x, axis=-1, keepdims=True)
    return (x * lax.rsqrt(ms + EPS)) * (g * (1.0 + scale)) + shift


def _shift_rows(a, k):
    n = a.shape[0]
    return pltpu.roll(a, (-k) % n, axis=0)


def _dwconv3_ext(p_ext, w, tm, edge_masks=None, halo=HALO):
    prev = _shift_rows(p_ext, -1)[halo:halo + tm]
    cur = p_ext[halo:halo + tm]
    nxt = _shift_rows(p_ext, 1)[halo:halo + tm]
    if edge_masks is not None:
        prev = prev * edge_masks[0]
        nxt = nxt * edge_masks[1]
    return prev * w[0:1] + cur * w[1:2] + nxt * w[2:3]


def _seq_edge_masks(tm, seq_len, width):
    if tm <= seq_len:
        return None
    pos = lax.broadcasted_iota(jnp.int32, (tm, width), 0) % seq_len
    return (jnp.where(pos != 0, 1.0, 0.0).astype(F32),
            jnp.where(pos != seq_len - 1, 1.0, 0.0).astype(F32))


def _halo_valid(tm, seq_len):
    i = pl.program_id(0)
    vp = jnp.where((i * tm) % seq_len != 0, 1.0, 0.0).astype(F32)
    vn = jnp.where(((i + 1) * tm) % seq_len != 0, 1.0, 0.0).astype(F32)
    return vp, vn


def _const_spec(shape):
    nd = len(shape)
    return pl.BlockSpec(shape, lambda *_: (0,) * nd)


def _tile_specs(tm, width, n_rows, halo=HALO):
    per = tm // halo
    last = n_rows // halo - 1
    prev = pl.BlockSpec((halo, width), lambda i: (jnp.maximum(i * per - 1, 0), 0))
    main = pl.BlockSpec((tm, width), lambda i: (i, 0))
    nxt = pl.BlockSpec((halo, width), lambda i: (jnp.minimum((i + 1) * per, last), 0))
    return [prev, main, nxt]


def _pe_specs(tm, width, seq_len):
    per = tm // HALO
    tiles = seq_len // tm
    last = seq_len // HALO - 1
    prev = pl.BlockSpec((HALO, width), lambda i: (jnp.maximum((i % tiles) * per - 1, 0), 0))
    main = pl.BlockSpec((tm, width), lambda i: (i % tiles, 0))
    nxt = pl.BlockSpec((HALO, width), lambda i: (jnp.minimum((i % tiles + 1) * per, last), 0))
    return [prev, main, nxt]


def _mod_spec(tm, seq_len, row_base, row_stride):
    return pl.BlockSpec((1, 1, N_MOD * D_MODEL),
                        lambda i: (row_base + ((i * tm) // seq_len) * row_stride, 0, 0))


def _params(n_axes=1, semantics=None):
    return pltpu.CompilerParams(
        dimension_semantics=semantics or ("parallel",) * n_axes,
        vmem_limit_bytes=VMEM_LIMIT)


def _mod_kernel(c_ref, w_ref, b_ref, o_ref):
    c = c_ref[...]
    s1, s2, _ = _split3(_silu(c))
    w = w_ref[0]
    w1 = w.astype(BF16)
    w2 = (w - w1.astype(F32)).astype(BF16)
    o_ref[0] = _dot(s1, w1) + (_dot(s2, w1) + _dot(s1, w2)) + b_ref[0]


def _modulation(cond, w_ada, b_ada):
    depth, _, width = w_ada.shape
    rows = cond.shape[0]
    tn = 1536
    return pl.pallas_call(
        _mod_kernel,
        grid=(depth, width // tn),
        in_specs=[pl.BlockSpec((rows, D_MODEL), lambda l, j: (0, 0)),
                  pl.BlockSpec((1, D_MODEL, tn), lambda l, j: (l, 0, j)),
                  pl.BlockSpec((1, 1, tn), lambda l, j: (l, 0, j))],
        out_specs=pl.BlockSpec((1, rows, tn), lambda l, j: (l, 0, j)),
        out_shape=jax.ShapeDtypeStruct((depth, rows, width), F32),
        compiler_params=_params(2),
        name="modulation",
    )(cond, w_ada, b_ada.reshape(depth, 1, width))


def _inproj_kernel(*refs, tm, seq_len, has_pe):
    if has_pe:
        xp, xm, xn, pp, pm, pn = refs[:6]
        refs = refs[6:]
    else:
        xp, xm, xn = refs[:3]
        pp = pm = pn = None
        refs = refs[3:]
    (mod_ref, ng_ref, w_ref, wsr_ref, cq_ref, cb_ref, pr_ref,
     q_ref, k_ref, v_ref, gate_ref, gbr_ref, yb_ref) = refs

    vp, vn = _halo_valid(tm, seq_len)
    mod = mod_ref[0]
    shift = mod[:, 0:D_MODEL]
    scale = mod[:, D_MODEL:2 * D_MODEL]
    g = ng_ref[...]

    def prep(x_ref, pe_ref):
        x = x_ref[...]
        if has_pe:
            x = x + pe_ref[...]
        return _norm_mod(x, g, scale, shift)

    h_main = prep(xm, pm)
    h_ext = jnp.concatenate([prep(xp, pp) * vp, h_main, prep(xn, pn) * vn], axis=0)
    hb_ext = h_ext.astype(BF16)
    hb = h_main.astype(BF16)

    slab = FF_CHUNK
    per_group = A_WIDTH // slab

    def project(lhs, group, half):
        lo = group * A_WIDTH + half * slab
        return _dot(lhs, w_ref[:, lo:lo + slab])

    cq = cq_ref[...]
    outs = (q_ref, k_ref, v_ref)
    short = []
    for part in range(3):
        for half in range(per_group):
            cols = slice(part * A_WIDTH + half * slab, part * A_WIDTH + (half + 1) * slab)
            p = project(hb_ext, part, half)
            short.append(project(hb_ext, 4 + part, half))
            a = _silu(_dwconv3_ext(p, cq[:, cols], tm))
            if part < 2:
                heads = []
                for h in range(slab // DK):
                    ah = a[:, h * DK:(h + 1) * DK]
                    ss = jnp.sum(ah * ah, axis=-1, keepdims=True)
                    nrm = lax.rsqrt(ss + EPS)
                    if part == 0:
                        nrm = nrm * (DK ** -0.5)
                    heads.append(ah * nrm)
                a = jnp.concatenate(heads, axis=1)
            outs[part][:, half * slab:(half + 1) * slab] = a.astype(BF16)

    cb = cb_ref[...]
    for half in range(per_group):
        cols = slice(half * slab, (half + 1) * slab)
        gate_ref[:, cols] = project(hb, 3, half).astype(BF16)
        bg, cg, hx = (short[part * per_group + half] for part in range(3))
        yb_ref[:, cols] = (bg[HALO:HALO + tm]
                           * _dwconv3_ext(cg * hx, cb[:, cols], tm)).astype(BF16)

    pr = pr_ref[...]
    bar = _dot_nt(wsr_ref[...], hb)
    sub = lax.broadcasted_iota(jnp.int32, bar.shape, 0)
    gbr_ref[...] = jnp.where(sub < N_DIR * H_A, _sigmoid(bar),
                             -pr[:, 0:1] * _softplus(bar + pr[:, 1:2]))


def _inproj(x, pe, mod3, row_base, row_stride, seq_len, tm, ng, wts):
    n_rows = x.shape[0]
    has_pe = pe is not None
    in_specs = _tile_specs(tm, D_MODEL, n_rows)
    args = [x, x, x]
    if has_pe:
        in_specs += _pe_specs(tm, D_MODEL, seq_len)
        args += [pe, pe, pe]
    in_specs += [_mod_spec(tm, seq_len, row_base, row_stride), _const_spec((1, D_MODEL)),
                 _const_spec((D_MODEL, GATE_END + 3 * B_WIDTH)), _const_spec((N_SMALL, D_MODEL)),
                 _const_spec((3, QKV_END)), _const_spec((3, B_WIDTH)), _const_spec((N_SMALL, 2))]
    args += [mod3, ng, wts["w_main"], wts["w_small_r"], wts["conv_qkv"], wts["conv_b"],
             wts["p_row"]]
    wide = pl.BlockSpec((tm, A_WIDTH), lambda i: (i, 0))
    out_specs = [wide, wide, wide, wide, pl.BlockSpec((N_SMALL, tm), lambda i: (0, i)), wide]
    wide_shape = jax.ShapeDtypeStruct((n_rows, A_WIDTH), BF16)
    out_shape = [wide_shape] * 4 + [jax.ShapeDtypeStruct((N_SMALL, n_rows), F32), wide_shape]
    return pl.pallas_call(
        functools.partial(_inproj_kernel, tm=tm, seq_len=seq_len, has_pe=has_pe),
        grid=(n_rows // tm,), in_specs=in_specs, out_specs=out_specs, out_shape=out_shape,
        compiler_params=_params(1), name="inproj",
    )(*args)


def _dot3_left(a, b_exact):
    a1, a2, a3 = _split3(a)
    return _dot(a1, b_exact) + (_dot(a2, b_exact) + _dot(a3, b_exact))


def _chunk_setup(q_ref, k_ref, v_ref, gbr_ref, qg_ref, kg_ref, at_ref, gl_ref,
                 m_scr, rhs_scr, *, n_sub):
    c = CHUNK
    n_beta = N_DIR * H_A
    r1 = lax.broadcasted_iota(jnp.int32, (c, c), 0)
    c1 = lax.broadcasted_iota(jnp.int32, (c, c), 1)
    tri_lo = jnp.where(r1 >= c1, 1.0, 0.0).astype(BF16)
    tri_up = jnp.where(r1 <= c1, 1.0, 0.0).astype(BF16)
    r2 = lax.broadcasted_iota(jnp.int32, (2 * c, 2 * c), 0)
    c2 = lax.broadcasted_iota(jnp.int32, (2 * c, 2 * c), 1)
    same_head = (r2 ^ c2) < c

    def col_pair(a, j):
        col = jnp.concatenate([a[:, j:j + 1], a[:, j + 1:j + 2]], axis=0)
        return jnp.broadcast_to(col, (2 * c, c))

    def row_pair(a, j):
        return jnp.concatenate([a[j:j + 1, :], a[j + 1:j + 2, :]], axis=1)

    def twice(a):
        return jnp.concatenate([a, a], axis=1)

    n = 0
    for ci in range(n_sub):
        rows_ci = slice(ci * c, (ci + 1) * c)
        gbr = gbr_ref[:, rows_ci]
        sub = lax.broadcasted_iota(jnp.int32, gbr.shape, 0)
        gc_r = jnp.where(sub < n_beta + H_A, _dot3_left(gbr, tri_up), _dot3_left(gbr, tri_lo))
        total = jnp.broadcast_to(jnp.sum(gbr, axis=1, keepdims=True), gbr.shape)
        gl_ref[ci] = jnp.exp(total)[n_beta:, :]
        pad = jnp.zeros((c - 2 * N_SMALL, c), F32)
        cols_c = jnp.concatenate([gbr, gc_r, pad], axis=0).T
        gbc = cols_c[:, 0:N_SMALL]
        gc_c = cols_c[:, N_SMALL:2 * N_SMALL]

        q = q_ref[rows_ci, :]
        k = k_ref[rows_ci, :]
        v = v_ref[rows_ci, :]
        for hp in range(H_A // 2):
            h0 = 2 * hp

            def stack(a):
                return jnp.concatenate([a[:, h0 * DK:(h0 + 1) * DK],
                                        a[:, (h0 + 1) * DK:(h0 + 2) * DK]], axis=0)

            q2b, k2b, v2b = stack(q), stack(k), stack(v)
            kk = _dot_nt(k2b, k2b)
            qk = _dot_nt(q2b, k2b)
            for d in range(N_DIR):
                jb = d * H_A + h0
                jg = n_beta + jb
                gcb = col_pair(gc_c, jg)
                beta = col_pair(gbc, jb)
                tot = jnp.concatenate([jnp.broadcast_to(total[jg:jg + 1, :], (c, c)),
                                       jnp.broadcast_to(total[jg + 1:jg + 2, :], (c, c))], axis=0)
                eg = jnp.exp(gcb)
                er = jnp.exp(tot - gcb)
                tri = (r2 >= c2) if d == 0 else (r2 <= c2)
                decay = jnp.exp(jnp.where(same_head & tri, twice(gcb) - row_pair(gc_r, jg), NEG_BIG))
                attn = qk * decay
                m_scr[n] = jnp.where(r2 != c2, (twice(beta) * kk) * decay, 0.0)
                rhs_scr[n] = jnp.concatenate([v2b * beta.astype(BF16),
                                              k2b * (beta * eg).astype(BF16)], axis=1)
                qg = q2b * eg.astype(BF16)
                kg = k2b * er.astype(BF16)
                for hh in range(2):
                    rows = slice(hh * c, (hh + 1) * c)
                    cols = slice((h0 + hh) * DK, (h0 + hh + 1) * DK)
                    qg_ref[d, rows_ci, cols] = qg[rows]
                    kg_ref[d, rows_ci, cols] = kg[rows]
                    at_ref[d, rows_ci, cols] = attn[rows, hh * c:(hh + 1) * c].astype(BF16)
                n += 1


def _chunk_solve(m_scr, rhs_scr, gsel_ref, u_ref, w_ref, *, n_sub):
    c = CHUNK
    chains = [{"ci": ci, "h0": 2 * hp, "d": d}
              for ci in range(n_sub) for hp in range(H_A // 2) for d in range(N_DIR)]
    for n, ch in enumerate(chains):
        ch["m"] = m_scr[n]
    r2 = lax.broadcasted_iota(jnp.int32, (2 * c, 2 * c), 0)
    c2 = lax.broadcasted_iota(jnp.int32, (2 * c, 2 * c), 1)
    x2 = r2 ^ c2

    sub8 = SUBLANES
    per_half = c // sub8
    lane8 = lax.broadcasted_iota(jnp.int32, (sub8, c), 1)
    row8 = lax.broadcasted_iota(jnp.int32, (sub8, c), 0)
    in_blk = [(lane8 >> 3) == i for i in range(per_half)]
    zeros8 = jnp.zeros((sub8, c), F32)
    packed = []
    for ch in chains:
        halves = []
        for half in range(2):
            pm = zeros8
            for i in range(per_half):
                blk = half * per_half + i
                pm = jnp.where(in_blk[i], ch["m"][blk * sub8:(blk + 1) * sub8,
                                                  half * c:(half + 1) * c], pm)
            halves.append(pm)
        packed.append(jnp.concatenate(halves, axis=1))
    stacked = jnp.concatenate(packed, axis=0).astype(BF16)
    spread = [_dot(stacked, gsel_ref[k]) for k in range(sub8)]
    eye8 = jnp.where((lane8 & (sub8 - 1)) == row8, 1.0, 0.0).astype(F32)
    for n, ch in enumerate(chains):
        rows_n = slice(n * sub8, (n + 1) * sub8)
        tp = jnp.concatenate([eye8, eye8], axis=1)
        order = range(sub8 - 1) if ch["d"] == 0 else range(sub8 - 1, 0, -1)
        for k in order:
            tp = tp - spread[k][rows_n] * tp[k:k + 1, :]
        row_groups = []
        for half in range(2):
            tph = tp[:, half * c:(half + 1) * c]
            for i in range(per_half):
                own = jnp.where(in_blk[i], tph, 0.0)
                row_groups.append(jnp.concatenate([own, zeros8] if half == 0 else [zeros8, own],
                                                  axis=1))
        ch["t"] = jnp.concatenate(row_groups, axis=0)

    def pick(a, b, parity):
        return jnp.concatenate([a[i * b:(i + 1) * b] for i in range(a.shape[0] // b)
                                if i % 2 == parity], axis=0)

    def weave(sel, rest, b, parity):
        blocks = []
        for i in range(2 * sel.shape[0] // b):
            src = sel if i % 2 == parity else rest
            blocks.append(src[(i // 2) * b:(i // 2 + 1) * b])
        return jnp.concatenate(blocks, axis=0)

    par = [1 - ch["d"] for ch in chains]
    for lvl in range(3, 7):
        b = 1 << lvl
        xs = []
        for ch, p in zip(chains, par):
            off = jnp.where((pick(x2, b, p) >> lvl) == 1, pick(ch["m"], b, p), 0.0)
            xs.append(_dot(off.astype(BF16), ch["t"].astype(BF16)))
        for ch, x, p in zip(chains, xs, par):
            x_full = weave(x, jnp.zeros_like(x), b, p).astype(BF16)
            t_sel = pick(ch["t"], b, p)
            t_new = t_sel - _dot(t_sel.astype(BF16), x_full)
            ch["t"] = weave(t_new, pick(ch["t"], b, 1 - p), b, p)

    for n, ch in enumerate(chains):
        uw = _dot(ch["t"].astype(BF16), rhs_scr[n])
        rows_ci = slice(ch["ci"] * c, (ch["ci"] + 1) * c)
        for hh in range(2):
            rows = slice(hh * c, (hh + 1) * c)
            cols = slice((ch["h0"] + hh) * DK, (ch["h0"] + hh + 1) * DK)
            u_ref[ch["d"], rows_ci, cols] = uw[rows, 0:DV].astype(BF16)
            w_ref[ch["d"], rows_ci, cols] = uw[rows, DV:2 * DV].astype(BF16)


def _chunk_local_kernel(q_ref, k_ref, v_ref, gbr_ref, gsel_ref,
                        u_ref, w_ref, qg_ref, kg_ref, at_ref, gl_ref,
                        m_a, rhs_a, m_b, rhs_b, *, n_sub):
    setup = functools.partial(_chunk_setup, q_ref, k_ref, v_ref, gbr_ref,
                              qg_ref, kg_ref, at_ref, gl_ref, n_sub=n_sub)
    solve = functools.partial(_chunk_solve, gsel_ref=gsel_ref, u_ref=u_ref, w_ref=w_ref,
                              n_sub=n_sub)
    j = pl.program_id(0)

    @pl.when(j == 0)
    def _():
        u_ref[...] = jnp.zeros(u_ref.shape, u_ref.dtype)
        w_ref[...] = jnp.zeros(w_ref.shape, w_ref.dtype)
        setup(m_a, rhs_a)

    @pl.when(j % 2 == 1)
    def _():
        solve(m_a, rhs_a)
        setup(m_b, rhs_b)

    @pl.when((j > 0) & (j % 2 == 0))
    def _():
        solve(m_b, rhs_b)
        setup(m_a, rhs_a)


def _chunk_local(q, k, v, gbr, n_sub=4):
    n_rows = q.shape[0]
    n_chunks = n_rows // CHUNK
    rows = n_sub * CHUNK
    n_blocks = n_chunks // n_sub
    n_sys = n_sub * (H_A // 2) * N_DIR
    cur = lambda j: jnp.minimum(j, n_blocks - 1)
    prev = lambda j: jnp.maximum(j - 1, 0)
    wide = pl.BlockSpec((rows, A_WIDTH), lambda j: (cur(j), 0))
    out_cur = pl.BlockSpec((N_DIR, rows, A_WIDTH), lambda j: (0, cur(j), 0))
    out_prev = pl.BlockSpec((N_DIR, rows, A_WIDTH), lambda j: (0, prev(j), 0))
    big = lambda dt: jax.ShapeDtypeStruct((N_DIR, n_rows, A_WIDTH), dt)
    idx = jnp.arange(2 * CHUNK)
    gsel = ((idx[None, :, None] // SUBLANES == idx[None, None, :] // SUBLANES)
            & (idx[None, :, None] % SUBLANES == jnp.arange(SUBLANES)[:, None, None])).astype(BF16)
    return pl.pallas_call(
        functools.partial(_chunk_local_kernel, n_sub=n_sub),
        grid=(n_blocks + 1,),
        in_specs=[wide, wide, wide,
                  pl.BlockSpec((N_SMALL, rows), lambda j: (0, cur(j))),
                  _const_spec((SUBLANES, 2 * CHUNK, 2 * CHUNK))],
        out_specs=[out_prev, out_prev, out_cur, out_cur, out_cur,
                   pl.BlockSpec((n_sub, N_DIR * H_A, CHUNK), lambda j: (cur(j), 0, 0))],
        out_shape=[big(BF16)] * 5 + [jax.ShapeDtypeStruct((n_chunks, N_DIR * H_A, CHUNK), F32)],
        scratch_shapes=[pltpu.VMEM((n_sys, 2 * CHUNK, 2 * CHUNK), F32),
                        pltpu.VMEM((n_sys, 2 * CHUNK, 2 * CHUNK), BF16),
                        pltpu.VMEM((n_sys, 2 * CHUNK, 2 * CHUNK), F32),
                        pltpu.VMEM((n_sys, 2 * CHUNK, 2 * CHUNK), BF16)],
        compiler_params=_params(1, ("arbitrary",)), name="chunk_local",
    )(q, k, v, gbr, gsel)


def _scan_kernel(*refs, n_chunks, zero_init, write_final):
    fwd = refs[0:6]
    bwd = refs[6:12]
    refs = refs[12:]
    if not zero_init:
        s0_ref = refs[0]
        refs = refs[1:]
    of_ref, ob_ref = refs[0:2]
    refs = refs[2:]
    if write_final:
        sfin_ref = refs[0]
        refs = refs[1:]
    s_scr = refs[0]
    i = pl.program_id(1)

    @pl.when(i == 0)
    def _():
        if zero_init:
            s_scr[...] = jnp.zeros(s_scr.shape, F32)
        else:
            s_scr[...] = s0_ref[...]

    problems = [(s, d, h) for s in range(SEQ_GROUP) for d in range(N_DIR) for h in range(H_A)]
    c = CHUNK
    stage1 = []
    for s, d, h in problems:
        u_ref, w_ref, qg_ref, _, _, _ = fwd if d == 0 else bwd
        cols = slice(h * DV, (h + 1) * DV)
        sb = s_scr[s, d, h].astype(BF16)
        lhs = jnp.concatenate([w_ref[0, s, :, cols], qg_ref[0, s, :, cols]], axis=0)
        stage1.append(_dot(lhs, sb))
    for (s, d, h), ws_qs in zip(problems, stage1):
        u_ref, _, _, kg_ref, at_ref, gl_ref = fwd if d == 0 else bwd
        o_ref = of_ref if d == 0 else ob_ref
        cols = slice(h * DV, (h + 1) * DV)
        vb = (u_ref[0, s, :, cols].astype(F32) - ws_qs[0:c]).astype(BF16)
        o_ref[s, :, cols] = (ws_qs[c:2 * c] + _dot(at_ref[0, s, :, cols], vb)).astype(BF16)
        gl = gl_ref[s, 0, d * H_A + h:d * H_A + h + 1, :]
        s_scr[s, d, h] = s_scr[s, d, h] * gl + _dot_tn(kg_ref[0, s, :, cols], vb)

    if write_final:
        @pl.when(i == n_chunks - 1)
        def _():
            sfin_ref[...] = s_scr[...]


def _scan(local, s0, n_seq, seq_len, write_final):
    u, w, qg, kg, at, gl = local
    n_chunks = seq_len // CHUNK
    zero_init = s0 is None
    shp4 = (N_DIR, n_seq, seq_len, A_WIDTH)
    arrs = [a.reshape(shp4) for a in (u, w, qg, kg, at)]
    gl4 = gl.reshape(n_seq, n_chunks, N_DIR * H_A, CHUNK)
    sg = SEQ_GROUP

    def specs(d, chunk_of):
        big = pl.BlockSpec((1, sg, CHUNK, A_WIDTH), lambda g, i: (d, g, chunk_of(i), 0))
        return [big] * 5 + [pl.BlockSpec((sg, 1, N_DIR * H_A, CHUNK),
                                         lambda g, i: (g, chunk_of(i), 0, 0))]

    rev = lambda i: n_chunks - 1 - i
    in_specs = specs(0, lambda i: i) + specs(1, rev)
    args = arrs + [gl4] + arrs + [gl4]
    state_spec = pl.BlockSpec((sg, N_DIR, H_A, DK, DV), lambda g, i: (g, 0, 0, 0, 0))
    if not zero_init:
        in_specs.append(state_spec)
        args.append(s0)
    o_shape = jax.ShapeDtypeStruct((n_seq, seq_len, A_WIDTH), BF16)
    out_specs = [pl.BlockSpec((sg, CHUNK, A_WIDTH), lambda g, i: (g, i, 0)),
                 pl.BlockSpec((sg, CHUNK, A_WIDTH), lambda g, i: (g, rev(i), 0))]
    out_shape = [o_shape, o_shape]
    if write_final:
        out_specs.append(state_spec)
        out_shape.append(jax.ShapeDtypeStruct((n_seq, N_DIR, H_A, DK, DV), F32))
    return pl.pallas_call(
        functools.partial(_scan_kernel, n_chunks=n_chunks, zero_init=zero_init,
                          write_final=write_final),
        grid=(n_seq // sg, n_chunks), in_specs=in_specs, out_specs=out_specs, out_shape=out_shape,
        scratch_shapes=[pltpu.VMEM((sg, N_DIR, H_A, DK, DV), F32)],
        compiler_params=_params(2, ("parallel", "arbitrary")), name="delta_scan",
    )(*args)


def _delta_mixer_rows(o_f, o_b, gate, yb, og):
    o = o_f + o_b
    sg = _silu(gate)
    parts = []
    for h in range(H_A):
        cols = slice(h * DV, (h + 1) * DV)
        oh = o[:, cols]
        ms = jnp.mean(oh * oh, axis=-1, keepdims=True)
        parts.append((oh * lax.rsqrt(ms + EPS)) * og * sg[:, cols])
    return jnp.concatenate(parts + [yb], axis=1)


def _ffn_kernel(*refs, tm, seq_len, final_norm, delta, has_pe):
    xp, xm, xn = refs[:3]
    refs = refs[3:]
    if delta:
        if has_pe:
            pe_refs = refs[:3]
            refs = refs[3:]
        mix_refs = refs[:12]
        og_ref, wout_ref = refs[12:14]
        refs = refs[14:]
    mod_ref, ng_ref, wg_ref, wu_ref, wd_ref, cf_ref = refs[:6]
    refs = refs[6:]
    if final_norm:
        fg_ref = refs[0]
        refs = refs[1:]
    out_ref = refs[0]
    vp, vn = _halo_valid(tm, seq_len)
    mod = mod_ref[0]
    shift = mod[:, 3 * D_MODEL:4 * D_MODEL]
    scale = mod[:, 4 * D_MODEL:5 * D_MODEL]
    gate2 = mod[:, 5 * D_MODEL:6 * D_MODEL]
    g = ng_ref[...]
    if delta:
        def mixer_rows(which):
            vals = []
            for a in range(4):
                r = mix_refs[3 * a + which][...].astype(F32)
                if which == 0:
                    r = r[HALO:2 * HALO]
                elif which == 2:
                    r = r[0:HALO]
                vals.append(r)
            return _delta_mixer_rows(*vals, og_ref[...])

        a_ext = jnp.concatenate([mixer_rows(0), mixer_rows(1), mixer_rows(2)], axis=0)
        x_ext = jnp.concatenate([xp[...], xm[...], xn[...]], axis=0)
        if has_pe:
            x_ext = x_ext + jnp.concatenate([r[...] for r in pe_refs], axis=0)
        gate1 = mod[:, 2 * D_MODEL:3 * D_MODEL]
        x_ext = x_ext + gate1 * _dot(a_ext.astype(BF16), wout_ref[...])
        x = x_ext[HALO:HALO + tm]
        h_all = _norm_mod(x_ext, g, scale, shift)
        h_main = h_all[HALO:HALO + tm]
        h_ext = jnp.concatenate([h_all[0:HALO] * vp, h_main, h_all[HALO + tm:] * vn], axis=0)
    else:
        x = xm[...]
        h_main = _norm_mod(x, g, scale, shift)
        h_ext = jnp.concatenate([_norm_mod(xp[...], g, scale, shift) * vp, h_main,
                                 _norm_mod(xn[...], g, scale, shift) * vn], axis=0)
    hb_ext = h_ext.astype(BF16)
    hb = h_main.astype(BF16)
    edge_masks = _seq_edge_masks(tm, seq_len, FF_CHUNK)
    acts = []
    for j in range(N_FF_CHUNKS):
        cols = slice(j * FF_CHUNK, (j + 1) * FF_CHUNK)
        u = _dwconv3_ext(_dot(hb_ext, wg_ref[:, cols]), cf_ref[:, cols], tm, edge_masks)
        acts.append((_silu(u) * _dot(hb, wu_ref[:, cols])).astype(BF16))
    y = x + gate2 * _dot(jnp.concatenate(acts, axis=1), wd_ref[...])
    if final_norm:
        ms = jnp.mean(y * y, axis=-1, keepdims=True)
        y = (y * lax.rsqrt(ms + EPS)) * fg_ref[...]
    out_ref[...] = y


def _ffn(x, mod3, row_base, row_stride, seq_len, tm, ng, wts, layer, final_g, delta=None):
    n_rows = x.shape[0]
    final_norm = final_g is not None
    of_layer = lambda *shape: pl.BlockSpec((None,) + shape, lambda i: (layer, 0, 0))
    in_specs = _tile_specs(tm, D_MODEL, n_rows)
    args = [x, x, x]
    has_pe = False
    if delta is not None:
        pe, o_f, o_b, gate, yb, og, w_out = delta
        has_pe = pe is not None
        if has_pe:
            in_specs += _pe_specs(tm, D_MODEL, seq_len)
            args += [pe, pe, pe]
        for arr in (o_f, o_b, gate, yb):
            in_specs += _tile_specs(tm, A_WIDTH, n_rows, halo=2 * HALO)
            args += [arr, arr, arr]
        in_specs += [_const_spec((1, DV)), _const_spec((D_MODEL, D_MODEL))]
        args += [og, w_out]
    in_specs += [_mod_spec(tm, seq_len, row_base, row_stride), _const_spec((1, D_MODEL)),
                 of_layer(D_MODEL, D_FF), of_layer(D_MODEL, D_FF), of_layer(D_FF, D_MODEL),
                 of_layer(3, D_FF)]
    args += [mod3, ng, wts["w_gate"], wts["w_up"], wts["w_down"], wts["conv"]]
    if final_norm:
        in_specs.append(_const_spec((1, D_MODEL)))
        args.append(final_g)
    return pl.pallas_call(
        functools.partial(_ffn_kernel, tm=tm, seq_len=seq_len, final_norm=final_norm,
                          delta=delta is not None, has_pe=has_pe),
        grid=(n_rows // tm,), in_specs=in_specs,
        out_specs=pl.BlockSpec((tm, D_MODEL), lambda i: (i, 0)),
        out_shape=jax.ShapeDtypeStruct((n_rows, D_MODEL), F32),
        compiler_params=_params(1), name="conv_ffn",
    )(*args)


def _pool_group(h_ext, n_out, pos, seq_len, wp_ref, gi):
    win = POOL_WINDOWS[gi]
    cols = slice(gi * POOL_GROUP, (gi + 1) * POOL_GROUP)
    s = h_ext[:, cols]
    width = 1
    while width < win:
        s = s + _shift_rows(s, width)
        width *= 2
    start = HALO - win // 2
    if start:
        s = _shift_rows(s, start)
    s = s[0:n_out]
    lo = jnp.maximum(pos - win // 2, 0)
    hi = jnp.minimum(pos - win // 2 + win, seq_len)
    count = jnp.maximum(hi - lo, 1).astype(F32)
    mixed = s / jnp.concatenate([count, count], axis=1) - h_ext[HALO:HALO + n_out, cols]
    return _dot(mixed.astype(BF16), wp_ref[gi])


def _pool_stage(xp, xm, xn, mod_ref, g1_ref, wp_ref, ps_ref, g2_ref, hb_scr, xr_scr,
                *, tm, seq_len, n_tiles):
    wide = 2 * HALO
    tile = jnp.minimum(pl.program_id(0), n_tiles - 1)
    first = tile * tm
    vp = jnp.where(first % seq_len != 0, 1.0, 0.0).astype(F32)
    vn = jnp.where((first + tm) % seq_len != 0, 1.0, 0.0).astype(F32)
    mod = mod_ref[0]
    shift1, scale1, gate1 = (mod[:, k * D_MODEL:(k + 1) * D_MODEL] for k in range(3))
    shift2, scale2 = (mod[:, k * D_MODEL:(k + 1) * D_MODEL] for k in (3, 4))
    g1 = g1_ref[...]
    g2 = g2_ref[...]
    lanes = POOL_GROUP // 2
    if tm <= seq_len:
        n_out = tm + 2 * HALO
        segments = [(jnp.concatenate([_norm_mod(xp[...], g1, scale1, shift1) * vp,
                                      _norm_mod(xm[...], g1, scale1, shift1),
                                      _norm_mod(xn[...], g1, scale1, shift1) * vn], axis=0),
                     (first - HALO + lax.broadcasted_iota(jnp.int32, (n_out, lanes), 0)) % seq_len)]
        x_res = jnp.concatenate([xp[HALO:, :], xm[...], xn[0:HALO, :]], axis=0)
    else:
        n_out = seq_len
        x_res = xm[...]
        h1 = _norm_mod(x_res, g1, scale1, shift1)
        zeros = jnp.zeros((HALO, D_MODEL), F32)
        pos = lax.broadcasted_iota(jnp.int32, (seq_len, lanes), 0)
        segments = [(jnp.concatenate([zeros, h1[s * seq_len:(s + 1) * seq_len], zeros], axis=0), pos)
                    for s in range(tm // seq_len)]
    yield
    ys = []
    for h_ext, pos in segments:
        groups = []
        for gi in range(N_POOL_GROUPS):
            groups.append(_pool_group(h_ext, n_out, pos, seq_len, wp_ref, gi))
            yield
        ys.append(jnp.concatenate(groups, axis=1))
    x3 = x_res + gate1 * (jnp.concatenate(ys, axis=0) * ps_ref[...])
    h2 = _norm_mod(x3, g2, scale2, shift2)
    if tm <= seq_len:
        zeros = jnp.zeros((HALO, D_MODEL), F32)
        hb_scr[...] = jnp.concatenate([zeros, h2[0:HALO] * vp, h2[HALO:HALO + tm],
                                       h2[HALO + tm:] * vn, zeros], axis=0).astype(BF16)
        xr_scr[...] = x3[HALO:HALO + tm]
    else:
        zeros_w = jnp.zeros((wide, D_MODEL), F32)
        hb_scr[...] = jnp.concatenate([zeros_w, h2, zeros_w], axis=0).astype(BF16)
        xr_scr[...] = x3
    yield


def _ffn_stage(hb_scr, xr_scr, mod_ref, wg_ref, wu_ref, wd_ref, cf_ref, fg_ref, out_ref,
               *, tm, seq_len):
    wide = 2 * HALO
    gate2 = mod_ref[0][:, 5 * D_MODEL:6 * D_MODEL]
    hb_ext = hb_scr[...]
    hb = hb_scr[wide:wide + tm, :]
    edge_masks = _seq_edge_masks(tm, seq_len, FF_CHUNK)
    acts = []
    for j in range(N_FF_CHUNKS):
        cols = slice(j * FF_CHUNK, (j + 1) * FF_CHUNK)
        u = _dwconv3_ext(_dot(hb_ext, wg_ref[:, cols]), cf_ref[:, cols], tm, edge_masks, wide)
        acts.append((_silu(u) * _dot(hb, wu_ref[:, cols])).astype(BF16))
        yield
    y = xr_scr[...] + gate2 * _dot(jnp.concatenate(acts, axis=1), wd_ref[...])
    ms = jnp.mean(y * y, axis=-1, keepdims=True)
    out_ref[...] = (y * lax.rsqrt(ms + EPS)) * fg_ref[...]
    yield


def _interleave(*stages):
    live = list(stages)
    while live:
        for stage in list(live):
            if next(stage, StopIteration) is StopIteration:
                live.remove(stage)


def _pool_ffn_kernel(xp, xm, xn, mod_cur, mod_prev, g1_ref, wp_ref, ps_ref, g2_ref,
                     wg_ref, wu_ref, wd_ref, cf_ref, fg_ref, out_ref,
                     hb_a, xr_a, hb_b, xr_b, *, tm, seq_len, n_tiles):
    stage = functools.partial(_pool_stage, xp, xm, xn, mod_cur, g1_ref, wp_ref, ps_ref, g2_ref,
                              tm=tm, seq_len=seq_len, n_tiles=n_tiles)
    finish = functools.partial(_ffn_stage, mod_ref=mod_prev, wg_ref=wg_ref, wu_ref=wu_ref,
                               wd_ref=wd_ref, cf_ref=cf_ref, fg_ref=fg_ref, out_ref=out_ref,
                               tm=tm, seq_len=seq_len)
    j = pl.program_id(0)

    @pl.when(j == 0)
    def _():
        out_ref[...] = jnp.zeros(out_ref.shape, out_ref.dtype)
        _interleave(stage(hb_a, xr_a))

    @pl.when(j % 2 == 1)
    def _():
        _interleave(finish(hb_a, xr_a), stage(hb_b, xr_b))

    @pl.when((j > 0) & (j % 2 == 0))
    def _():
        _interleave(finish(hb_b, xr_b), stage(hb_a, xr_a))


def _pool_ffn(x, mod3, row_base, row_stride, seq_len, tm, g1, w_pool, pool_scale, g2, wts, layer,
              final_g):
    n_rows = x.shape[0]
    n_tiles = n_rows // tm
    wide = 2 * HALO
    per = tm // wide
    last = n_rows // wide - 1
    cur = lambda j: jnp.minimum(j, n_tiles - 1)
    prev = lambda j: jnp.maximum(j - 1, 0)
    mod_row = lambda t: row_base + ((t * tm) // seq_len) * row_stride
    of_layer = lambda *shape: pl.BlockSpec((None,) + shape, lambda j: (layer, 0, 0))
    in_specs = [
        pl.BlockSpec((wide, D_MODEL), lambda j: (jnp.maximum(cur(j) * per - 1, 0), 0)),
        pl.BlockSpec((tm, D_MODEL), lambda j: (cur(j), 0)),
        pl.BlockSpec((wide, D_MODEL), lambda j: (jnp.minimum((cur(j) + 1) * per, last), 0)),
        pl.BlockSpec((1, 1, N_MOD * D_MODEL), lambda j: (mod_row(cur(j)), 0, 0)),
        pl.BlockSpec((1, 1, N_MOD * D_MODEL), lambda j: (mod_row(prev(j)), 0, 0)),
        _const_spec((1, D_MODEL)), _const_spec((N_POOL_GROUPS, POOL_GROUP, POOL_GROUP)),
        _const_spec((1, D_MODEL)), _const_spec((1, D_MODEL)),
        of_layer(D_MODEL, D_FF), of_layer(D_MODEL, D_FF), of_layer(D_FF, D_MODEL),
        of_layer(3, D_FF), _const_spec((1, D_MODEL))]
    return pl.pallas_call(
        functools.partial(_pool_ffn_kernel, tm=tm, seq_len=seq_len, n_tiles=n_tiles),
        grid=(n_tiles + 1,), in_specs=in_specs,
        out_specs=pl.BlockSpec((tm, D_MODEL), lambda j: (prev(j), 0)),
        out_shape=jax.ShapeDtypeStruct((n_rows, D_MODEL), F32),
        scratch_shapes=[pltpu.VMEM((tm + 2 * wide, D_MODEL), BF16), pltpu.VMEM((tm, D_MODEL), F32),
                        pltpu.VMEM((tm + 2 * wide, D_MODEL), BF16), pltpu.VMEM((tm, D_MODEL), F32)],
        compiler_params=_params(1, ("arbitrary",)), name="pool_ffn",
    )(x, x, x, mod3, mod3, g1, w_pool, pool_scale, g2, wts["w_gate"], wts["w_up"], wts["w_down"],
      wts["conv"], final_g)


def _sincos_2d(rows, cols, d):
    quarter = d // 4
    omega = 1.0 / (10000.0 ** (jnp.arange(quarter, dtype=F32) / quarter))
    er = jnp.arange(rows, dtype=F32)[:, None] * omega[None, :]
    ec = jnp.arange(cols, dtype=F32)[:, None] * omega[None, :]
    er = jnp.concatenate([jnp.sin(er), jnp.cos(er)], axis=-1)
    ec = jnp.concatenate([jnp.sin(ec), jnp.cos(ec)], axis=-1)
    pe = jnp.concatenate([jnp.broadcast_to(er[:, None, :], (rows, cols, d // 2)),
                          jnp.broadcast_to(ec[None, :, :], (rows, cols, d // 2))], axis=-1)
    return pe.reshape(rows * cols, d)


def _even_layer_weights(w_in, conv_qkv, a_log, dt_bias, conv_b, w_out):
    small = w_in[:, GATE_END:ALPHA_END]
    zeros = jnp.zeros((N_DIR * H_A,), F32)
    ea = jnp.concatenate([zeros, jnp.exp(a_log.astype(F32)).reshape(-1)])
    dtb = jnp.concatenate([zeros, dt_bias.astype(F32).reshape(-1)])
    return {
        "w_main": jnp.concatenate([w_in[:, :GATE_END], w_in[:, ALPHA_END:IN_AB]],
                                  axis=1).astype(BF16),
        "w_small_r": small.T.astype(BF16),
        "conv_qkv": conv_qkv, "conv_b": conv_b,
        "p_row": jnp.stack([ea, dtb], axis=1),
        "w_out": w_out.astype(BF16),
    }


def _ffn_weights(w_gate, w_up, conv, w_down):
    return {"w_gate": w_gate.astype(BF16), "w_up": w_up.astype(BF16),
            "w_down": w_down.astype(BF16), "conv": conv}


def _trunk(x, pe, mod, row_base, row_stride, n_seq, seq_len, tm, tm_ffn, s0, write_final, p):
    row = lambda a: a.reshape(1, -1)
    finals = None
    depth = mod.shape[0]
    for layer in range(depth):
        mod3 = mod[layer][:, None, :]
        place = (mod3, row_base, row_stride, seq_len, tm)
        place_ffn = (mod3, row_base, row_stride, seq_len, tm_ffn // 2)
        if layer % 2 == 0:
            e = layer // 2
            wts = p["even"][e]
            q, k, v, gate, gbr, yb = _inproj(x, pe, *place, row(p["norm_mix_g"][layer]), wts)
            local = _chunk_local(q, k, v, gbr)
            res = _scan(local, None if s0 is None else s0[:, e], n_seq, seq_len, write_final)
            o_f = res[0].reshape(-1, A_WIDTH)
            o_b = res[1].reshape(-1, A_WIDTH)
            if write_final:
                finals = res[2]
            delta = (pe, o_f, o_b, gate, yb, row(p["o_norm_g"][e]), wts["w_out"])
            x = _ffn(x, *place_ffn, row(p["norm_ffn_g"][layer]), p["ffn"], layer, None, delta)
        else:
            o = layer // 2
            assert layer == depth - 1
            x = _pool_ffn(x, *place_ffn, row(p["norm_mix_g"][layer]), p["w_pool"][o],
                          row(p["pool_scale"][o]), row(p["norm_ffn_g"][layer]), p["ffn"], layer,
                          row(p["final_norm_g"]))
    return x, finals


def kernel(x_prompt, x_sample, state_delta, c, c_ctx, norm_mix_g, norm_ffn_g, w_ada, b_ada,
           w_in_ab, conv_qkv, a_log, dt_bias, o_norm_g, conv_b, w_out_ab, w_pool, pool_scale,
           w_ffn_gate, w_ffn_up, ffn_conv, w_ffn_down, final_norm_g):
    batch, seq, d = x_prompt.shape
    dec_batch, dec_seq, _ = x_sample.shape
    depth = w_ada.shape[0]
    n_even = w_in_ab.shape[0]
    assert d == D_MODEL and depth == 2 and n_even == 1

    p = {
        "norm_mix_g": norm_mix_g, "norm_ffn_g": norm_ffn_g, "o_norm_g": o_norm_g,
        "pool_scale": pool_scale, "final_norm_g": final_norm_g,
        "w_pool": w_pool.astype(BF16),
        "even": [_even_layer_weights(w_in_ab[e], conv_qkv[e], a_log[e], dt_bias[e], conv_b[e],
                                     w_out_ab[e]) for e in range(n_even)],
        "ffn": _ffn_weights(w_ffn_gate, w_ffn_up, ffn_conv, w_ffn_down),
    }

    n_cond = 1 + dec_batch
    pad = (-n_cond) % HALO
    cond = jnp.concatenate([c_ctx[None, :], c, jnp.zeros((pad, d), F32)], axis=0)
    mod = _modulation(cond, w_ada, b_ada)

    y_prompt, finals = _trunk(x_prompt.reshape(batch * seq, d), None, mod, 0, 0, batch, seq,
                              min(seq, 256), 1024, None, True, p)
    pe = _sincos_2d(dec_seq // GRID_W, GRID_W, d)
    y_sample, _ = _trunk(x_sample.reshape(dec_batch * dec_seq, d), pe, mod, 1, 1, dec_batch,
                         dec_seq, 512, 1024, state_delta, False, p)
    return (y_prompt.reshape(batch, seq, d), y_sample.reshape(dec_batch, dec_seq, d),
            finals[:, None])
```

```python
import functools

import jax
import jax.numpy as jnp
from jax import lax
from jax.experimental import pallas as pl
from jax.experimental.pallas import tpu as pltpu

F32 = jnp.float32
BF16 = jnp.bfloat16

D_MODEL = 1024
GRID_W = 64
A_WIDTH = 512
B_WIDTH = 512
DK = 128
DV = 128
H_A = 4
N_DIR = 2
N_POOL_GROUPS = 4
POOL_GROUP = D_MODEL // N_POOL_GROUPS
POOL_WINDOWS = (2, 4, 8, 16)
D_FF = 2816
N_MOD = 6
EPS = 1e-6
QKV_END = 3 * A_WIDTH
GATE_END = QKV_END + A_WIDTH
BETA_END = GATE_END + N_DIR * H_A
ALPHA_END = BETA_END + N_DIR * H_A
BG_END = ALPHA_END + B_WIDTH
CG_END = BG_END + B_WIDTH
IN_AB = CG_END + B_WIDTH

SUBLANES = 8
HALO = SUBLANES
CHUNK = 128
FF_CHUNK = 256
N_FF_CHUNKS = D_FF // FF_CHUNK
N_SMALL = 2 * N_DIR * H_A
SEQ_GROUP = 4
VMEM_LIMIT = 56 * 1024 * 1024
NEG_BIG = -1e30


def _sigmoid(x):
    return 1.0 / (1.0 + jnp.exp(-x))


def _silu(x):
    return x * _sigmoid(x)


def _softplus(x):
    return jnp.maximum(x, 0.0) + jnp.log1p(jnp.exp(-jnp.abs(x)))


def _dot(a, b):
    return jnp.dot(a, b, preferred_element_type=F32)


def _dot_nt(a, b):
    return lax.dot_general(a, b, (((1,), (1,)), ((), ())), preferred_element_type=F32)


def _dot_tn(a, b):
    return lax.dot_general(a, b, (((0,), (0,)), ((), ())), preferred_element_type=F32)


def _split3(a):
    a1 = a.astype(BF16)
    r1 = a - a1.astype(F32)
    a2 = r1.astype(BF16)
    a3 = (r1 - a2.astype(F32)).astype(BF16)
    return a1, a2, a3


def _norm_mod(x, g, scale, shift):
    ms = jnp.mean(x * x, axis=-1, keepdims=True)
    return (x * lax.rsqrt(ms + EPS)) * (g * (1.0 + scale)) + shift


def _shift_rows(a, k):
    n = a.shape[0]
    return pltpu.roll(a, (-k) % n, axis=0)


def _dwconv3_ext(p_ext, w, tm, edge_masks=None, halo=HALO):
    prev = _shift_rows(p_ext, -1)[halo:halo + tm]
    cur = p_ext[halo:halo + tm]
    nxt = _shift_rows(p_ext, 1)[halo:halo + tm]
    if edge_masks is not None:
        prev = prev * edge_masks[0]
        nxt = nxt * edge_masks[1]
    return prev * w[0:1] + cur * w[1:2] + nxt * w[2:3]


def _seq_edge_masks(tm, seq_len, width):
    if tm <= seq_len:
        return None
    pos = lax.broadcasted_iota(jnp.int32, (tm, width), 0) % seq_len
    return (jnp.where(pos != 0, 1.0, 0.0).astype(F32),
            jnp.where(pos != seq_len - 1, 1.0, 0.0).astype(F32))


def _halo_valid(tm, seq_len):
    i = pl.program_id(0)
    vp = jnp.where((i * tm) % seq_len != 0, 1.0, 0.0).astype(F32)
    vn = jnp.where(((i + 1) * tm) % seq_len != 0, 1.0, 0.0).astype(F32)
    return vp, vn


def _const_spec(shape):
    nd = len(shape)
    return pl.BlockSpec(shape, lambda *_: (0,) * nd)


def _tile_specs(tm, width, n_rows, halo=HALO):
    per = tm // halo
    last = n_rows // halo - 1
    prev = pl.BlockSpec((halo, width), lambda i: (jnp.maximum(i * per - 1, 0), 0))
    main = pl.BlockSpec((tm, width), lambda i: (i, 0))
    nxt = pl.BlockSpec((halo, width), lambda i: (jnp.minimum((i + 1) * per, last), 0))
    return [prev, main, nxt]


def _pe_specs(tm, width, seq_len):
    per = tm // HALO
    tiles = seq_len // tm
    last = seq_len // HALO - 1
    prev = pl.BlockSpec((HALO, width), lambda i: (jnp.maximum((i % tiles) * per - 1, 0), 0))
    main = pl.BlockSpec((tm, width), lambda i: (i % tiles, 0))
    nxt = pl.BlockSpec((HALO, width), lambda i: (jnp.minimum((i % tiles + 1) * per, last), 0))
    return [prev, main, nxt]


def _mod_spec(tm, seq_len, row_base, row_stride):
    return pl.BlockSpec((1, 1, N_MOD * D_MODEL),
                        lambda i: (row_base + ((i * tm) // seq_len) * row_stride, 0, 0))


def _params(n_axes=1, semantics=None):
    return pltpu.CompilerParams(
        dimension_semantics=semantics or ("parallel",) * n_axes,
        vmem_limit_bytes=VMEM_LIMIT)


def _interleave(*stages):
    live = list(stages)
    while live:
        for stage in list(live):
            if next(stage, StopIteration) is StopIteration:
                live.remove(stage)


def _mod_kernel(c_ref, w_ref, b_ref, o_ref):
    c = c_ref[...]
    s1, s2, _ = _split3(_silu(c))
    w = w_ref[0]
    w1 = w.astype(BF16)
    w2 = (w - w1.astype(F32)).astype(BF16)
    o_ref[0] = _dot(s1, w1) + (_dot(s2, w1) + _dot(s1, w2)) + b_ref[0]


def _modulation(cond, w_ada, b_ada):
    depth, _, width = w_ada.shape
    rows = cond.shape[0]
    tn = 1536
    return pl.pallas_call(
        _mod_kernel,
        grid=(depth, width // tn),
        in_specs=[pl.BlockSpec((rows, D_MODEL), lambda l, j: (0, 0)),
                  pl.BlockSpec((1, D_MODEL, tn), lambda l, j: (l, 0, j)),
                  pl.BlockSpec((1, 1, tn), lambda l, j: (l, 0, j))],
        out_specs=pl.BlockSpec((1, rows, tn), lambda l, j: (l, 0, j)),
        out_shape=jax.ShapeDtypeStruct((depth, rows, width), F32),
        compiler_params=_params(2),
        name="modulation",
    )(cond, w_ada, b_ada.reshape(depth, 1, width))


def _inproj_kernel(*refs, tm, seq_len, has_pe):
    if has_pe:
        xp, xm, xn, pp, pm, pn = refs[:6]
        refs = refs[6:]
    else:
        xp, xm, xn = refs[:3]
        pp = pm = pn = None
        refs = refs[3:]
    (mod_ref, ng_ref, w_ref, wsc_ref, cq_ref, cb_ref, pr_ref,
     q_ref, k_ref, v_ref, gate_ref, gbr_ref, yb_ref) = refs

    vp, vn = _halo_valid(tm, seq_len)
    mod = mod_ref[0]
    shift = mod[:, 0:D_MODEL]
    scale = mod[:, D_MODEL:2 * D_MODEL]
    g = ng_ref[...]

    def prep(x_ref, pe_ref):
        x = x_ref[...]
        if has_pe:
            x = x + pe_ref[...]
        return _norm_mod(x, g, scale, shift)

    h_main = prep(xm, pm)
    h_ext = jnp.concatenate([prep(xp, pp) * vp, h_main, prep(xn, pn) * vn], axis=0)
    hb_ext = h_ext.astype(BF16)
    hb = h_main.astype(BF16)

    slab = FF_CHUNK
    per_group = A_WIDTH // slab

    def project(lhs, group, half):
        lo = group * A_WIDTH + half * slab
        return _dot(lhs, w_ref[:, lo:lo + slab])

    cq = cq_ref[...]
    outs = (q_ref, k_ref, v_ref)
    short = []
    for part in range(3):
        for half in range(per_group):
            cols = slice(part * A_WIDTH + half * slab, part * A_WIDTH + (half + 1) * slab)
            p = project(hb_ext, part, half)
            short.append(project(hb_ext, 4 + part, half))
            a = _silu(_dwconv3_ext(p, cq[:, cols], tm))
            if part < 2:
                heads = []
                for h in range(slab // DK):
                    ah = a[:, h * DK:(h + 1) * DK]
                    ss = jnp.sum(ah * ah, axis=-1, keepdims=True)
                    nrm = lax.rsqrt(ss + EPS)
                    if part == 0:
                        nrm = nrm * (DK ** -0.5)
                    heads.append(ah * nrm)
                a = jnp.concatenate(heads, axis=1)
            outs[part][:, half * slab:(half + 1) * slab] = a.astype(BF16)

    cb = cb_ref[...]
    for half in range(per_group):
        cols = slice(half * slab, (half + 1) * slab)
        gate_ref[:, cols] = project(hb, 3, half).astype(BF16)
        bg, cg, hx = (short[part * per_group + half] for part in range(3))
        yb_ref[:, cols] = (bg[HALO:HALO + tm]
                           * _dwconv3_ext(cg * hx, cb[:, cols], tm)).astype(BF16)

    pr = pr_ref[...]
    w_small = wsc_ref[...].T[0:N_SMALL].astype(BF16)
    bar = _dot_nt(w_small, hb)
    sub = lax.broadcasted_iota(jnp.int32, bar.shape, 0)
    gbr_ref[...] = jnp.where(sub < N_DIR * H_A, _sigmoid(bar),
                             -pr[:, 0:1] * _softplus(bar + pr[:, 1:2]))


def _inproj(x, pe, mod3, row_base, row_stride, seq_len, tm, ng, wts):
    n_rows = x.shape[0]
    has_pe = pe is not None
    in_specs = _tile_specs(tm, D_MODEL, n_rows)
    args = [x, x, x]
    if has_pe:
        in_specs += _pe_specs(tm, D_MODEL, seq_len)
        args += [pe, pe, pe]
    in_specs += [_mod_spec(tm, seq_len, row_base, row_stride), _const_spec((1, D_MODEL)),
                 _const_spec((D_MODEL, GATE_END + 3 * B_WIDTH)), _const_spec((D_MODEL, CHUNK)),
                 _const_spec((3, QKV_END)), _const_spec((3, B_WIDTH)), _const_spec((N_SMALL, 2))]
    args += [mod3, ng, wts["w_main"], wts["w_small"], wts["conv_qkv"], wts["conv_b"],
             wts["p_row"]]
    wide = pl.BlockSpec((tm, A_WIDTH), lambda i: (i, 0))
    out_specs = [wide, wide, wide, wide, pl.BlockSpec((N_SMALL, tm), lambda i: (0, i)), wide]
    wide_shape = jax.ShapeDtypeStruct((n_rows, A_WIDTH), BF16)
    out_shape = [wide_shape] * 4 + [jax.ShapeDtypeStruct((N_SMALL, n_rows), F32), wide_shape]
    return pl.pallas_call(
        functools.partial(_inproj_kernel, tm=tm, seq_len=seq_len, has_pe=has_pe),
        grid=(n_rows // tm,), in_specs=in_specs, out_specs=out_specs, out_shape=out_shape,
        compiler_params=_params(1), name="inproj",
    )(*args)


def _dot3_left(a, b_exact):
    a1, a2, a3 = _split3(a)
    return _dot(a1, b_exact) + (_dot(a2, b_exact) + _dot(a3, b_exact))


def _chunk_setup(q_ref, k_ref, v_ref, gbr_ref, qg_ref, kg_ref, at_ref, gl_ref,
                 m_scr, rhs_scr, *, n_sub):
    c = CHUNK
    n_beta = N_DIR * H_A
    r1 = lax.broadcasted_iota(jnp.int32, (c, c), 0)
    c1 = lax.broadcasted_iota(jnp.int32, (c, c), 1)
    tri_lo = jnp.where(r1 >= c1, 1.0, 0.0).astype(BF16)
    tri_up = jnp.where(r1 <= c1, 1.0, 0.0).astype(BF16)
    r2 = lax.broadcasted_iota(jnp.int32, (2 * c, 2 * c), 0)
    c2 = lax.broadcasted_iota(jnp.int32, (2 * c, 2 * c), 1)
    same_head = (r2 ^ c2) < c

    def col_pair(a, j):
        col = jnp.concatenate([a[:, j:j + 1], a[:, j + 1:j + 2]], axis=0)
        return jnp.broadcast_to(col, (2 * c, c))

    def row_pair(a, j):
        return jnp.concatenate([a[j:j + 1, :], a[j + 1:j + 2, :]], axis=1)

    def twice(a):
        return jnp.concatenate([a, a], axis=1)

    n = 0
    for ci in range(n_sub):
        rows_ci = slice(ci * c, (ci + 1) * c)
        gbr = gbr_ref[:, rows_ci]
        sub = lax.broadcasted_iota(jnp.int32, gbr.shape, 0)
        gc_r = jnp.where(sub < n_beta + H_A, _dot3_left(gbr, tri_up), _dot3_left(gbr, tri_lo))
        total = jnp.broadcast_to(jnp.sum(gbr, axis=1, keepdims=True), gbr.shape)
        gl_ref[ci] = jnp.exp(total)[n_beta:, :]
        pad = jnp.zeros((c - 2 * N_SMALL, c), F32)
        cols_c = jnp.concatenate([gbr, gc_r, pad], axis=0).T
        gbc = cols_c[:, 0:N_SMALL]
        gc_c = cols_c[:, N_SMALL:2 * N_SMALL]

        q = q_ref[rows_ci, :]
        k = k_ref[rows_ci, :]
        v = v_ref[rows_ci, :]
        for hp in range(H_A // 2):
            h0 = 2 * hp

            def stack(a):
                return jnp.concatenate([a[:, h0 * DK:(h0 + 1) * DK],
                                        a[:, (h0 + 1) * DK:(h0 + 2) * DK]], axis=0)

            q2b, k2b, v2b = stack(q), stack(k), stack(v)
            kk = _dot_nt(k2b, k2b)
            qk = _dot_nt(q2b, k2b)
            for d in range(N_DIR):
                jb = d * H_A + h0
                jg = n_beta + jb
                gcb = col_pair(gc_c, jg)
                beta = col_pair(gbc, jb)
                tot = jnp.concatenate([jnp.broadcast_to(total[jg:jg + 1, :], (c, c)),
                                       jnp.broadcast_to(total[jg + 1:jg + 2, :], (c, c))], axis=0)
                eg = jnp.exp(gcb)
                er = jnp.exp(tot - gcb)
                tri = (r2 >= c2) if d == 0 else (r2 <= c2)
                decay = jnp.exp(jnp.where(same_head & tri, twice(gcb) - row_pair(gc_r, jg), NEG_BIG))
                attn = qk * decay
                m_scr[n] = jnp.where(r2 != c2, (twice(beta) * kk) * decay, 0.0)
                rhs_scr[n] = jnp.concatenate([v2b * beta.astype(BF16),
                                              k2b * (beta * eg).astype(BF16)], axis=1)
                qg = q2b * eg.astype(BF16)
                kg = k2b * er.astype(BF16)
                for hh in range(2):
                    rows = slice(hh * c, (hh + 1) * c)
                    cols = slice((h0 + hh) * DK, (h0 + hh + 1) * DK)
                    qg_ref[d, rows_ci, cols] = qg[rows]
                    kg_ref[d, rows_ci, cols] = kg[rows]
                    at_ref[d, rows_ci, cols] = attn[rows, hh * c:(hh + 1) * c].astype(BF16)
                n += 1
                yield


def _chunk_solve(m_scr, rhs_scr, gsel_ref, u_ref, w_ref, *, n_sub):
    c = CHUNK
    chains = [{"ci": ci, "h0": 2 * hp, "d": d}
              for ci in range(n_sub) for hp in range(H_A // 2) for d in range(N_DIR)]
    for n, ch in enumerate(chains):
        ch["m"] = m_scr[n]
    r2 = lax.broadcasted_iota(jnp.int32, (2 * c, 2 * c), 0)
    c2 = lax.broadcasted_iota(jnp.int32, (2 * c, 2 * c), 1)
    x2 = r2 ^ c2

    sub8 = SUBLANES
    per_half = c // sub8
    lane8 = lax.broadcasted_iota(jnp.int32, (sub8, c), 1)
    row8 = lax.broadcasted_iota(jnp.int32, (sub8, c), 0)
    in_blk = [(lane8 >> 3) == i for i in range(per_half)]
    zeros8 = jnp.zeros((sub8, c), F32)
    packed = []
    for ch in chains:
        halves = []
        for half in range(2):
            pm = zeros8
            for i in range(per_half):
                blk = half * per_half + i
                pm = jnp.where(in_blk[i], ch["m"][blk * sub8:(blk + 1) * sub8,
                                                  half * c:(half + 1) * c], pm)
            halves.append(pm)
        packed.append(jnp.concatenate(halves, axis=1))
    stacked = jnp.concatenate(packed, axis=0).astype(BF16)
    yield
    spread = [_dot(stacked, gsel_ref[k]) for k in range(sub8)]
    yield
    eye8 = jnp.where((lane8 & (sub8 - 1)) == row8, 1.0, 0.0).astype(F32)
    for n, ch in enumerate(chains):
        rows_n = slice(n * sub8, (n + 1) * sub8)
        tp = jnp.concatenate([eye8, eye8], axis=1)
        order = range(sub8 - 1) if ch["d"] == 0 else range(sub8 - 1, 0, -1)
        for k in order:
            tp = tp - spread[k][rows_n] * tp[k:k + 1, :]
        row_groups = []
        for half in range(2):
            tph = tp[:, half * c:(half + 1) * c]
            for i in range(per_half):
                own = jnp.where(in_blk[i], tph, 0.0)
                row_groups.append(jnp.concatenate([own, zeros8] if half == 0 else [zeros8, own],
                                                  axis=1))
        ch["t"] = jnp.concatenate(row_groups, axis=0)
    yield

    def pick(a, b, parity):
        return jnp.concatenate([a[i * b:(i + 1) * b] for i in range(a.shape[0] // b)
                                if i % 2 == parity], axis=0)

    def weave(sel, rest, b, parity):
        blocks = []
        for i in range(2 * sel.shape[0] // b):
            src = sel if i % 2 == parity else rest
            blocks.append(src[(i // 2) * b:(i // 2 + 1) * b])
        return jnp.concatenate(blocks, axis=0)

    par = [1 - ch["d"] for ch in chains]
    for lvl in range(3, 7):
        b = 1 << lvl
        xs = []
        for ch, p in zip(chains, par):
            off = jnp.where((pick(x2, b, p) >> lvl) == 1, pick(ch["m"], b, p), 0.0)
            xs.append(_dot(off.astype(BF16), ch["t"].astype(BF16)))
        yield
        for ch, x, p in zip(chains, xs, par):
            x_full = weave(x, jnp.zeros_like(x), b, p).astype(BF16)
            t_sel = pick(ch["t"], b, p)
            t_new = t_sel - _dot(t_sel.astype(BF16), x_full)
            ch["t"] = weave(t_new, pick(ch["t"], b, 1 - p), b, p)
        yield

    for n, ch in enumerate(chains):
        uw = _dot(ch["t"].astype(BF16), rhs_scr[n])
        rows_ci = slice(ch["ci"] * c, (ch["ci"] + 1) * c)
        for hh in range(2):
            rows = slice(hh * c, (hh + 1) * c)
            cols = slice((ch["h0"] + hh) * DK, (ch["h0"] + hh + 1) * DK)
            u_ref[ch["d"], rows_ci, cols] = uw[rows, 0:DV].astype(BF16)
            w_ref[ch["d"], rows_ci, cols] = uw[rows, DV:2 * DV].astype(BF16)
        if n % 2:
            yield


def _chunk_local_kernel(q_ref, k_ref, v_ref, gbr_ref, gsel_ref,
                        u_ref, w_ref, qg_ref, kg_ref, at_ref, gl_ref,
                        m_a, rhs_a, m_b, rhs_b, *, n_sub):
    setup = functools.partial(_chunk_setup, q_ref, k_ref, v_ref, gbr_ref,
                              qg_ref, kg_ref, at_ref, gl_ref, n_sub=n_sub)
    solve = functools.partial(_chunk_solve, gsel_ref=gsel_ref, u_ref=u_ref, w_ref=w_ref,
                              n_sub=n_sub)
    j = pl.program_id(0)

    @pl.when(j == 0)
    def _():
        u_ref[...] = jnp.zeros(u_ref.shape, u_ref.dtype)
        w_ref[...] = jnp.zeros(w_ref.shape, w_ref.dtype)
        _interleave(setup(m_a, rhs_a))

    @pl.when(j % 2 == 1)
    def _():
        _interleave(solve(m_a, rhs_a), setup(m_b, rhs_b))

    @pl.when((j > 0) & (j % 2 == 0))
    def _():
        _interleave(solve(m_b, rhs_b), setup(m_a, rhs_a))


def _chunk_local(q, k, v, gbr, n_sub=4):
    n_rows = q.shape[0]
    n_chunks = n_rows // CHUNK
    rows = n_sub * CHUNK
    n_blocks = n_chunks // n_sub
    n_sys = n_sub * (H_A // 2) * N_DIR
    cur = lambda j: jnp.minimum(j, n_blocks - 1)
    prev = lambda j: jnp.maximum(j - 1, 0)
    wide = pl.BlockSpec((rows, A_WIDTH), lambda j: (cur(j), 0))
    out_cur = pl.BlockSpec((N_DIR, rows, A_WIDTH), lambda j: (0, cur(j), 0))
    out_prev = pl.BlockSpec((N_DIR, rows, A_WIDTH), lambda j: (0, prev(j), 0))
    big = lambda dt: jax.ShapeDtypeStruct((N_DIR, n_rows, A_WIDTH), dt)
    idx = jnp.arange(2 * CHUNK)
    gsel = ((idx[None, :, None] // SUBLANES == idx[None, None, :] // SUBLANES)
            & (idx[None, :, None] % SUBLANES == jnp.arange(SUBLANES)[:, None, None])).astype(BF16)
    return pl.pallas_call(
        functools.partial(_chunk_local_kernel, n_sub=n_sub),
        grid=(n_blocks + 1,),
        in_specs=[wide, wide, wide,
                  pl.BlockSpec((N_SMALL, rows), lambda j: (0, cur(j))),
                  _const_spec((SUBLANES, 2 * CHUNK, 2 * CHUNK))],
        out_specs=[out_prev, out_prev, out_cur, out_cur, out_cur,
                   pl.BlockSpec((n_sub, N_DIR * H_A, CHUNK), lambda j: (cur(j), 0, 0))],
        out_shape=[big(BF16)] * 5 + [jax.ShapeDtypeStruct((n_chunks, N_DIR * H_A, CHUNK), F32)],
        scratch_shapes=[pltpu.VMEM((n_sys, 2 * CHUNK, 2 * CHUNK), F32),
                        pltpu.VMEM((n_sys, 2 * CHUNK, 2 * CHUNK), BF16),
                        pltpu.VMEM((n_sys, 2 * CHUNK, 2 * CHUNK), F32),
                        pltpu.VMEM((n_sys, 2 * CHUNK, 2 * CHUNK), BF16)],
        compiler_params=_params(1, ("arbitrary",)), name="chunk_local",
    )(q, k, v, gbr, gsel)


def _scan_kernel(*refs, n_chunks, zero_init, write_final):
    fwd = refs[0:6]
    bwd = refs[6:12]
    refs = refs[12:]
    if not zero_init:
        s0_ref = refs[0]
        refs = refs[1:]
    of_ref, ob_ref = refs[0:2]
    refs = refs[2:]
    if write_final:
        sfin_ref = refs[0]
        refs = refs[1:]
    s_scr = refs[0]
    i = pl.program_id(1)

    @pl.when(i == 0)
    def _():
        if zero_init:
            s_scr[...] = jnp.zeros(s_scr.shape, F32)
        else:
            s_scr[...] = s0_ref[...]

    problems = [(s, d, h) for s in range(SEQ_GROUP) for d in range(N_DIR) for h in range(H_A)]
    c = CHUNK
    stage1 = []
    for s, d, h in problems:
        u_ref, w_ref, qg_ref, _, _, _ = fwd if d == 0 else bwd
        cols = slice(h * DV, (h + 1) * DV)
        sb = s_scr[s, d, h].astype(BF16)
        lhs = jnp.concatenate([w_ref[0, s, :, cols], qg_ref[0, s, :, cols]], axis=0)
        stage1.append(_dot(lhs, sb))
    for (s, d, h), ws_qs in zip(problems, stage1):
        u_ref, _, _, kg_ref, at_ref, gl_ref = fwd if d == 0 else bwd
        o_ref = of_ref if d == 0 else ob_ref
        cols = slice(h * DV, (h + 1) * DV)
        vb = (u_ref[0, s, :, cols].astype(F32) - ws_qs[0:c]).astype(BF16)
        o_ref[s, :, cols] = (ws_qs[c:2 * c] + _dot(at_ref[0, s, :, cols], vb)).astype(BF16)
        gl = gl_ref[s, 0, d * H_A + h:d * H_A + h + 1, :]
        s_scr[s, d, h] = s_scr[s, d, h] * gl + _dot_tn(kg_ref[0, s, :, cols], vb)

    if write_final:
        @pl.when(i == n_chunks - 1)
        def _():
            sfin_ref[...] = s_scr[...]


def _scan(local, s0, n_seq, seq_len, write_final):
    u, w, qg, kg, at, gl = local
    n_chunks = seq_len // CHUNK
    zero_init = s0 is None
    shp4 = (N_DIR, n_seq, seq_len, A_WIDTH)
    arrs = [a.reshape(shp4) for a in (u, w, qg, kg, at)]
    gl4 = gl.reshape(n_seq, n_chunks, N_DIR * H_A, CHUNK)
    sg = SEQ_GROUP

    def specs(d, chunk_of):
        big = pl.BlockSpec((1, sg, CHUNK, A_WIDTH), lambda g, i: (d, g, chunk_of(i), 0))
        return [big] * 5 + [pl.BlockSpec((sg, 1, N_DIR * H_A, CHUNK),
                                         lambda g, i: (g, chunk_of(i), 0, 0))]

    rev = lambda i: n_chunks - 1 - i
    in_specs = specs(0, lambda i: i) + specs(1, rev)
    args = arrs + [gl4] + arrs + [gl4]
    state_spec = pl.BlockSpec((sg, N_DIR, H_A, DK, DV), lambda g, i: (g, 0, 0, 0, 0))
    if not zero_init:
        in_specs.append(state_spec)
        args.append(s0)
    o_shape = jax.ShapeDtypeStruct((n_seq, seq_len, A_WIDTH), BF16)
    out_specs = [pl.BlockSpec((sg, CHUNK, A_WIDTH), lambda g, i: (g, i, 0)),
                 pl.BlockSpec((sg, CHUNK, A_WIDTH), lambda g, i: (g, rev(i), 0))]
    out_shape = [o_shape, o_shape]
    if write_final:
        out_specs.append(state_spec)
        out_shape.append(jax.ShapeDtypeStruct((n_seq, N_DIR, H_A, DK, DV), F32))
    return pl.pallas_call(
        functools.partial(_scan_kernel, n_chunks=n_chunks, zero_init=zero_init,
                          write_final=write_final),
        grid=(n_seq // sg, n_chunks), in_specs=in_specs, out_specs=out_specs, out_shape=out_shape,
        scratch_shapes=[pltpu.VMEM((sg, N_DIR, H_A, DK, DV), F32)],
        compiler_params=_params(2, ("parallel", "arbitrary")), name="delta_scan",
    )(*args)


def _delta_mixer_rows(o_f, o_b, gate, yb, og):
    o = o_f + o_b
    sg = _silu(gate)
    parts = []
    for h in range(H_A):
        cols = slice(h * DV, (h + 1) * DV)
        oh = o[:, cols]
        ms = jnp.mean(oh * oh, axis=-1, keepdims=True)
        parts.append((oh * lax.rsqrt(ms + EPS)) * og * sg[:, cols])
    return jnp.concatenate(parts + [yb], axis=1)


def _ffn_kernel(*refs, tm, seq_len, final_norm, delta, has_pe):
    xp, xm, xn = refs[:3]
    refs = refs[3:]
    if delta:
        if has_pe:
            pe_refs = refs[:3]
            refs = refs[3:]
        mix_refs = refs[:12]
        og_ref, wout_ref = refs[12:14]
        refs = refs[14:]
    mod_ref, ng_ref, wg_ref, wu_ref, wd_ref, cf_ref = refs[:6]
    refs = refs[6:]
    if final_norm:
        fg_ref = refs[0]
        refs = refs[1:]
    out_ref = refs[0]
    vp, vn = _halo_valid(tm, seq_len)
    mod = mod_ref[0]
    shift = mod[:, 3 * D_MODEL:4 * D_MODEL]
    scale = mod[:, 4 * D_MODEL:5 * D_MODEL]
    gate2 = mod[:, 5 * D_MODEL:6 * D_MODEL]
    g = ng_ref[...]
    if delta:
        def mixer_rows(which):
            vals = []
            for a in range(4):
                r = mix_refs[3 * a + which][...].astype(F32)
                if which == 0:
                    r = r[HALO:2 * HALO]
                elif which == 2:
                    r = r[0:HALO]
                vals.append(r)
            return _delta_mixer_rows(*vals, og_ref[...])

        a_ext = jnp.concatenate([mixer_rows(0), mixer_rows(1), mixer_rows(2)], axis=0)
        x_ext = jnp.concatenate([xp[...], xm[...], xn[...]], axis=0)
        if has_pe:
            x_ext = x_ext + jnp.concatenate([r[...] for r in pe_refs], axis=0)
        gate1 = mod[:, 2 * D_MODEL:3 * D_MODEL]
        x_ext = x_ext + gate1 * _dot(a_ext.astype(BF16), wout_ref[...])
        x = x_ext[HALO:HALO + tm]
        h_all = _norm_mod(x_ext, g, scale, shift)
        h_main = h_all[HALO:HALO + tm]
        h_ext = jnp.concatenate([h_all[0:HALO] * vp, h_main, h_all[HALO + tm:] * vn], axis=0)
    else:
        x = xm[...]
        h_main = _norm_mod(x, g, scale, shift)
        h_ext = jnp.concatenate([_norm_mod(xp[...], g, scale, shift) * vp, h_main,
                                 _norm_mod(xn[...], g, scale, shift) * vn], axis=0)
    hb_ext = h_ext.astype(BF16)
    hb = h_main.astype(BF16)
    edge_masks = _seq_edge_masks(tm, seq_len, FF_CHUNK)
    acts = []
    for j in range(N_FF_CHUNKS):
        cols = slice(j * FF_CHUNK, (j + 1) * FF_CHUNK)
        u = _dwconv3_ext(_dot(hb_ext, wg_ref[:, cols]), cf_ref[:, cols], tm, edge_masks)
        acts.append((_silu(u) * _dot(hb, wu_ref[:, cols])).astype(BF16))
    y = x + gate2 * _dot(jnp.concatenate(acts, axis=1), wd_ref[...])
    if final_norm:
        ms = jnp.mean(y * y, axis=-1, keepdims=True)
        y = (y * lax.rsqrt(ms + EPS)) * fg_ref[...]
    out_ref[...] = y


def _ffn(x, mod3, row_base, row_stride, seq_len, tm, ng, wts, layer, final_g, delta=None):
    n_rows = x.shape[0]
    final_norm = final_g is not None
    of_layer = lambda *shape: pl.BlockSpec((None,) + shape, lambda i: (layer, 0, 0))
    in_specs = _tile_specs(tm, D_MODEL, n_rows)
    args = [x, x, x]
    has_pe = False
    if delta is not None:
        pe, o_f, o_b, gate, yb, og, w_out = delta
        has_pe = pe is not None
        if has_pe:
            in_specs += _pe_specs(tm, D_MODEL, seq_len)
            args += [pe, pe, pe]
        for arr in (o_f, o_b, gate, yb):
            in_specs += _tile_specs(tm, A_WIDTH, n_rows, halo=2 * HALO)
            args += [arr, arr, arr]
        in_specs += [_const_spec((1, DV)), _const_spec((D_MODEL, D_MODEL))]
        args += [og, w_out]
    in_specs += [_mod_spec(tm, seq_len, row_base, row_stride), _const_spec((1, D_MODEL)),
                 of_layer(D_MODEL, D_FF), of_layer(D_MODEL, D_FF), of_layer(D_FF, D_MODEL),
                 of_layer(3, D_FF)]
    args += [mod3, ng, wts["w_gate"], wts["w_up"], wts["w_down"], wts["conv"]]
    if final_norm:
        in_specs.append(_const_spec((1, D_MODEL)))
        args.append(final_g)
    return pl.pallas_call(
        functools.partial(_ffn_kernel, tm=tm, seq_len=seq_len, final_norm=final_norm,
                          delta=delta is not None, has_pe=has_pe),
        grid=(n_rows // tm,), in_specs=in_specs,
        out_specs=pl.BlockSpec((tm, D_MODEL), lambda i: (i, 0)),
        out_shape=jax.ShapeDtypeStruct((n_rows, D_MODEL), F32),
        compiler_params=_params(1), name="conv_ffn",
    )(*args)


def _pool_group(h_ext, n_out, pos, seq_len, wp_ref, gi):
    win = POOL_WINDOWS[gi]
    cols = slice(gi * POOL_GROUP, (gi + 1) * POOL_GROUP)
    s = h_ext[:, cols]
    width = 1
    while width < win:
        s = s + _shift_rows(s, width)
        width *= 2
    start = HALO - win // 2
    if start:
        s = _shift_rows(s, start)
    s = s[0:n_out]
    lo = jnp.maximum(pos - win // 2, 0)
    hi = jnp.minimum(pos - win // 2 + win, seq_len)
    count = jnp.maximum(hi - lo, 1).astype(F32)
    mixed = s / jnp.concatenate([count, count], axis=1) - h_ext[HALO:HALO + n_out, cols]
    return _dot(mixed.astype(BF16), wp_ref[gi])


def _pool_stage(xp, xm, xn, mod_ref, g1_ref, wp_ref, ps_ref, g2_ref, hb_scr, xr_scr,
                *, tm, seq_len, n_tiles):
    wide = 2 * HALO
    tile = jnp.minimum(pl.program_id(0), n_tiles - 1)
    first = tile * tm
    vp = jnp.where(first % seq_len != 0, 1.0, 0.0).astype(F32)
    vn = jnp.where((first + tm) % seq_len != 0, 1.0, 0.0).astype(F32)
    mod = mod_ref[0]
    shift1, scale1, gate1 = (mod[:, k * D_MODEL:(k + 1) * D_MODEL] for k in range(3))
    shift2, scale2 = (mod[:, k * D_MODEL:(k + 1) * D_MODEL] for k in (3, 4))
    g1 = g1_ref[...]
    g2 = g2_ref[...]
    lanes = POOL_GROUP // 2
    if tm <= seq_len:
        n_out = tm + 2 * HALO
        segments = [(jnp.concatenate([_norm_mod(xp[...], g1, scale1, shift1) * vp,
                                      _norm_mod(xm[...], g1, scale1, shift1),
                                      _norm_mod(xn[...], g1, scale1, shift1) * vn], axis=0),
                     (first - HALO + lax.broadcasted_iota(jnp.int32, (n_out, lanes), 0)) % seq_len)]
        x_res = jnp.concatenate([xp[HALO:, :], xm[...], xn[0:HALO, :]], axis=0)
    else:
        n_out = seq_len
        x_res = xm[...]
        h1 = _norm_mod(x_res, g1, scale1, shift1)
        zeros = jnp.zeros((HALO, D_MODEL), F32)
        pos = lax.broadcasted_iota(jnp.int32, (seq_len, lanes), 0)
        segments = [(jnp.concatenate([zeros, h1[s * seq_len:(s + 1) * seq_len], zeros], axis=0), pos)
                    for s in range(tm // seq_len)]
    yield
    ys = []
    for h_ext, pos in segments:
        groups = []
        for gi in range(N_POOL_GROUPS):
            groups.append(_pool_group(h_ext, n_out, pos, seq_len, wp_ref, gi))
            yield
        ys.append(jnp.concatenate(groups, axis=1))
    x3 = x_res + gate1 * (jnp.concatenate(ys, axis=0) * ps_ref[...])
    h2 = _norm_mod(x3, g2, scale2, shift2)
    if tm <= seq_len:
        zeros = jnp.zeros((HALO, D_MODEL), F32)
        hb_scr[...] = jnp.concatenate([zeros, h2[0:HALO] * vp, h2[HALO:HALO + tm],
                                       h2[HALO + tm:] * vn, zeros], axis=0).astype(BF16)
        xr_scr[...] = x3[HALO:HALO + tm]
    else:
        zeros_w = jnp.zeros((wide, D_MODEL), F32)
        hb_scr[...] = jnp.concatenate([zeros_w, h2, zeros_w], axis=0).astype(BF16)
        xr_scr[...] = x3
    yield


def _ffn_stage(hb_scr, xr_scr, mod_ref, wg_ref, wu_ref, wd_ref, cf_ref, fg_ref, out_ref,
               *, tm, seq_len):
    wide = 2 * HALO
    gate2 = mod_ref[0][:, 5 * D_MODEL:6 * D_MODEL]
    hb_ext = hb_scr[...]
    hb = hb_scr[wide:wide + tm, :]
    edge_masks = _seq_edge_masks(tm, seq_len, FF_CHUNK)
    acts = []
    for j in range(N_FF_CHUNKS):
        cols = slice(j * FF_CHUNK, (j + 1) * FF_CHUNK)
        u = _dwconv3_ext(_dot(hb_ext, wg_ref[:, cols]), cf_ref[:, cols], tm, edge_masks, wide)
        acts.append((_silu(u) * _dot(hb, wu_ref[:, cols])).astype(BF16))
        yield
    y = xr_scr[...] + gate2 * _dot(jnp.concatenate(acts, axis=1), wd_ref[...])
    ms = jnp.mean(y * y, axis=-1, keepdims=True)
    out_ref[...] = (y * lax.rsqrt(ms + EPS)) * fg_ref[...]
    yield


def _pool_ffn_kernel(xp, xm, xn, mod_cur, mod_prev, g1_ref, wp_ref, ps_ref, g2_ref,
                     wg_ref, wu_ref, wd_ref, cf_ref, fg_ref, out_ref,
                     hb_a, xr_a, hb_b, xr_b, *, tm, seq_len, n_tiles):
    stage = functools.partial(_pool_stage, xp, xm, xn, mod_cur, g1_ref, wp_ref, ps_ref, g2_ref,
                              tm=tm, seq_len=seq_len, n_tiles=n_tiles)
    finish = functools.partial(_ffn_stage, mod_ref=mod_prev, wg_ref=wg_ref, wu_ref=wu_ref,
                               wd_ref=wd_ref, cf_ref=cf_ref, fg_ref=fg_ref, out_ref=out_ref,
                               tm=tm, seq_len=seq_len)
    j = pl.program_id(0)

    @pl.when(j == 0)
    def _():
        out_ref[...] = jnp.zeros(out_ref.shape, out_ref.dtype)
        _interleave(stage(hb_a, xr_a))

    @pl.when(j % 2 == 1)
    def _():
        _interleave(finish(hb_a, xr_a), stage(hb_b, xr_b))

    @pl.when((j > 0) & (j % 2 == 0))
    def _():
        _interleave(finish(hb_b, xr_b), stage(hb_a, xr_a))


def _pool_ffn(x, mod3, row_base, row_stride, seq_len, tm, g1, w_pool, pool_scale, g2, wts, layer,
              final_g):
    n_rows = x.shape[0]
    n_tiles = n_rows // tm
    wide = 2 * HALO
    per = tm // wide
    last = n_rows // wide - 1
    cur = lambda j: jnp.minimum(j, n_tiles - 1)
    prev = lambda j: jnp.maximum(j - 1, 0)
    mod_row = lambda t: row_base + ((t * tm) // seq_len) * row_stride
    of_layer = lambda *shape: pl.BlockSpec((None,) + shape, lambda j: (layer, 0, 0))
    in_specs = [
        pl.BlockSpec((wide, D_MODEL), lambda j: (jnp.maximum(cur(j) * per - 1, 0), 0)),
        pl.BlockSpec((tm, D_MODEL), lambda j: (cur(j), 0)),
        pl.BlockSpec((wide, D_MODEL), lambda j: (jnp.minimum((cur(j) + 1) * per, last), 0)),
        pl.BlockSpec((1, 1, N_MOD * D_MODEL), lambda j: (mod_row(cur(j)), 0, 0)),
        pl.BlockSpec((1, 1, N_MOD * D_MODEL), lambda j: (mod_row(prev(j)), 0, 0)),
        _const_spec((1, D_MODEL)), _const_spec((N_POOL_GROUPS, POOL_GROUP, POOL_GROUP)),
        _const_spec((1, D_MODEL)), _const_spec((1, D_MODEL)),
        of_layer(D_MODEL, D_FF), of_layer(D_MODEL, D_FF), of_layer(D_FF, D_MODEL),
        of_layer(3, D_FF), _const_spec((1, D_MODEL))]
    return pl.pallas_call(
        functools.partial(_pool_ffn_kernel, tm=tm, seq_len=seq_len, n_tiles=n_tiles),
        grid=(n_tiles + 1,), in_specs=in_specs,
        out_specs=pl.BlockSpec((tm, D_MODEL), lambda j: (prev(j), 0)),
        out_shape=jax.ShapeDtypeStruct((n_rows, D_MODEL), F32),
        scratch_shapes=[pltpu.VMEM((tm + 2 * wide, D_MODEL), BF16), pltpu.VMEM((tm, D_MODEL), F32),
                        pltpu.VMEM((tm + 2 * wide, D_MODEL), BF16), pltpu.VMEM((tm, D_MODEL), F32)],
        compiler_params=_params(1, ("arbitrary",)), name="pool_ffn",
    )(x, x, x, mod3, mod3, g1, w_pool, pool_scale, g2, wts["w_gate"], wts["w_up"], wts["w_down"],
      wts["conv"], final_g)


def _sincos_2d(rows, cols, d):
    quarter = d // 4
    omega = 1.0 / (10000.0 ** (jnp.arange(quarter, dtype=F32) / quarter))
    er = jnp.arange(rows, dtype=F32)[:, None] * omega[None, :]
    ec = jnp.arange(cols, dtype=F32)[:, None] * omega[None, :]
    er = jnp.concatenate([jnp.sin(er), jnp.cos(er)], axis=-1)
    ec = jnp.concatenate([jnp.sin(ec), jnp.cos(ec)], axis=-1)
    pe = jnp.concatenate([jnp.broadcast_to(er[:, None, :], (rows, cols, d // 2)),
                          jnp.broadcast_to(ec[None, :, :], (rows, cols, d // 2))], axis=-1)
    return pe.reshape(rows * cols, d)


def _even_layer_weights(w_in, conv_qkv, a_log, dt_bias, conv_b, w_out):
    small = w_in[:, GATE_END:ALPHA_END]
    zeros = jnp.zeros((N_DIR * H_A,), F32)
    ea = jnp.concatenate([zeros, jnp.exp(a_log.astype(F32)).reshape(-1)])
    dtb = jnp.concatenate([zeros, dt_bias.astype(F32).reshape(-1)])
    return {
        "w_main": jnp.concatenate([w_in[:, :GATE_END], w_in[:, ALPHA_END:IN_AB]],
                                  axis=1).astype(BF16),
        "w_small": jnp.pad(small, ((0, 0), (0, CHUNK - N_SMALL))),
        "conv_qkv": conv_qkv, "conv_b": conv_b,
        "p_row": jnp.stack([ea, dtb], axis=1),
        "w_out": w_out.astype(BF16),
    }


def _ffn_weights(w_gate, w_up, conv, w_down):
    return {"w_gate": w_gate.astype(BF16), "w_up": w_up.astype(BF16),
            "w_down": w_down.astype(BF16), "conv": conv}


def _trunk(x, pe, mod, row_base, row_stride, n_seq, seq_len, tm, tm_ffn, s0, write_final, p):
    row = lambda a: a.reshape(1, -1)
    finals = None
    depth = mod.shape[0]
    for layer in range(depth):
        mod3 = mod[layer][:, None, :]
        place = (mod3, row_base, row_stride, seq_len, tm)
        place_ffn = (mod3, row_base, row_stride, seq_len, tm_ffn // 2)
        if layer % 2 == 0:
            e = layer // 2
            wts = p["even"][e]
            q, k, v, gate, gbr, yb = _inproj(x, pe, *place, row(p["norm_mix_g"][layer]), wts)
            local = _chunk_local(q, k, v, gbr)
            res = _scan(local, None if s0 is None else s0[:, e], n_seq, seq_len, write_final)
            o_f = res[0].reshape(-1, A_WIDTH)
            o_b = res[1].reshape(-1, A_WIDTH)
            if write_final:
                finals = res[2]
            delta = (pe, o_f, o_b, gate, yb, row(p["o_norm_g"][e]), wts["w_out"])
            x = _ffn(x, *place_ffn, row(p["norm_ffn_g"][layer]), p["ffn"], layer, None, delta)
        else:
            o = layer // 2
            assert layer == depth - 1
            x = _pool_ffn(x, *place_ffn, row(p["norm_mix_g"][layer]), p["w_pool"][o],
                          row(p["pool_scale"][o]), row(p["norm_ffn_g"][layer]), p["ffn"], layer,
                          row(p["final_norm_g"]))
    return x, finals


def kernel(x_prompt, x_sample, state_delta, c, c_ctx, norm_mix_g, norm_ffn_g, w_ada, b_ada,
           w_in_ab, conv_qkv, a_log, dt_bias, o_norm_g, conv_b, w_out_ab, w_pool, pool_scale,
           w_ffn_gate, w_ffn_up, ffn_conv, w_ffn_down, final_norm_g):
    batch, seq, d = x_prompt.shape
    dec_batch, dec_seq, _ = x_sample.shape
    depth = w_ada.shape[0]
    n_even = w_in_ab.shape[0]
    assert d == D_MODEL and depth == 2 and n_even == 1

    p = {
        "norm_mix_g": norm_mix_g, "norm_ffn_g": norm_ffn_g, "o_norm_g": o_norm_g,
        "pool_scale": pool_scale, "final_norm_g": final_norm_g,
        "w_pool": w_pool.astype(BF16),
        "even": [_even_layer_weights(w_in_ab[e], conv_qkv[e], a_log[e], dt_bias[e], conv_b[e],
                                     w_out_ab[e]) for e in range(n_even)],
        "ffn": _ffn_weights(w_ffn_gate, w_ffn_up, ffn_conv, w_ffn_down),
    }

    n_cond = 1 + dec_batch
    pad = (-n_cond) % HALO
    cond = jnp.concatenate([c_ctx[None, :], c, jnp.zeros((pad, d), F32)], axis=0)
    mod = _modulation(cond, w_ada, b_ada)

    y_prompt, finals = _trunk(x_prompt.reshape(batch * seq, d), None, mod, 0, 0, batch, seq,
                              min(seq, 256), 1024, None, True, p)
    pe = _sincos_2d(dec_seq // GRID_W, GRID_W, d)
    y_sample, _ = _trunk(x_sample.reshape(dec_batch * dec_seq, d), pe, mod, 1, 1, dec_batch,
                         dec_seq, 512, 1024, state_delta, False, p)
    return (y_prompt.reshape(batch, seq, d), y_sample.reshape(dec_batch, dec_seq, d),
            finals[:, None])
```

```python
import functools

import jax
import jax.numpy as jnp
from jax import lax
from jax.experimental import pallas as pl
from jax.experimental.pallas import tpu as pltpu

F32 = jnp.float32
BF16 = jnp.bfloat16

D_MODEL = 1024
GRID_W = 64
A_WIDTH = 512
B_WIDTH = 512
DK = 128
DV = 128
H_A = 4
N_DIR = 2
N_POOL_GROUPS = 4
POOL_GROUP = D_MODEL // N_POOL_GROUPS
POOL_WINDOWS = (2, 4, 8, 16)
D_FF = 2816
N_MOD = 6
EPS = 1e-6
QKV_END = 3 * A_WIDTH
GATE_END = QKV_END + A_WIDTH
BETA_END = GATE_END + N_DIR * H_A
ALPHA_END = BETA_END + N_DIR * H_A
BG_END = ALPHA_END + B_WIDTH
CG_END = BG_END + B_WIDTH
IN_AB = CG_END + B_WIDTH

SUBLANES = 8
HALO = SUBLANES
CHUNK = 128
FF_CHUNK = 256
N_FF_CHUNKS = D_FF // FF_CHUNK
N_SMALL = 2 * N_DIR * H_A
SEQ_GROUP = 4
VMEM_LIMIT = 56 * 1024 * 1024
NEG_BIG = -1e30


def _sigmoid(x):
    return 1.0 / (1.0 + jnp.exp(-x))


def _silu(x):
    return x * _sigmoid(x)


def _softplus(x):
    return jnp.maximum(x, 0.0) + jnp.log1p(jnp.exp(-jnp.abs(x)))


def _dot(a, b):
    return jnp.dot(a, b, preferred_element_type=F32)


def _dot_nt(a, b):
    return lax.dot_general(a, b, (((1,), (1,)), ((), ())), preferred_element_type=F32)


def _dot_tn(a, b):
    return lax.dot_general(a, b, (((0,), (0,)), ((), ())), preferred_element_type=F32)


def _split3(a):
    a1 = a.astype(BF16)
    r1 = a - a1.astype(F32)
    a2 = r1.astype(BF16)
    a3 = (r1 - a2.astype(F32)).astype(BF16)
    return a1, a2, a3


def _norm_mod(x, g, scale, shift):
    ms = jnp.mean(x * x, axis=-1, keepdims=True)
    return (x * lax.rsqrt(ms + EPS)) * (g * (1.0 + scale)) + shift


def _shift_rows(a, k):
    n = a.shape[0]
    return pltpu.roll(a, (-k) % n, axis=0)


def _dwconv3_ext(p_ext, w, tm, edge_masks=None, halo=HALO):
    prev = _shift_rows(p_ext, -1)[halo:halo + tm]
    cur = p_ext[halo:halo + tm]
    nxt = _shift_rows(p_ext, 1)[halo:halo + tm]
    if edge_masks is not None:
        prev = prev * edge_masks[0]
        nxt = nxt * edge_masks[1]
    return prev * w[0:1] + cur * w[1:2] + nxt * w[2:3]


def _seq_edge_masks(tm, seq_len, width):
    if tm <= seq_len:
        return None
    pos = lax.broadcasted_iota(jnp.int32, (tm, width), 0) % seq_len
    return (jnp.where(pos != 0, 1.0, 0.0).astype(F32),
            jnp.where(pos != seq_len - 1, 1.0, 0.0).astype(F32))


def _halo_valid(tm, seq_len):
    i = pl.program_id(0)
    vp = jnp.where((i * tm) % seq_len != 0, 1.0, 0.0).astype(F32)
    vn = jnp.where(((i + 1) * tm) % seq_len != 0, 1.0, 0.0).astype(F32)
    return vp, vn


def _const_spec(shape):
    nd = len(shape)
    return pl.BlockSpec(shape, lambda *_: (0,) * nd)


def _tile_specs(tm, width, n_rows, halo=HALO):
    per = tm // halo
    last = n_rows // halo - 1
    prev = pl.BlockSpec((halo, width), lambda i: (jnp.maximum(i * per - 1, 0), 0))
    main = pl.BlockSpec((tm, width), lambda i: (i, 0))
    nxt = pl.BlockSpec((halo, width), lambda i: (jnp.minimum((i + 1) * per, last), 0))
    return [prev, main, nxt]


def _pe_specs(tm, width, seq_len):
    per = tm // HALO
    tiles = seq_len // tm
    last = seq_len // HALO - 1
    prev = pl.BlockSpec((HALO, width), lambda i: (jnp.maximum((i % tiles) * per - 1, 0), 0))
    main = pl.BlockSpec((tm, width), lambda i: (i % tiles, 0))
    nxt = pl.BlockSpec((HALO, width), lambda i: (jnp.minimum((i % tiles + 1) * per, last), 0))
    return [prev, main, nxt]


def _mod_spec(tm, seq_len, row_base, row_stride):
    return pl.BlockSpec((1, 1, N_MOD * D_MODEL),
                        lambda i: (row_base + ((i * tm) // seq_len) * row_stride, 0, 0))


def _params(n_axes=1, semantics=None):
    return pltpu.CompilerParams(
        dimension_semantics=semantics or ("parallel",) * n_axes,
        vmem_limit_bytes=VMEM_LIMIT)


def _interleave(*stages):
    live = list(stages)
    while live:
        for stage in list(live):
            if next(stage, StopIteration) is StopIteration:
                live.remove(stage)


def _mod_kernel(c_ref, w_ref, b_ref, o_ref):
    c = c_ref[...]
    s1, s2, _ = _split3(_silu(c))
    w = w_ref[0]
    w1 = w.astype(BF16)
    w2 = (w - w1.astype(F32)).astype(BF16)
    o_ref[0] = _dot(s1, w1) + (_dot(s2, w1) + _dot(s1, w2)) + b_ref[0]


def _modulation(cond, w_ada, b_ada):
    depth, _, width = w_ada.shape
    rows = cond.shape[0]
    tn = 1536
    return pl.pallas_call(
        _mod_kernel,
        grid=(depth, width // tn),
        in_specs=[pl.BlockSpec((rows, D_MODEL), lambda l, j: (0, 0)),
                  pl.BlockSpec((1, D_MODEL, tn), lambda l, j: (l, 0, j)),
                  pl.BlockSpec((1, 1, tn), lambda l, j: (l, 0, j))],
        out_specs=pl.BlockSpec((1, rows, tn), lambda l, j: (l, 0, j)),
        out_shape=jax.ShapeDtypeStruct((depth, rows, width), F32),
        compiler_params=_params(2),
        name="modulation",
    )(cond, w_ada, b_ada.reshape(depth, 1, width))


def _inproj_kernel(*refs, tm, seq_len, has_pe):
    if has_pe:
        xp, xm, xn, pp, pm, pn = refs[:6]
        refs = refs[6:]
    else:
        xp, xm, xn = refs[:3]
        pp = pm = pn = None
        refs = refs[3:]
    (mod_ref, ng_ref, w_ref, wsc_ref, cq_ref, cb_ref, pr_ref,
     q_ref, k_ref, v_ref, gate_ref, gbr_ref, yb_ref) = refs

    vp, vn = _halo_valid(tm, seq_len)
    mod = mod_ref[0]
    shift = mod[:, 0:D_MODEL]
    scale = mod[:, D_MODEL:2 * D_MODEL]
    g = ng_ref[...]

    def prep(x_ref, pe_ref):
        x = x_ref[...]
        if has_pe:
            x = x + pe_ref[...]
        return _norm_mod(x, g, scale, shift)

    h_main = prep(xm, pm)
    h_ext = jnp.concatenate([prep(xp, pp) * vp, h_main, prep(xn, pn) * vn], axis=0)
    hb_ext = h_ext.astype(BF16)
    hb = h_main.astype(BF16)

    slab = FF_CHUNK
    per_group = A_WIDTH // slab

    def project(lhs, group, half):
        lo = group * A_WIDTH + half * slab
        return _dot(lhs, w_ref[:, lo:lo + slab])

    cq = cq_ref[...]
    outs = (q_ref, k_ref, v_ref)
    short = []
    for part in range(3):
        for half in range(per_group):
            cols = slice(part * A_WIDTH + half * slab, part * A_WIDTH + (half + 1) * slab)
            p = project(hb_ext, part, half)
            short.append(project(hb_ext, 4 + part, half))
            a = _silu(_dwconv3_ext(p, cq[:, cols], tm))
            if part < 2:
                heads = []
                for h in range(slab // DK):
                    ah = a[:, h * DK:(h + 1) * DK]
                    ss = jnp.sum(ah * ah, axis=-1, keepdims=True)
                    nrm = lax.rsqrt(ss + EPS)
                    if part == 0:
                        nrm = nrm * (DK ** -0.5)
                    heads.append(ah * nrm)
                a = jnp.concatenate(heads, axis=1)
            outs[part][:, half * slab:(half + 1) * slab] = a.astype(BF16)

    cb = cb_ref[...]
    for half in range(per_group):
        cols = slice(half * slab, (half + 1) * slab)
        gate_ref[:, cols] = project(hb, 3, half).astype(BF16)
        bg, cg, hx = (short[part * per_group + half] for part in range(3))
        yb_ref[:, cols] = (bg[HALO:HALO + tm]
                           * _dwconv3_ext(cg * hx, cb[:, cols], tm)).astype(BF16)

    pr = pr_ref[...]
    w_small = wsc_ref[...].T[0:N_SMALL].astype(BF16)
    bar = _dot_nt(w_small, hb)
    sub = lax.broadcasted_iota(jnp.int32, bar.shape, 0)
    gbr_ref[...] = jnp.where(sub < N_DIR * H_A, _sigmoid(bar),
                             -pr[:, 0:1] * _softplus(bar + pr[:, 1:2]))


def _inproj(x, pe, mod3, row_base, row_stride, seq_len, tm, ng, wts):
    n_rows = x.shape[0]
    has_pe = pe is not None
    in_specs = _tile_specs(tm, D_MODEL, n_rows)
    args = [x, x, x]
    if has_pe:
        in_specs += _pe_specs(tm, D_MODEL, seq_len)
        args += [pe, pe, pe]
    in_specs += [_mod_spec(tm, seq_len, row_base, row_stride), _const_spec((1, D_MODEL)),
                 _const_spec((D_MODEL, GATE_END + 3 * B_WIDTH)), _const_spec((D_MODEL, CHUNK)),
                 _const_spec((3, QKV_END)), _const_spec((3, B_WIDTH)), _const_spec((N_SMALL, 2))]
    args += [mod3, ng, wts["w_main"], wts["w_small"], wts["conv_qkv"], wts["conv_b"],
             wts["p_row"]]
    wide = pl.BlockSpec((tm, A_WIDTH), lambda i: (i, 0))
    out_specs = [wide, wide, wide, wide, pl.BlockSpec((N_SMALL, tm), lambda i: (0, i)), wide]
    wide_shape = jax.ShapeDtypeStruct((n_rows, A_WIDTH), BF16)
    out_shape = [wide_shape] * 4 + [jax.ShapeDtypeStruct((N_SMALL, n_rows), F32), wide_shape]
    return pl.pallas_call(
        functools.partial(_inproj_kernel, tm=tm, seq_len=seq_len, has_pe=has_pe),
        grid=(n_rows // tm,), in_specs=in_specs, out_specs=out_specs, out_shape=out_shape,
        compiler_params=_params(1), name="inproj",
    )(*args)


def _dot3_left(a, b_exact):
    a1, a2, a3 = _split3(a)
    return _dot(a1, b_exact) + (_dot(a2, b_exact) + _dot(a3, b_exact))


def _chunk_setup(q_ref, k_ref, v_ref, gbr_ref, qg_ref, kg_ref, at_ref, gl_ref,
                 m_scr, rhs_scr, *, n_sub):
    c = CHUNK
    n_beta = N_DIR * H_A
    r1 = lax.broadcasted_iota(jnp.int32, (c, c), 0)
    c1 = lax.broadcasted_iota(jnp.int32, (c, c), 1)
    tri_lo = jnp.where(r1 >= c1, 1.0, 0.0).astype(BF16)
    tri_up = jnp.where(r1 <= c1, 1.0, 0.0).astype(BF16)
    r2 = lax.broadcasted_iota(jnp.int32, (2 * c, 2 * c), 0)
    c2 = lax.broadcasted_iota(jnp.int32, (2 * c, 2 * c), 1)
    same_head = (r2 ^ c2) < c

    def col_pair(a, j):
        col = jnp.concatenate([a[:, j:j + 1], a[:, j + 1:j + 2]], axis=0)
        return jnp.broadcast_to(col, (2 * c, c))

    def row_pair(a, j):
        return jnp.concatenate([a[j:j + 1, :], a[j + 1:j + 2, :]], axis=1)

    def twice(a):
        return jnp.concatenate([a, a], axis=1)

    n = 0
    for ci in range(n_sub):
        rows_ci = slice(ci * c, (ci + 1) * c)
        gbr = gbr_ref[:, rows_ci]
        sub = lax.broadcasted_iota(jnp.int32, gbr.shape, 0)
        gc_r = jnp.where(sub < n_beta + H_A, _dot3_left(gbr, tri_up), _dot3_left(gbr, tri_lo))
        total = jnp.broadcast_to(jnp.sum(gbr, axis=1, keepdims=True), gbr.shape)
        gl_ref[ci] = jnp.exp(total)[n_beta:, :]
        pad = jnp.zeros((c - 2 * N_SMALL, c), F32)
        cols_c = jnp.concatenate([gbr, gc_r, pad], axis=0).T
        gbc = cols_c[:, 0:N_SMALL]
        gc_c = cols_c[:, N_SMALL:2 * N_SMALL]

        q = q_ref[rows_ci, :]
        k = k_ref[rows_ci, :]
        v = v_ref[rows_ci, :]
        for hp in range(H_A // 2):
            h0 = 2 * hp

            def stack(a):
                return jnp.concatenate([a[:, h0 * DK:(h0 + 1) * DK],
                                        a[:, (h0 + 1) * DK:(h0 + 2) * DK]], axis=0)

            q2b, k2b, v2b = stack(q), stack(k), stack(v)
            kk = _dot_nt(k2b, k2b)
            qk = _dot_nt(q2b, k2b)
            for d in range(N_DIR):
                jb = d * H_A + h0
                jg = n_beta + jb
                gcb = col_pair(gc_c, jg)
                beta = col_pair(gbc, jb)
                tot = jnp.concatenate([jnp.broadcast_to(total[jg:jg + 1, :], (c, c)),
                                       jnp.broadcast_to(total[jg + 1:jg + 2, :], (c, c))], axis=0)
                eg = jnp.exp(gcb)
                er = jnp.exp(tot - gcb)
                tri = (r2 >= c2) if d == 0 else (r2 <= c2)
                decay = jnp.exp(jnp.where(same_head & tri, twice(gcb) - row_pair(gc_r, jg), NEG_BIG))
                attn = qk * decay
                m_scr[n] = jnp.where(r2 != c2, (twice(beta) * kk) * decay, 0.0)
                rhs_scr[n] = jnp.concatenate([v2b * beta.astype(BF16),
                                              k2b * (beta * eg).astype(BF16)], axis=1)
                qg = q2b * eg.astype(BF16)
                kg = k2b * er.astype(BF16)
                for hh in range(2):
                    rows = slice(hh * c, (hh + 1) * c)
                    cols = slice((h0 + hh) * DK, (h0 + hh + 1) * DK)
                    qg_ref[d, rows_ci, cols] = qg[rows]
                    kg_ref[d, rows_ci, cols] = kg[rows]
                    at_ref[d, rows_ci, cols] = attn[rows, hh * c:(hh + 1) * c].astype(BF16)
                n += 1
                yield


def _chunk_solve(m_scr, rhs_scr, gsel_ref, u_ref, w_ref, *, n_sub):
    c = CHUNK
    chains = [{"ci": ci, "h0": 2 * hp, "d": d}
              for ci in range(n_sub) for hp in range(H_A // 2) for d in range(N_DIR)]
    for n, ch in enumerate(chains):
        ch["m"] = m_scr[n]
    r2 = lax.broadcasted_iota(jnp.int32, (2 * c, 2 * c), 0)
    c2 = lax.broadcasted_iota(jnp.int32, (2 * c, 2 * c), 1)
    x2 = r2 ^ c2

    sub8 = SUBLANES
    per_half = c // sub8
    lane8 = lax.broadcasted_iota(jnp.int32, (sub8, c), 1)
    row8 = lax.broadcasted_iota(jnp.int32, (sub8, c), 0)
    in_blk = [(lane8 >> 3) == i for i in range(per_half)]
    zeros8 = jnp.zeros((sub8, c), F32)
    packed = []
    for ch in chains:
        halves = []
        for half in range(2):
            pm = zeros8
            for i in range(per_half):
                blk = half * per_half + i
                pm = jnp.where(in_blk[i], ch["m"][blk * sub8:(blk + 1) * sub8,
                                                  half * c:(half + 1) * c], pm)
            halves.append(pm)
        packed.append(jnp.concatenate(halves, axis=1))
    stacked = jnp.concatenate(packed, axis=0).astype(BF16)
    yield
    spread = [_dot(stacked, gsel_ref[k]) for k in range(sub8)]
    yield
    eye8 = jnp.where((lane8 & (sub8 - 1)) == row8, 1.0, 0.0).astype(F32)
    for n, ch in enumerate(chains):
        rows_n = slice(n * sub8, (n + 1) * sub8)
        tp = jnp.concatenate([eye8, eye8], axis=1)
        order = range(sub8 - 1) if ch["d"] == 0 else range(sub8 - 1, 0, -1)
        for k in order:
            tp = tp - spread[k][rows_n] * tp[k:k + 1, :]
        row_groups = []
        for half in range(2):
            tph = tp[:, half * c:(half + 1) * c]
            for i in range(per_half):
                own = jnp.where(in_blk[i], tph, 0.0)
                row_groups.append(jnp.concatenate([own, zeros8] if half == 0 else [zeros8, own],
                                                  axis=1))
        ch["t"] = jnp.concatenate(row_groups, axis=0)
    yield

    def pick(a, b, parity):
        return jnp.concatenate([a[i * b:(i + 1) * b] for i in range(a.shape[0] // b)
                                if i % 2 == parity], axis=0)

    def weave(sel, rest, b, parity):
        blocks = []
        for i in range(2 * sel.shape[0] // b):
            src = sel if i % 2 == parity else rest
            blocks.append(src[(i // 2) * b:(i // 2 + 1) * b])
        return jnp.concatenate(blocks, axis=0)

    par = [1 - ch["d"] for ch in chains]
    for lvl in range(3, 7):
        b = 1 << lvl
        xs = []
        for ch, p in zip(chains, par):
            off = jnp.where((pick(x2, b, p) >> lvl) == 1, pick(ch["m"], b, p), 0.0)
            xs.append(_dot(off.astype(BF16), ch["t"].astype(BF16)))
        yield
        for ch, x, p in zip(chains, xs, par):
            x_full = weave(x, jnp.zeros_like(x), b, p).astype(BF16)
            t_sel = pick(ch["t"], b, p)
            t_new = t_sel - _dot(t_sel.astype(BF16), x_full)
            ch["t"] = weave(t_new, pick(ch["t"], b, 1 - p), b, p)
        yield

    for n, ch in enumerate(chains):
        uw = _dot(ch["t"].astype(BF16), rhs_scr[n])
        rows_ci = slice(ch["ci"] * c, (ch["ci"] + 1) * c)
        for hh in range(2):
            rows = slice(hh * c, (hh + 1) * c)
            cols = slice((ch["h0"] + hh) * DK, (ch["h0"] + hh + 1) * DK)
            u_ref[ch["d"], rows_ci, cols] = uw[rows, 0:DV].astype(BF16)
            w_ref[ch["d"], rows_ci, cols] = uw[rows, DV:2 * DV].astype(BF16)
        if n % 2:
            yield


def _chunk_local_kernel(q_ref, k_ref, v_ref, gbr_ref, gsel_ref,
                        u_ref, w_ref, qg_ref, kg_ref, at_ref, gl_ref,
                        m_a, rhs_a, m_b, rhs_b, *, n_sub):
    setup = functools.partial(_chunk_setup, q_ref, k_ref, v_ref, gbr_ref,
                              qg_ref, kg_ref, at_ref, gl_ref, n_sub=n_sub)
    solve = functools.partial(_chunk_solve, gsel_ref=gsel_ref, u_ref=u_ref, w_ref=w_ref,
                              n_sub=n_sub)
    j = pl.program_id(0)

    @pl.when(j == 0)
    def _():
        u_ref[...] = jnp.zeros(u_ref.shape, u_ref.dtype)
        w_ref[...] = jnp.zeros(w_ref.shape, w_ref.dtype)
        _interleave(setup(m_a, rhs_a))

    @pl.when(j % 2 == 1)
    def _():
        _interleave(solve(m_a, rhs_a), setup(m_b, rhs_b))

    @pl.when((j > 0) & (j % 2 == 0))
    def _():
        _interleave(solve(m_b, rhs_b), setup(m_a, rhs_a))


def _chunk_local(q, k, v, gbr, n_sub=4):
    n_rows = q.shape[0]
    n_chunks = n_rows // CHUNK
    rows = n_sub * CHUNK
    n_blocks = n_chunks // n_sub
    n_sys = n_sub * (H_A // 2) * N_DIR
    cur = lambda j: jnp.minimum(j, n_blocks - 1)
    prev = lambda j: jnp.maximum(j - 1, 0)
    wide = pl.BlockSpec((rows, A_WIDTH), lambda j: (cur(j), 0))
    out_cur = pl.BlockSpec((N_DIR, rows, A_WIDTH), lambda j: (0, cur(j), 0))
    out_prev = pl.BlockSpec((N_DIR, rows, A_WIDTH), lambda j: (0, prev(j), 0))
    big = lambda dt: jax.ShapeDtypeStruct((N_DIR, n_rows, A_WIDTH), dt)
    idx = jnp.arange(2 * CHUNK)
    gsel = ((idx[None, :, None] // SUBLANES == idx[None, None, :] // SUBLANES)
            & (idx[None, :, None] % SUBLANES == jnp.arange(SUBLANES)[:, None, None])).astype(BF16)
    return pl.pallas_call(
        functools.partial(_chunk_local_kernel, n_sub=n_sub),
        grid=(n_blocks + 1,),
        in_specs=[wide, wide, wide,
                  pl.BlockSpec((N_SMALL, rows), lambda j: (0, cur(j))),
                  _const_spec((SUBLANES, 2 * CHUNK, 2 * CHUNK))],
        out_specs=[out_prev, out_prev, out_cur, out_cur, out_cur,
                   pl.BlockSpec((n_sub, N_DIR * H_A, CHUNK), lambda j: (cur(j), 0, 0))],
        out_shape=[big(BF16)] * 5 + [jax.ShapeDtypeStruct((n_chunks, N_DIR * H_A, CHUNK), F32)],
        scratch_shapes=[pltpu.VMEM((n_sys, 2 * CHUNK, 2 * CHUNK), F32),
                        pltpu.VMEM((n_sys, 2 * CHUNK, 2 * CHUNK), BF16),
                        pltpu.VMEM((n_sys, 2 * CHUNK, 2 * CHUNK), F32),
                        pltpu.VMEM((n_sys, 2 * CHUNK, 2 * CHUNK), BF16)],
        compiler_params=_params(1, ("arbitrary",)), name="chunk_local",
    )(q, k, v, gbr, gsel)


def _scan_kernel(*refs, n_chunks, zero_init, write_final):
    fwd = refs[0:6]
    bwd = refs[6:12]
    refs = refs[12:]
    if not zero_init:
        s0_ref = refs[0]
        refs = refs[1:]
    of_ref, ob_ref = refs[0:2]
    refs = refs[2:]
    if write_final:
        sfin_ref = refs[0]
        refs = refs[1:]
    s_scr = refs[0]
    i = pl.program_id(1)

    @pl.when(i == 0)
    def _():
        if zero_init:
            s_scr[...] = jnp.zeros(s_scr.shape, F32)
        else:
            s_scr[...] = s0_ref[...]

    problems = [(s, d, h) for s in range(SEQ_GROUP) for d in range(N_DIR) for h in range(H_A)]
    c = CHUNK
    stage1 = []
    for s, d, h in problems:
        u_ref, w_ref, qg_ref, _, _, _ = fwd if d == 0 else bwd
        cols = slice(h * DV, (h + 1) * DV)
        sb = s_scr[s, d, h].astype(BF16)
        lhs = jnp.concatenate([w_ref[0, s, :, cols], qg_ref[0, s, :, cols]], axis=0)
        stage1.append(_dot(lhs, sb))
    for (s, d, h), ws_qs in zip(problems, stage1):
        u_ref, _, _, kg_ref, at_ref, gl_ref = fwd if d == 0 else bwd
        o_ref = of_ref if d == 0 else ob_ref
        cols = slice(h * DV, (h + 1) * DV)
        vb = (u_ref[0, s, :, cols].astype(F32) - ws_qs[0:c]).astype(BF16)
        o_ref[s, :, cols] = (ws_qs[c:2 * c] + _dot(at_ref[0, s, :, cols], vb)).astype(BF16)
        gl = gl_ref[s, 0, d * H_A + h:d * H_A + h + 1, :]
        s_scr[s, d, h] = s_scr[s, d, h] * gl + _dot_tn(kg_ref[0, s, :, cols], vb)

    if write_final:
        @pl.when(i == n_chunks - 1)
        def _():
            sfin_ref[...] = s_scr[...]


def _scan(local, s0, n_seq, seq_len, write_final):
    u, w, qg, kg, at, gl = local
    n_chunks = seq_len // CHUNK
    zero_init = s0 is None
    shp4 = (N_DIR, n_seq, seq_len, A_WIDTH)
    arrs = [a.reshape(shp4) for a in (u, w, qg, kg, at)]
    gl4 = gl.reshape(n_seq, n_chunks, N_DIR * H_A, CHUNK)
    sg = SEQ_GROUP

    def specs(d, chunk_of):
        big = pl.BlockSpec((1, sg, CHUNK, A_WIDTH), lambda g, i: (d, g, chunk_of(i), 0))
        return [big] * 5 + [pl.BlockSpec((sg, 1, N_DIR * H_A, CHUNK),
                                         lambda g, i: (g, chunk_of(i), 0, 0))]

    rev = lambda i: n_chunks - 1 - i
    in_specs = specs(0, lambda i: i) + specs(1, rev)
    args = arrs + [gl4] + arrs + [gl4]
    state_spec = pl.BlockSpec((sg, N_DIR, H_A, DK, DV), lambda g, i: (g, 0, 0, 0, 0))
    if not zero_init:
        in_specs.append(state_spec)
        args.append(s0)
    o_shape = jax.ShapeDtypeStruct((n_seq, seq_len, A_WIDTH), BF16)
    out_specs = [pl.BlockSpec((sg, CHUNK, A_WIDTH), lambda g, i: (g, i, 0)),
                 pl.BlockSpec((sg, CHUNK, A_WIDTH), lambda g, i: (g, rev(i), 0))]
    out_shape = [o_shape, o_shape]
    if write_final:
        out_specs.append(state_spec)
        out_shape.append(jax.ShapeDtypeStruct((n_seq, N_DIR, H_A, DK, DV), F32))
    return pl.pallas_call(
        functools.partial(_scan_kernel, n_chunks=n_chunks, zero_init=zero_init,
                          write_final=write_final),
        grid=(n_seq // sg, n_chunks), in_specs=in_specs, out_specs=out_specs, out_shape=out_shape,
        scratch_shapes=[pltpu.VMEM((sg, N_DIR, H_A, DK, DV), F32)],
        compiler_params=_params(2, ("parallel", "arbitrary")), name="delta_scan",
    )(*args)


def _delta_stage(*refs, tm, seq_len, n_tiles, has_pe):
    xp, xm, xn = refs[:3]
    refs = refs[3:]
    if has_pe:
        pe_refs = refs[:3]
        refs = refs[3:]
    mix_refs = refs[:12]
    og_ref, wout_ref, mod_ref, g2_ref, hb_scr, xr_scr = refs[12:]
    tile = jnp.minimum(pl.program_id(0), n_tiles - 1)
    first = tile * tm
    vp = jnp.where(first % seq_len != 0, 1.0, 0.0).astype(F32)
    vn = jnp.where((first + tm) % seq_len != 0, 1.0, 0.0).astype(F32)
    mod = mod_ref[0]
    gate1 = mod[:, 2 * D_MODEL:3 * D_MODEL]
    shift2 = mod[:, 3 * D_MODEL:4 * D_MODEL]
    scale2 = mod[:, 4 * D_MODEL:5 * D_MODEL]
    og = og_ref[...]

    def operands(which):
        vals = []
        for a in range(4):
            r = mix_refs[3 * a + which][...].astype(F32)
            if which == 0:
                r = r[HALO:2 * HALO]
            elif which == 2:
                r = r[0:HALO]
            vals.append(r)
        return vals

    pieces = [_delta_mixer_rows(*operands(0), og)]
    half = tm // 2
    o_f, o_b, gate, yb = operands(1)
    for lo in (0, half):
        rows = slice(lo, lo + half)
        pieces.append(_delta_mixer_rows(o_f[rows], o_b[rows], gate[rows], yb[rows], og))
        yield
    pieces.append(_delta_mixer_rows(*operands(2), og))
    y_ext = _dot(jnp.concatenate(pieces, axis=0).astype(BF16), wout_ref[...])
    yield
    x_ext = jnp.concatenate([xp[...], xm[...], xn[...]], axis=0)
    if has_pe:
        x_ext = x_ext + jnp.concatenate([r[...] for r in pe_refs], axis=0)
    x1 = x_ext + gate1 * y_ext
    h2 = _norm_mod(x1, g2_ref[...], scale2, shift2)
    zeros = jnp.zeros((HALO, D_MODEL), F32)
    hb_scr[...] = jnp.concatenate([zeros, h2[0:HALO] * vp, h2[HALO:HALO + tm],
                                   h2[HALO + tm:] * vn, zeros], axis=0).astype(BF16)
    xr_scr[...] = x1[HALO:HALO + tm]
    yield


def _delta_mixer_rows(o_f, o_b, gate, yb, og):
    o = o_f + o_b
    sg = _silu(gate)
    parts = []
    for h in range(H_A):
        cols = slice(h * DV, (h + 1) * DV)
        oh = o[:, cols]
        ms = jnp.mean(oh * oh, axis=-1, keepdims=True)
        parts.append((oh * lax.rsqrt(ms + EPS)) * og * sg[:, cols])
    return jnp.concatenate(parts + [yb], axis=1)


def _delta_ffn_kernel(*refs, tm, seq_len, n_tiles, has_pe):
    n_stage = 3 + (3 if has_pe else 0) + 12 + 2
    stage_refs = refs[:n_stage]
    mod_cur, mod_prev, g2_ref, wg_ref, wu_ref, wd_ref, cf_ref, out_ref = refs[n_stage:n_stage + 8]
    hb_a, xr_a, hb_b, xr_b = refs[n_stage + 8:]
    stage = lambda hb, xr: _delta_stage(*stage_refs, mod_cur, g2_ref, hb, xr, tm=tm,
                                        seq_len=seq_len, n_tiles=n_tiles, has_pe=has_pe)
    finish = functools.partial(_ffn_stage, mod_ref=mod_prev, wg_ref=wg_ref, wu_ref=wu_ref,
                               wd_ref=wd_ref, cf_ref=cf_ref, fg_ref=None, out_ref=out_ref,
                               tm=tm, seq_len=seq_len)
    j = pl.program_id(0)

    @pl.when(j == 0)
    def _():
        out_ref[...] = jnp.zeros(out_ref.shape, out_ref.dtype)
        _interleave(stage(hb_a, xr_a))

    @pl.when(j % 2 == 1)
    def _():
        _interleave(finish(hb_a, xr_a), stage(hb_b, xr_b))

    @pl.when((j > 0) & (j % 2 == 0))
    def _():
        _interleave(finish(hb_b, xr_b), stage(hb_a, xr_a))


def _delta_ffn(x, pe, o_f, o_b, gate, yb, og, w_out, mod3, row_base, row_stride, seq_len, tm,
               g2, wts, layer):
    n_rows = x.shape[0]
    n_tiles = n_rows // tm
    has_pe = pe is not None
    wide = 2 * HALO
    cur = lambda j: jnp.minimum(j, n_tiles - 1)
    prev = lambda j: jnp.maximum(j - 1, 0)
    mod_row = lambda t: row_base + ((t * tm) // seq_len) * row_stride
    of_layer = lambda *shape: pl.BlockSpec((None,) + shape, lambda j: (layer, 0, 0))

    def halo_specs(width, halo, tile_of, n_total):
        per = tm // halo
        last = n_total // halo - 1
        return [pl.BlockSpec((halo, width), lambda j: (jnp.maximum(tile_of(j) * per - 1, 0), 0)),
                pl.BlockSpec((tm, width), lambda j: (tile_of(j), 0)),
                pl.BlockSpec((halo, width), lambda j: (jnp.minimum((tile_of(j) + 1) * per, last), 0))]

    in_specs = halo_specs(D_MODEL, HALO, cur, n_rows)
    args = [x, x, x]
    if has_pe:
        tiles = seq_len // tm
        in_specs += halo_specs(D_MODEL, HALO, lambda j: cur(j) % tiles, seq_len)
        args += [pe, pe, pe]
    for arr in (o_f, o_b, gate, yb):
        in_specs += halo_specs(A_WIDTH, wide, cur, n_rows)
        args += [arr, arr, arr]
    in_specs += [_const_spec((1, DV)), _const_spec((D_MODEL, D_MODEL)),
                 pl.BlockSpec((1, 1, N_MOD * D_MODEL), lambda j: (mod_row(cur(j)), 0, 0)),
                 pl.BlockSpec((1, 1, N_MOD * D_MODEL), lambda j: (mod_row(prev(j)), 0, 0)),
                 _const_spec((1, D_MODEL)),
                 of_layer(D_MODEL, D_FF), of_layer(D_MODEL, D_FF), of_layer(D_FF, D_MODEL),
                 of_layer(3, D_FF)]
    args += [og, w_out, mod3, mod3, g2, wts["w_gate"], wts["w_up"], wts["w_down"], wts["conv"]]
    return pl.pallas_call(
        functools.partial(_delta_ffn_kernel, tm=tm, seq_len=seq_len, n_tiles=n_tiles,
                          has_pe=has_pe),
        grid=(n_tiles + 1,), in_specs=in_specs,
        out_specs=pl.BlockSpec((tm, D_MODEL), lambda j: (prev(j), 0)),
        out_shape=jax.ShapeDtypeStruct((n_rows, D_MODEL), F32),
        scratch_shapes=[pltpu.VMEM((tm + 2 * wide, D_MODEL), BF16), pltpu.VMEM((tm, D_MODEL), F32),
                        pltpu.VMEM((tm + 2 * wide, D_MODEL), BF16), pltpu.VMEM((tm, D_MODEL), F32)],
        compiler_params=_params(1, ("arbitrary",)), name="delta_ffn",
    )(*args)


def _pool_group(h_ext, n_out, pos, seq_len, wp_ref, gi):
    win = POOL_WINDOWS[gi]
    cols = slice(gi * POOL_GROUP, (gi + 1) * POOL_GROUP)
    s = h_ext[:, cols]
    width = 1
    while width < win:
        s = s + _shift_rows(s, width)
        width *= 2
    start = HALO - win // 2
    if start:
        s = _shift_rows(s, start)
    s = s[0:n_out]
    lo = jnp.maximum(pos - win // 2, 0)
    hi = jnp.minimum(pos - win // 2 + win, seq_len)
    count = jnp.maximum(hi - lo, 1).astype(F32)
    mixed = s / jnp.concatenate([count, count], axis=1) - h_ext[HALO:HALO + n_out, cols]
    return _dot(mixed.astype(BF16), wp_ref[gi])


def _pool_stage(xp, xm, xn, mod_ref, g1_ref, wp_ref, ps_ref, g2_ref, hb_scr, xr_scr,
                *, tm, seq_len, n_tiles):
    wide = 2 * HALO
    tile = jnp.minimum(pl.program_id(0), n_tiles - 1)
    first = tile * tm
    vp = jnp.where(first % seq_len != 0, 1.0, 0.0).astype(F32)
    vn = jnp.where((first + tm) % seq_len != 0, 1.0, 0.0).astype(F32)
    mod = mod_ref[0]
    shift1, scale1, gate1 = (mod[:, k * D_MODEL:(k + 1) * D_MODEL] for k in range(3))
    shift2, scale2 = (mod[:, k * D_MODEL:(k + 1) * D_MODEL] for k in (3, 4))
    g1 = g1_ref[...]
    g2 = g2_ref[...]
    lanes = POOL_GROUP // 2
    if tm <= seq_len:
        n_out = tm + 2 * HALO
        segments = [(jnp.concatenate([_norm_mod(xp[...], g1, scale1, shift1) * vp,
                                      _norm_mod(xm[...], g1, scale1, shift1),
                                      _norm_mod(xn[...], g1, scale1, shift1) * vn], axis=0),
                     (first - HALO + lax.broadcasted_iota(jnp.int32, (n_out, lanes), 0)) % seq_len)]
        x_res = jnp.concatenate([xp[HALO:, :], xm[...], xn[0:HALO, :]], axis=0)
    else:
        n_out = seq_len
        x_res = xm[...]
        h1 = _norm_mod(x_res, g1, scale1, shift1)
        zeros = jnp.zeros((HALO, D_MODEL), F32)
        pos = lax.broadcasted_iota(jnp.int32, (seq_len, lanes), 0)
        segments = [(jnp.concatenate([zeros, h1[s * seq_len:(s + 1) * seq_len], zeros], axis=0), pos)
                    for s in range(tm // seq_len)]
    yield
    ys = []
    for h_ext, pos in segments:
        groups = []
        for gi in range(N_POOL_GROUPS):
            groups.append(_pool_group(h_ext, n_out, pos, seq_len, wp_ref, gi))
            yield
        ys.append(jnp.concatenate(groups, axis=1))
    x3 = x_res + gate1 * (jnp.concatenate(ys, axis=0) * ps_ref[...])
    h2 = _norm_mod(x3, g2, scale2, shift2)
    if tm <= seq_len:
        zeros = jnp.zeros((HALO, D_MODEL), F32)
        hb_scr[...] = jnp.concatenate([zeros, h2[0:HALO] * vp, h2[HALO:HALO + tm],
                                       h2[HALO + tm:] * vn, zeros], axis=0).astype(BF16)
        xr_scr[...] = x3[HALO:HALO + tm]
    else:
        zeros_w = jnp.zeros((wide, D_MODEL), F32)
        hb_scr[...] = jnp.concatenate([zeros_w, h2, zeros_w], axis=0).astype(BF16)
        xr_scr[...] = x3
    yield


def _ffn_stage(hb_scr, xr_scr, mod_ref, wg_ref, wu_ref, wd_ref, cf_ref, fg_ref, out_ref,
               *, tm, seq_len):
    wide = 2 * HALO
    gate2 = mod_ref[0][:, 5 * D_MODEL:6 * D_MODEL]
    hb_ext = hb_scr[...]
    hb = hb_scr[wide:wide + tm, :]
    edge_masks = _seq_edge_masks(tm, seq_len, FF_CHUNK)
    acts = []
    for j in range(N_FF_CHUNKS):
        cols = slice(j * FF_CHUNK, (j + 1) * FF_CHUNK)
        u = _dwconv3_ext(_dot(hb_ext, wg_ref[:, cols]), cf_ref[:, cols], tm, edge_masks, wide)
        acts.append((_silu(u) * _dot(hb, wu_ref[:, cols])).astype(BF16))
        yield
    y = xr_scr[...] + gate2 * _dot(jnp.concatenate(acts, axis=1), wd_ref[...])
    if fg_ref is not None:
        ms = jnp.mean(y * y, axis=-1, keepdims=True)
        y = (y * lax.rsqrt(ms + EPS)) * fg_ref[...]
    out_ref[...] = y
    yield


def _pool_ffn_kernel(xp, xm, xn, mod_cur, mod_prev, g1_ref, wp_ref, ps_ref, g2_ref,
                     wg_ref, wu_ref, wd_ref, cf_ref, fg_ref, out_ref,
                     hb_a, xr_a, hb_b, xr_b, *, tm, seq_len, n_tiles):
    stage = functools.partial(_pool_stage, xp, xm, xn, mod_cur, g1_ref, wp_ref, ps_ref, g2_ref,
                              tm=tm, seq_len=seq_len, n_tiles=n_tiles)
    finish = functools.partial(_ffn_stage, mod_ref=mod_prev, wg_ref=wg_ref, wu_ref=wu_ref,
                               wd_ref=wd_ref, cf_ref=cf_ref, fg_ref=fg_ref, out_ref=out_ref,
                               tm=tm, seq_len=seq_len)
    j = pl.program_id(0)

    @pl.when(j == 0)
    def _():
        out_ref[...] = jnp.zeros(out_ref.shape, out_ref.dtype)
        _interleave(stage(hb_a, xr_a))

    @pl.when(j % 2 == 1)
    def _():
        _interleave(finish(hb_a, xr_a), stage(hb_b, xr_b))

    @pl.when((j > 0) & (j % 2 == 0))
    def _():
        _interleave(finish(hb_b, xr_b), stage(hb_a, xr_a))


def _pool_ffn(x, mod3, row_base, row_stride, seq_len, tm, g1, w_pool, pool_scale, g2, wts, layer,
              final_g):
    n_rows = x.shape[0]
    n_tiles = n_rows // tm
    wide = 2 * HALO
    per = tm // wide
    last = n_rows // wide - 1
    cur = lambda j: jnp.minimum(j, n_tiles - 1)
    prev = lambda j: jnp.maximum(j - 1, 0)
    mod_row = lambda t: row_base + ((t * tm) // seq_len) * row_stride
    of_layer = lambda *shape: pl.BlockSpec((None,) + shape, lambda j: (layer, 0, 0))
    in_specs = [
        pl.BlockSpec((wide, D_MODEL), lambda j: (jnp.maximum(cur(j) * per - 1, 0), 0)),
        pl.BlockSpec((tm, D_MODEL), lambda j: (cur(j), 0)),
        pl.BlockSpec((wide, D_MODEL), lambda j: (jnp.minimum((cur(j) + 1) * per, last), 0)),
        pl.BlockSpec((1, 1, N_MOD * D_MODEL), lambda j: (mod_row(cur(j)), 0, 0)),
        pl.BlockSpec((1, 1, N_MOD * D_MODEL), lambda j: (mod_row(prev(j)), 0, 0)),
        _const_spec((1, D_MODEL)), _const_spec((N_POOL_GROUPS, POOL_GROUP, POOL_GROUP)),
        _const_spec((1, D_MODEL)), _const_spec((1, D_MODEL)),
        of_layer(D_MODEL, D_FF), of_layer(D_MODEL, D_FF), of_layer(D_FF, D_MODEL),
        of_layer(3, D_FF), _const_spec((1, D_MODEL))]
    return pl.pallas_call(
        functools.partial(_pool_ffn_kernel, tm=tm, seq_len=seq_len, n_tiles=n_tiles),
        grid=(n_tiles + 1,), in_specs=in_specs,
        out_specs=pl.BlockSpec((tm, D_MODEL), lambda j: (prev(j), 0)),
        out_shape=jax.ShapeDtypeStruct((n_rows, D_MODEL), F32),
        scratch_shapes=[pltpu.VMEM((tm + 2 * wide, D_MODEL), BF16), pltpu.VMEM((tm, D_MODEL), F32),
                        pltpu.VMEM((tm + 2 * wide, D_MODEL), BF16), pltpu.VMEM((tm, D_MODEL), F32)],
        compiler_params=_params(1, ("arbitrary",)), name="pool_ffn",
    )(x, x, x, mod3, mod3, g1, w_pool, pool_scale, g2, wts["w_gate"], wts["w_up"], wts["w_down"],
      wts["conv"], final_g)


def _sincos_2d(rows, cols, d):
    quarter = d // 4
    omega = 1.0 / (10000.0 ** (jnp.arange(quarter, dtype=F32) / quarter))
    er = jnp.arange(rows, dtype=F32)[:, None] * omega[None, :]
    ec = jnp.arange(cols, dtype=F32)[:, None] * omega[None, :]
    er = jnp.concatenate([jnp.sin(er), jnp.cos(er)], axis=-1)
    ec = jnp.concatenate([jnp.sin(ec), jnp.cos(ec)], axis=-1)
    pe = jnp.concatenate([jnp.broadcast_to(er[:, None, :], (rows, cols, d // 2)),
                          jnp.broadcast_to(ec[None, :, :], (rows, cols, d // 2))], axis=-1)
    return pe.reshape(rows * cols, d)


def _even_layer_weights(w_in, conv_qkv, a_log, dt_bias, conv_b, w_out):
    small = w_in[:, GATE_END:ALPHA_END]
    zeros = jnp.zeros((N_DIR * H_A,), F32)
    ea = jnp.concatenate([zeros, jnp.exp(a_log.astype(F32)).reshape(-1)])
    dtb = jnp.concatenate([zeros, dt_bias.astype(F32).reshape(-1)])
    return {
        "w_main": jnp.concatenate([w_in[:, :GATE_END], w_in[:, ALPHA_END:IN_AB]],
                                  axis=1).astype(BF16),
        "w_small": jnp.pad(small, ((0, 0), (0, CHUNK - N_SMALL))),
        "conv_qkv": conv_qkv, "conv_b": conv_b,
        "p_row": jnp.stack([ea, dtb], axis=1),
        "w_out": w_out.astype(BF16),
    }


def _ffn_weights(w_gate, w_up, conv, w_down):
    return {"w_gate": w_gate.astype(BF16), "w_up": w_up.astype(BF16),
            "w_down": w_down.astype(BF16), "conv": conv}


def _trunk(x, pe, mod, row_base, row_stride, n_seq, seq_len, tm, tm_ffn, s0, write_final, p):
    row = lambda a: a.reshape(1, -1)
    finals = None
    depth = mod.shape[0]
    for layer in range(depth):
        mod3 = mod[layer][:, None, :]
        place = (mod3, row_base, row_stride, seq_len, tm)
        place_ffn = (mod3, row_base, row_stride, seq_len, tm_ffn // 2)
        if layer % 2 == 0:
            e = layer // 2
            wts = p["even"][e]
            q, k, v, gate, gbr, yb = _inproj(x, pe, *place, row(p["norm_mix_g"][layer]), wts)
            local = _chunk_local(q, k, v, gbr)
            res = _scan(local, None if s0 is None else s0[:, e], n_seq, seq_len, write_final)
            o_f = res[0].reshape(-1, A_WIDTH)
            o_b = res[1].reshape(-1, A_WIDTH)
            if write_final:
                finals = res[2]
            delta = (pe, o_f, o_b, gate, yb, row(p["o_norm_g"][e]), wts["w_out"])
            x = _delta_ffn(x, *delta, *place_ffn, row(p["norm_ffn_g"][layer]), p["ffn"], layer)
        else:
            o = layer // 2
            assert layer == depth - 1
            x = _pool_ffn(x, *place_ffn, row(p["norm_mix_g"][layer]), p["w_pool"][o],
                          row(p["pool_scale"][o]), row(p["norm_ffn_g"][layer]), p["ffn"], layer,
                          row(p["final_norm_g"]))
    return x, finals


def kernel(x_prompt, x_sample, state_delta, c, c_ctx, norm_mix_g, norm_ffn_g, w_ada, b_ada,
           w_in_ab, conv_qkv, a_log, dt_bias, o_norm_g, conv_b, w_out_ab, w_pool, pool_scale,
           w_ffn_gate, w_ffn_up, ffn_conv, w_ffn_down, final_norm_g):
    batch, seq, d = x_prompt.shape
    dec_batch, dec_seq, _ = x_sample.shape
    depth = w_ada.shape[0]
    n_even = w_in_ab.shape[0]
    assert d == D_MODEL and depth == 2 and n_even == 1

    p = {
        "norm_mix_g": norm_mix_g, "norm_ffn_g": norm_ffn_g, "o_norm_g": o_norm_g,
        "pool_scale": pool_scale, "final_norm_g": final_norm_g,
        "w_pool": w_pool.astype(BF16),
        "even": [_even_layer_weights(w_in_ab[e], conv_qkv[e], a_log[e], dt_bias[e], conv_b[e],
                                     w_out_ab[e]) for e in range(n_even)],
        "ffn": _ffn_weights(w_ffn_gate, w_ffn_up, ffn_conv, w_ffn_down),
    }

    n_cond = 1 + dec_batch
    pad = (-n_cond) % HALO
    cond = jnp.concatenate([c_ctx[None, :], c, jnp.zeros((pad, d), F32)], axis=0)
    mod = _modulation(cond, w_ada, b_ada)

    y_prompt, finals = _trunk(x_prompt.reshape(batch * seq, d), None, mod, 0, 0, batch, seq,
                              min(seq, 256), 1024, None, True, p)
    pe = _sincos_2d(dec_seq // GRID_W, GRID_W, d)
    y_sample, _ = _trunk(x_sample.reshape(dec_batch * dec_seq, d), pe, mod, 1, 1, dec_batch,
                         dec_seq, 512, 1024, state_delta, False, p)
    return (y_prompt.reshape(batch, seq, d), y_sample.reshape(dec_batch, dec_seq, d),
            finals[:, None])
```

```python
import functools

import jax
import jax.numpy as jnp
from jax import lax
from jax.experimental import pallas as pl
from jax.experimental.pallas import tpu as pltpu

F32 = jnp.float32
BF16 = jnp.bfloat16

D_MODEL = 1024
GRID_W = 64
A_WIDTH = 512
B_WIDTH = 512
DK = 128
DV = 128
H_A = 4
N_DIR = 2
N_POOL_GROUPS = 4
POOL_GROUP = D_MODEL // N_POOL_GROUPS
POOL_WINDOWS = (2, 4, 8, 16)
D_FF = 2816
N_MOD = 6
EPS = 1e-6
QKV_END = 3 * A_WIDTH
GATE_END = QKV_END + A_WIDTH
BETA_END = GATE_END + N_DIR * H_A
ALPHA_END = BETA_END + N_DIR * H_A
BG_END = ALPHA_END + B_WIDTH
CG_END = BG_END + B_WIDTH
IN_AB = CG_END + B_WIDTH

SUBLANES = 8
HALO = SUBLANES
CHUNK = 128
FF_CHUNK = 256
N_FF_CHUNKS = D_FF // FF_CHUNK
N_SMALL = 2 * N_DIR * H_A
SEQ_GROUP = 4
VMEM_LIMIT = 56 * 1024 * 1024
NEG_BIG = -1e30


def _sigmoid(x):
    return 1.0 / (1.0 + jnp.exp(-x))


def _silu(x):
    return x * _sigmoid(x)


def _softplus(x):
    return jnp.maximum(x, 0.0) + jnp.log1p(jnp.exp(-jnp.abs(x)))


def _dot(a, b):
    return jnp.dot(a, b, preferred_element_type=F32)


def _dot_nt(a, b):
    return lax.dot_general(a, b, (((1,), (1,)), ((), ())), preferred_element_type=F32)


def _dot_tn(a, b):
    return lax.dot_general(a, b, (((0,), (0,)), ((), ())), preferred_element_type=F32)


def _split3(a):
    a1 = a.astype(BF16)
    r1 = a - a1.astype(F32)
    a2 = r1.astype(BF16)
    a3 = (r1 - a2.astype(F32)).astype(BF16)
    return a1, a2, a3


def _norm_mod(x, g, scale, shift):
    ms = jnp.mean(x * x, axis=-1, keepdims=True)
    return (x * lax.rsqrt(ms + EPS)) * (g * (1.0 + scale)) + shift


def _shift_rows(a, k):
    n = a.shape[0]
    return pltpu.roll(a, (-k) % n, axis=0)


def _dwconv3_ext(p_ext, w, tm, edge_masks=None, halo=HALO):
    prev = _shift_rows(p_ext, -1)[halo:halo + tm]
    cur = p_ext[halo:halo + tm]
    nxt = _shift_rows(p_ext, 1)[halo:halo + tm]
    if edge_masks is not None:
        prev = prev * edge_masks[0]
        nxt = nxt * edge_masks[1]
    return prev * w[0:1] + cur * w[1:2] + nxt * w[2:3]


def _seq_edge_masks(tm, seq_len, width):
    if tm <= seq_len:
        return None
    pos = lax.broadcasted_iota(jnp.int32, (tm, width), 0) % seq_len
    return (jnp.where(pos != 0, 1.0, 0.0).astype(F32),
            jnp.where(pos != seq_len - 1, 1.0, 0.0).astype(F32))


def _halo_valid(tm, seq_len):
    i = pl.program_id(0)
    vp = jnp.where((i * tm) % seq_len != 0, 1.0, 0.0).astype(F32)
    vn = jnp.where(((i + 1) * tm) % seq_len != 0, 1.0, 0.0).astype(F32)
    return vp, vn


def _const_spec(shape):
    nd = len(shape)
    return pl.BlockSpec(shape, lambda *_: (0,) * nd)


def _tile_specs(tm, width, n_rows, halo=HALO):
    per = tm // halo
    last = n_rows // halo - 1
    prev = pl.BlockSpec((halo, width), lambda i: (jnp.maximum(i * per - 1, 0), 0))
    main = pl.BlockSpec((tm, width), lambda i: (i, 0))
    nxt = pl.BlockSpec((halo, width), lambda i: (jnp.minimum((i + 1) * per, last), 0))
    return [prev, main, nxt]


def _pe_specs(tm, width, seq_len):
    per = tm // HALO
    tiles = seq_len // tm
    last = seq_len // HALO - 1
    prev = pl.BlockSpec((HALO, width), lambda i: (jnp.maximum((i % tiles) * per - 1, 0), 0))
    main = pl.BlockSpec((tm, width), lambda i: (i % tiles, 0))
    nxt = pl.BlockSpec((HALO, width), lambda i: (jnp.minimum((i % tiles + 1) * per, last), 0))
    return [prev, main, nxt]


def _mod_spec(tm, seq_len, row_base, row_stride):
    return pl.BlockSpec((1, 1, N_MOD * D_MODEL),
                        lambda i: (row_base + ((i * tm) // seq_len) * row_stride, 0, 0))


def _params(n_axes=1, semantics=None):
    return pltpu.CompilerParams(
        dimension_semantics=semantics or ("parallel",) * n_axes,
        vmem_limit_bytes=VMEM_LIMIT)


def _mod_kernel(c_ref, w_ref, b_ref, o_ref):
    c = c_ref[...]
    s1, s2, _ = _split3(_silu(c))
    w = w_ref[0]
    w1 = w.astype(BF16)
    w2 = (w - w1.astype(F32)).astype(BF16)
    o_ref[0] = _dot(s1, w1) + (_dot(s2, w1) + _dot(s1, w2)) + b_ref[0]


def _modulation(cond, w_ada, b_ada):
    depth, _, width = w_ada.shape
    rows = cond.shape[0]
    tn = 1536
    return pl.pallas_call(
        _mod_kernel,
        grid=(depth, width // tn),
        in_specs=[pl.BlockSpec((rows, D_MODEL), lambda l, j: (0, 0)),
                  pl.BlockSpec((1, D_MODEL, tn), lambda l, j: (l, 0, j)),
                  pl.BlockSpec((1, 1, tn), lambda l, j: (l, 0, j))],
        out_specs=pl.BlockSpec((1, rows, tn), lambda l, j: (l, 0, j)),
        out_shape=jax.ShapeDtypeStruct((depth, rows, width), F32),
        compiler_params=_params(2),
        name="modulation",
    )(cond, w_ada, b_ada.reshape(depth, 1, width))


def _inproj_kernel(*refs, tm, seq_len, has_pe):
    if has_pe:
        xp, xm, xn, pp, pm, pn = refs[:6]
        refs = refs[6:]
    else:
        xp, xm, xn = refs[:3]
        pp = pm = pn = None
        refs = refs[3:]
    (mod_ref, ng_ref, w_ref, wsr_ref, cq_ref, cb_ref, pr_ref,
     q_ref, k_ref, v_ref, gate_ref, gbr_ref, yb_ref) = refs

    vp, vn = _halo_valid(tm, seq_len)
    mod = mod_ref[0]
    shift = mod[:, 0:D_MODEL]
    scale = mod[:, D_MODEL:2 * D_MODEL]
    g = ng_ref[...]

    def prep(x_ref, pe_ref):
        x = x_ref[...]
        if has_pe:
            x = x + pe_ref[...]
        return _norm_mod(x, g, scale, shift)

    h_main = prep(xm, pm)
    h_ext = jnp.concatenate([prep(xp, pp) * vp, h_main, prep(xn, pn) * vn], axis=0)
    hb_ext = h_ext.astype(BF16)
    hb = h_main.astype(BF16)

    slab = FF_CHUNK
    per_group = A_WIDTH // slab

    def project(lhs, group, half):
        lo = group * A_WIDTH + half * slab
        return _dot(lhs, w_ref[:, lo:lo + slab])

    cq = cq_ref[...]
    outs = (q_ref, k_ref, v_ref)
    short = []
    for part in range(3):
        for half in range(per_group):
            cols = slice(part * A_WIDTH + half * slab, part * A_WIDTH + (half + 1) * slab)
            p = project(hb_ext, part, half)
            short.append(project(hb_ext, 4 + part, half))
            a = _silu(_dwconv3_ext(p, cq[:, cols], tm))
            if part < 2:
                heads = []
                for h in range(slab // DK):
                    ah = a[:, h * DK:(h + 1) * DK]
                    ss = jnp.sum(ah * ah, axis=-1, keepdims=True)
                    nrm = lax.rsqrt(ss + EPS)
                    if part == 0:
                        nrm = nrm * (DK ** -0.5)
                    heads.append(ah * nrm)
                a = jnp.concatenate(heads, axis=1)
            outs[part][:, half * slab:(half + 1) * slab] = a.astype(BF16)

    cb = cb_ref[...]
    for half in range(per_group):
        cols = slice(half * slab, (half + 1) * slab)
        gate_ref[:, cols] = project(hb, 3, half).astype(BF16)
        bg, cg, hx = (short[part * per_group + half] for part in range(3))
        yb_ref[:, cols] = (bg[HALO:HALO + tm]
                           * _dwconv3_ext(cg * hx, cb[:, cols], tm)).astype(BF16)

    pr = pr_ref[...]
    bar = _dot_nt(wsr_ref[...], hb)
    sub = lax.broadcasted_iota(jnp.int32, bar.shape, 0)
    gbr_ref[...] = jnp.where(sub < N_DIR * H_A, _sigmoid(bar),
                             -pr[:, 0:1] * _softplus(bar + pr[:, 1:2]))


def _inproj(x, pe, mod3, row_base, row_stride, seq_len, tm, ng, wts):
    n_rows = x.shape[0]
    has_pe = pe is not None
    in_specs = _tile_specs(tm, D_MODEL, n_rows)
    args = [x, x, x]
    if has_pe:
        in_specs += _pe_specs(tm, D_MODEL, seq_len)
        args += [pe, pe, pe]
    in_specs += [_mod_spec(tm, seq_len, row_base, row_stride), _const_spec((1, D_MODEL)),
                 _const_spec((D_MODEL, GATE_END + 3 * B_WIDTH)), _const_spec((N_SMALL, D_MODEL)),
                 _const_spec((3, QKV_END)), _const_spec((3, B_WIDTH)), _const_spec((N_SMALL, 2))]
    args += [mod3, ng, wts["w_main"], wts["w_small_r"], wts["conv_qkv"], wts["conv_b"],
             wts["p_row"]]
    wide = pl.BlockSpec((tm, A_WIDTH), lambda i: (i, 0))
    out_specs = [wide, wide, wide, wide, pl.BlockSpec((N_SMALL, tm), lambda i: (0, i)), wide]
    wide_shape = jax.ShapeDtypeStruct((n_rows, A_WIDTH), BF16)
    out_shape = [wide_shape] * 4 + [jax.ShapeDtypeStruct((N_SMALL, n_rows), F32), wide_shape]
    return pl.pallas_call(
        functools.partial(_inproj_kernel, tm=tm, seq_len=seq_len, has_pe=has_pe),
        grid=(n_rows // tm,), in_specs=in_specs, out_specs=out_specs, out_shape=out_shape,
        compiler_params=_params(1), name="inproj",
    )(*args)


def _dot3_left(a, b_exact):
    a1, a2, a3 = _split3(a)
    return _dot(a1, b_exact) + (_dot(a2, b_exact) + _dot(a3, b_exact))


def _chunk_setup(q_ref, k_ref, v_ref, gbr_ref, qg_ref, kg_ref, at_ref, gl_ref,
                 m_scr, rhs_scr, *, n_sub):
    c = CHUNK
    n_beta = N_DIR * H_A
    r1 = lax.broadcasted_iota(jnp.int32, (c, c), 0)
    c1 = lax.broadcasted_iota(jnp.int32, (c, c), 1)
    tri_lo = jnp.where(r1 >= c1, 1.0, 0.0).astype(BF16)
    tri_up = jnp.where(r1 <= c1, 1.0, 0.0).astype(BF16)
    r2 = lax.broadcasted_iota(jnp.int32, (2 * c, 2 * c), 0)
    c2 = lax.broadcasted_iota(jnp.int32, (2 * c, 2 * c), 1)
    same_head = (r2 ^ c2) < c

    def col_pair(a, j):
        col = jnp.concatenate([a[:, j:j + 1], a[:, j + 1:j + 2]], axis=0)
        return jnp.broadcast_to(col, (2 * c, c))

    def row_pair(a, j):
        return jnp.concatenate([a[j:j + 1, :], a[j + 1:j + 2, :]], axis=1)

    def twice(a):
        return jnp.concatenate([a, a], axis=1)

    n = 0
    for ci in range(n_sub):
        rows_ci = slice(ci * c, (ci + 1) * c)
        gbr = gbr_ref[:, rows_ci]
        sub = lax.broadcasted_iota(jnp.int32, gbr.shape, 0)
        gc_r = jnp.where(sub < n_beta + H_A, _dot3_left(gbr, tri_up), _dot3_left(gbr, tri_lo))
        total = jnp.broadcast_to(jnp.sum(gbr, axis=1, keepdims=True), gbr.shape)
        gl_ref[ci] = jnp.exp(total)[n_beta:, :]
        pad = jnp.zeros((c - 2 * N_SMALL, c), F32)
        cols_c = jnp.concatenate([gbr, gc_r, pad], axis=0).T
        gbc = cols_c[:, 0:N_SMALL]
        gc_c = cols_c[:, N_SMALL:2 * N_SMALL]

        q = q_ref[rows_ci, :]
        k = k_ref[rows_ci, :]
        v = v_ref[rows_ci, :]
        for hp in range(H_A // 2):
            h0 = 2 * hp

            def stack(a):
                return jnp.concatenate([a[:, h0 * DK:(h0 + 1) * DK],
                                        a[:, (h0 + 1) * DK:(h0 + 2) * DK]], axis=0)

            q2b, k2b, v2b = stack(q), stack(k), stack(v)
            kk = _dot_nt(k2b, k2b)
            qk = _dot_nt(q2b, k2b)
            for d in range(N_DIR):
                jb = d * H_A + h0
                jg = n_beta + jb
                gcb = col_pair(gc_c, jg)
                beta = col_pair(gbc, jb)
                tot = jnp.concatenate([jnp.broadcast_to(total[jg:jg + 1, :], (c, c)),
                                       jnp.broadcast_to(total[jg + 1:jg + 2, :], (c, c))], axis=0)
                eg = jnp.exp(gcb)
                er = jnp.exp(tot - gcb)
                tri = (r2 >= c2) if d == 0 else (r2 <= c2)
                decay = jnp.exp(jnp.where(same_head & tri, twice(gcb) - row_pair(gc_r, jg), NEG_BIG))
                attn = qk * decay
                m_scr[n] = jnp.where(r2 != c2, (twice(beta) * kk) * decay, 0.0)
                rhs_scr[n] = jnp.concatenate([v2b * beta.astype(BF16),
                                              k2b * (beta * eg).astype(BF16)], axis=1)
                qg = q2b * eg.astype(BF16)
                kg = k2b * er.astype(BF16)
                for hh in range(2):
                    rows = slice(hh * c, (hh + 1) * c)
                    cols = slice((h0 + hh) * DK, (h0 + hh + 1) * DK)
                    qg_ref[d, rows_ci, cols] = qg[rows]
                    kg_ref[d, rows_ci, cols] = kg[rows]
                    at_ref[d, rows_ci, cols] = attn[rows, hh * c:(hh + 1) * c].astype(BF16)
                n += 1
                yield


def _chunk_solve(m_scr, rhs_scr, gsel_ref, u_ref, w_ref, *, n_sub):
    c = CHUNK
    chains = [{"ci": ci, "h0": 2 * hp, "d": d}
              for ci in range(n_sub) for hp in range(H_A // 2) for d in range(N_DIR)]
    for n, ch in enumerate(chains):
        ch["m"] = m_scr[n]
    r2 = lax.broadcasted_iota(jnp.int32, (2 * c, 2 * c), 0)
    c2 = lax.broadcasted_iota(jnp.int32, (2 * c, 2 * c), 1)
    x2 = r2 ^ c2

    sub8 = SUBLANES
    per_half = c // sub8
    lane8 = lax.broadcasted_iota(jnp.int32, (sub8, c), 1)
    row8 = lax.broadcasted_iota(jnp.int32, (sub8, c), 0)
    in_blk = [(lane8 >> 3) == i for i in range(per_half)]
    zeros8 = jnp.zeros((sub8, c), F32)
    packed = []
    for ch in chains:
        halves = []
        for half in range(2):
            pm = zeros8
            for i in range(per_half):
                blk = half * per_half + i
                pm = jnp.where(in_blk[i], ch["m"][blk * sub8:(blk + 1) * sub8,
                                                  half * c:(half + 1) * c], pm)
            halves.append(pm)
        packed.append(jnp.concatenate(halves, axis=1))
    stacked = jnp.concatenate(packed, axis=0).astype(BF16)
    yield
    spread = [_dot(stacked, gsel_ref[k]) for k in range(sub8)]
    yield
    eye8 = jnp.where((lane8 & (sub8 - 1)) == row8, 1.0, 0.0).astype(F32)
    for n, ch in enumerate(chains):
        rows_n = slice(n * sub8, (n + 1) * sub8)
        tp = jnp.concatenate([eye8, eye8], axis=1)
        order = range(sub8 - 1) if ch["d"] == 0 else range(sub8 - 1, 0, -1)
        for k in order:
            tp = tp - spread[k][rows_n] * tp[k:k + 1, :]
        row_groups = []
        for half in range(2):
            tph = tp[:, half * c:(half + 1) * c]
            for i in range(per_half):
                own = jnp.where(in_blk[i], tph, 0.0)
                row_groups.append(jnp.concatenate([own, zeros8] if half == 0 else [zeros8, own],
                                                  axis=1))
        ch["t"] = jnp.concatenate(row_groups, axis=0)
    yield

    def pick(a, b, parity):
        return jnp.concatenate([a[i * b:(i + 1) * b] for i in range(a.shape[0] // b)
                                if i % 2 == parity], axis=0)

    def weave(sel, rest, b, parity):
        blocks = []
        for i in range(2 * sel.shape[0] // b):
            src = sel if i % 2 == parity else rest
            blocks.append(src[(i // 2) * b:(i // 2 + 1) * b])
        return jnp.concatenate(blocks, axis=0)

    par = [1 - ch["d"] for ch in chains]
    for lvl in range(3, 7):
        b = 1 << lvl
        xs = []
        for ch, p in zip(chains, par):
            off = jnp.where((pick(x2, b, p) >> lvl) == 1, pick(ch["m"], b, p), 0.0)
            xs.append(_dot(off.astype(BF16), ch["t"].astype(BF16)))
        yield
        for ch, x, p in zip(chains, xs, par):
            x_full = weave(x, jnp.zeros_like(x), b, p).astype(BF16)
            t_sel = pick(ch["t"], b, p)
            t_new = t_sel - _dot(t_sel.astype(BF16), x_full)
            ch["t"] = weave(t_new, pick(ch["t"], b, 1 - p), b, p)
        yield

    for n, ch in enumerate(chains):
        uw = _dot(ch["t"].astype(BF16), rhs_scr[n])
        rows_ci = slice(ch["ci"] * c, (ch["ci"] + 1) * c)
        for hh in range(2):
            rows = slice(hh * c, (hh + 1) * c)
            cols = slice((ch["h0"] + hh) * DK, (ch["h0"] + hh + 1) * DK)
            u_ref[ch["d"], rows_ci, cols] = uw[rows, 0:DV].astype(BF16)
            w_ref[ch["d"], rows_ci, cols] = uw[rows, DV:2 * DV].astype(BF16)
        if n % 2:
            yield


def _chunk_local_kernel(q_ref, k_ref, v_ref, gbr_ref, gsel_ref,
                        u_ref, w_ref, qg_ref, kg_ref, at_ref, gl_ref,
                        m_a, rhs_a, m_b, rhs_b, *, n_sub):
    setup = functools.partial(_chunk_setup, q_ref, k_ref, v_ref, gbr_ref,
                              qg_ref, kg_ref, at_ref, gl_ref, n_sub=n_sub)
    solve = functools.partial(_chunk_solve, gsel_ref=gsel_ref, u_ref=u_ref, w_ref=w_ref,
                              n_sub=n_sub)
    j = pl.program_id(0)

    @pl.when(j == 0)
    def _():
        u_ref[...] = jnp.zeros(u_ref.shape, u_ref.dtype)
        w_ref[...] = jnp.zeros(w_ref.shape, w_ref.dtype)
        _interleave(setup(m_a, rhs_a))

    @pl.when(j % 2 == 1)
    def _():
        _interleave(solve(m_a, rhs_a), setup(m_b, rhs_b))

    @pl.when((j > 0) & (j % 2 == 0))
    def _():
        _interleave(solve(m_b, rhs_b), setup(m_a, rhs_a))


def _chunk_local(q, k, v, gbr, n_sub=4):
    n_rows = q.shape[0]
    n_chunks = n_rows // CHUNK
    rows = n_sub * CHUNK
    n_blocks = n_chunks // n_sub
    n_sys = n_sub * (H_A // 2) * N_DIR
    cur = lambda j: jnp.minimum(j, n_blocks - 1)
    prev = lambda j: jnp.maximum(j - 1, 0)
    wide = pl.BlockSpec((rows, A_WIDTH), lambda j: (cur(j), 0))
    out_cur = pl.BlockSpec((N_DIR, rows, A_WIDTH), lambda j: (0, cur(j), 0))
    out_prev = pl.BlockSpec((N_DIR, rows, A_WIDTH), lambda j: (0, prev(j), 0))
    big = lambda dt: jax.ShapeDtypeStruct((N_DIR, n_rows, A_WIDTH), dt)
    idx = jnp.arange(2 * CHUNK)
    gsel = ((idx[None, :, None] // SUBLANES == idx[None, None, :] // SUBLANES)
            & (idx[None, :, None] % SUBLANES == jnp.arange(SUBLANES)[:, None, None])).astype(BF16)
    return pl.pallas_call(
        functools.partial(_chunk_local_kernel, n_sub=n_sub),
        grid=(n_blocks + 1,),
        in_specs=[wide, wide, wide,
                  pl.BlockSpec((N_SMALL, rows), lambda j: (0, cur(j))),
                  _const_spec((SUBLANES, 2 * CHUNK, 2 * CHUNK))],
        out_specs=[out_prev, out_prev, out_cur, out_cur, out_cur,
                   pl.BlockSpec((n_sub, N_DIR * H_A, CHUNK), lambda j: (cur(j), 0, 0))],
        out_shape=[big(BF16)] * 5 + [jax.ShapeDtypeStruct((n_chunks, N_DIR * H_A, CHUNK), F32)],
        scratch_shapes=[pltpu.VMEM((n_sys, 2 * CHUNK, 2 * CHUNK), F32),
                        pltpu.VMEM((n_sys, 2 * CHUNK, 2 * CHUNK), BF16),
                        pltpu.VMEM((n_sys, 2 * CHUNK, 2 * CHUNK), F32),
                        pltpu.VMEM((n_sys, 2 * CHUNK, 2 * CHUNK), BF16)],
        compiler_params=_params(1, ("arbitrary",)), name="chunk_local",
    )(q, k, v, gbr, gsel)


def _scan_kernel(*refs, n_chunks, zero_init, write_final):
    fwd = refs[0:6]
    bwd = refs[6:12]
    refs = refs[12:]
    if not zero_init:
        s0_ref = refs[0]
        refs = refs[1:]
    of_ref, ob_ref = refs[0:2]
    refs = refs[2:]
    if write_final:
        sfin_ref = refs[0]
        refs = refs[1:]
    s_scr = refs[0]
    i = pl.program_id(1)

    @pl.when(i == 0)
    def _():
        if zero_init:
            s_scr[...] = jnp.zeros(s_scr.shape, F32)
        else:
            s_scr[...] = s0_ref[...]

    problems = [(s, d, h) for s in range(SEQ_GROUP) for d in range(N_DIR) for h in range(H_A)]
    c = CHUNK
    stage1 = []
    for s, d, h in problems:
        u_ref, w_ref, qg_ref, _, _, _ = fwd if d == 0 else bwd
        cols = slice(h * DV, (h + 1) * DV)
        sb = s_scr[s, d, h].astype(BF16)
        lhs = jnp.concatenate([w_ref[0, s, :, cols], qg_ref[0, s, :, cols]], axis=0)
        stage1.append(_dot(lhs, sb))
    for (s, d, h), ws_qs in zip(problems, stage1):
        u_ref, _, _, kg_ref, at_ref, gl_ref = fwd if d == 0 else bwd
        o_ref = of_ref if d == 0 else ob_ref
        cols = slice(h * DV, (h + 1) * DV)
        vb = (u_ref[0, s, :, cols].astype(F32) - ws_qs[0:c]).astype(BF16)
        o_ref[s, :, cols] = (ws_qs[c:2 * c] + _dot(at_ref[0, s, :, cols], vb)).astype(BF16)
        gl = gl_ref[s, 0, d * H_A + h:d * H_A + h + 1, :]
        s_scr[s, d, h] = s_scr[s, d, h] * gl + _dot_tn(kg_ref[0, s, :, cols], vb)

    if write_final:
        @pl.when(i == n_chunks - 1)
        def _():
            sfin_ref[...] = s_scr[...]


def _scan(local, s0, n_seq, seq_len, write_final):
    u, w, qg, kg, at, gl = local
    n_chunks = seq_len // CHUNK
    zero_init = s0 is None
    shp4 = (N_DIR, n_seq, seq_len, A_WIDTH)
    arrs = [a.reshape(shp4) for a in (u, w, qg, kg, at)]
    gl4 = gl.reshape(n_seq, n_chunks, N_DIR * H_A, CHUNK)
    sg = SEQ_GROUP

    def specs(d, chunk_of):
        big = pl.BlockSpec((1, sg, CHUNK, A_WIDTH), lambda g, i: (d, g, chunk_of(i), 0))
        return [big] * 5 + [pl.BlockSpec((sg, 1, N_DIR * H_A, CHUNK),
                                         lambda g, i: (g, chunk_of(i), 0, 0))]

    rev = lambda i: n_chunks - 1 - i
    in_specs = specs(0, lambda i: i) + specs(1, rev)
    args = arrs + [gl4] + arrs + [gl4]
    state_spec = pl.BlockSpec((sg, N_DIR, H_A, DK, DV), lambda g, i: (g, 0, 0, 0, 0))
    if not zero_init:
        in_specs.append(state_spec)
        args.append(s0)
    o_shape = jax.ShapeDtypeStruct((n_seq, seq_len, A_WIDTH), BF16)
    out_specs = [pl.BlockSpec((sg, CHUNK, A_WIDTH), lambda g, i: (g, i, 0)),
                 pl.BlockSpec((sg, CHUNK, A_WIDTH), lambda g, i: (g, rev(i), 0))]
    out_shape = [o_shape, o_shape]
    if write_final:
        out_specs.append(state_spec)
        out_shape.append(jax.ShapeDtypeStruct((n_seq, N_DIR, H_A, DK, DV), F32))
    return pl.pallas_call(
        functools.partial(_scan_kernel, n_chunks=n_chunks, zero_init=zero_init,
                          write_final=write_final),
        grid=(n_seq // sg, n_chunks), in_specs=in_specs, out_specs=out_specs, out_shape=out_shape,
        scratch_shapes=[pltpu.VMEM((sg, N_DIR, H_A, DK, DV), F32)],
        compiler_params=_params(2, ("parallel", "arbitrary")), name="delta_scan",
    )(*args)


def _delta_mixer_rows(o_f, o_b, gate, yb, og):
    o = o_f + o_b
    sg = _silu(gate)
    parts = []
    for h in range(H_A):
        cols = slice(h * DV, (h + 1) * DV)
        oh = o[:, cols]
        ms = jnp.mean(oh * oh, axis=-1, keepdims=True)
        parts.append((oh * lax.rsqrt(ms + EPS)) * og * sg[:, cols])
    return jnp.concatenate(parts + [yb], axis=1)


def _ffn_kernel(*refs, tm, seq_len, final_norm, delta, has_pe):
    xp, xm, xn = refs[:3]
    refs = refs[3:]
    if delta:
        if has_pe:
            pe_refs = refs[:3]
            refs = refs[3:]
        mix_refs = refs[:12]
        og_ref, wout_ref = refs[12:14]
        refs = refs[14:]
    mod_ref, ng_ref, wg_ref, wu_ref, wd_ref, cf_ref = refs[:6]
    refs = refs[6:]
    if final_norm:
        fg_ref = refs[0]
        refs = refs[1:]
    out_ref = refs[0]
    vp, vn = _halo_valid(tm, seq_len)
    mod = mod_ref[0]
    shift = mod[:, 3 * D_MODEL:4 * D_MODEL]
    scale = mod[:, 4 * D_MODEL:5 * D_MODEL]
    gate2 = mod[:, 5 * D_MODEL:6 * D_MODEL]
    g = ng_ref[...]
    if delta:
        def mixer_rows(which):
            vals = []
            for a in range(4):
                r = mix_refs[3 * a + which][...].astype(F32)
                if which == 0:
                    r = r[HALO:2 * HALO]
                elif which == 2:
                    r = r[0:HALO]
                vals.append(r)
            return _delta_mixer_rows(*vals, og_ref[...])

        a_ext = jnp.concatenate([mixer_rows(0), mixer_rows(1), mixer_rows(2)], axis=0)
        x_ext = jnp.concatenate([xp[...], xm[...], xn[...]], axis=0)
        if has_pe:
            x_ext = x_ext + jnp.concatenate([r[...] for r in pe_refs], axis=0)
        gate1 = mod[:, 2 * D_MODEL:3 * D_MODEL]
        x_ext = x_ext + gate1 * _dot(a_ext.astype(BF16), wout_ref[...])
        x = x_ext[HALO:HALO + tm]
        h_all = _norm_mod(x_ext, g, scale, shift)
        h_main = h_all[HALO:HALO + tm]
        h_ext = jnp.concatenate([h_all[0:HALO] * vp, h_main, h_all[HALO + tm:] * vn], axis=0)
    else:
        x = xm[...]
        h_main = _norm_mod(x, g, scale, shift)
        h_ext = jnp.concatenate([_norm_mod(xp[...], g, scale, shift) * vp, h_main,
                                 _norm_mod(xn[...], g, scale, shift) * vn], axis=0)
    hb_ext = h_ext.astype(BF16)
    hb = h_main.astype(BF16)
    edge_masks = _seq_edge_masks(tm, seq_len, FF_CHUNK)
    acts = []
    for j in range(N_FF_CHUNKS):
        cols = slice(j * FF_CHUNK, (j + 1) * FF_CHUNK)
        u = _dwconv3_ext(_dot(hb_ext, wg_ref[:, cols]), cf_ref[:, cols], tm, edge_masks)
        acts.append((_silu(u) * _dot(hb, wu_ref[:, cols])).astype(BF16))
    y = x + gate2 * _dot(jnp.concatenate(acts, axis=1), wd_ref[...])
    if final_norm:
        ms = jnp.mean(y * y, axis=-1, keepdims=True)
        y = (y * lax.rsqrt(ms + EPS)) * fg_ref[...]
    out_ref[...] = y


def _ffn(x, mod3, row_base, row_stride, seq_len, tm, ng, wts, layer, final_g, delta=None):
    n_rows = x.shape[0]
    final_norm = final_g is not None
    of_layer = lambda *shape: pl.BlockSpec((None,) + shape, lambda i: (layer, 0, 0))
    in_specs = _tile_specs(tm, D_MODEL, n_rows)
    args = [x, x, x]
    has_pe = False
    if delta is not None:
        pe, o_f, o_b, gate, yb, og, w_out = delta
        has_pe = pe is not None
        if has_pe:
            in_specs += _pe_specs(tm, D_MODEL, seq_len)
            args += [pe, pe, pe]
        for arr in (o_f, o_b, gate, yb):
            in_specs += _tile_specs(tm, A_WIDTH, n_rows, halo=2 * HALO)
            args += [arr, arr, arr]
        in_specs += [_const_spec((1, DV)), _const_spec((D_MODEL, D_MODEL))]
        args += [og, w_out]
    in_specs += [_mod_spec(tm, seq_len, row_base, row_stride), _const_spec((1, D_MODEL)),
                 of_layer(D_MODEL, D_FF), of_layer(D_MODEL, D_FF), of_layer(D_FF, D_MODEL),
                 of_layer(3, D_FF)]
    args += [mod3, ng, wts["w_gate"], wts["w_up"], wts["w_down"], wts["conv"]]
    if final_norm:
        in_specs.append(_const_spec((1, D_MODEL)))
        args.append(final_g)
    return pl.pallas_call(
        functools.partial(_ffn_kernel, tm=tm, seq_len=seq_len, final_norm=final_norm,
                          delta=delta is not None, has_pe=has_pe),
        grid=(n_rows // tm,), in_specs=in_specs,
        out_specs=pl.BlockSpec((tm, D_MODEL), lambda i: (i, 0)),
        out_shape=jax.ShapeDtypeStruct((n_rows, D_MODEL), F32),
        compiler_params=_params(1), name="conv_ffn",
    )(*args)


def _pool_group(h_ext, n_out, pos, seq_len, wp_ref, gi):
    win = POOL_WINDOWS[gi]
    cols = slice(gi * POOL_GROUP, (gi + 1) * POOL_GROUP)
    s = h_ext[:, cols]
    width = 1
    while width < win:
        s = s + _shift_rows(s, width)
        width *= 2
    start = HALO - win // 2
    if start:
        s = _shift_rows(s, start)
    s = s[0:n_out]
    lo = jnp.maximum(pos - win // 2, 0)
    hi = jnp.minimum(pos - win // 2 + win, seq_len)
    count = jnp.maximum(hi - lo, 1).astype(F32)
    mixed = s / jnp.concatenate([count, count], axis=1) - h_ext[HALO:HALO + n_out, cols]
    return _dot(mixed.astype(BF16), wp_ref[gi])


def _pool_stage(xp, xm, xn, mod_ref, g1_ref, wp_ref, ps_ref, g2_ref, hb_scr, xr_scr,
                *, tm, seq_len, n_tiles):
    wide = 2 * HALO
    tile = jnp.minimum(pl.program_id(0), n_tiles - 1)
    first = tile * tm
    vp = jnp.where(first % seq_len != 0, 1.0, 0.0).astype(F32)
    vn = jnp.where((first + tm) % seq_len != 0, 1.0, 0.0).astype(F32)
    mod = mod_ref[0]
    shift1, scale1, gate1 = (mod[:, k * D_MODEL:(k + 1) * D_MODEL] for k in range(3))
    shift2, scale2 = (mod[:, k * D_MODEL:(k + 1) * D_MODEL] for k in (3, 4))
    g1 = g1_ref[...]
    g2 = g2_ref[...]
    lanes = POOL_GROUP // 2
    if tm <= seq_len:
        n_out = tm + 2 * HALO
        segments = [(jnp.concatenate([_norm_mod(xp[...], g1, scale1, shift1) * vp,
                                      _norm_mod(xm[...], g1, scale1, shift1),
                                      _norm_mod(xn[...], g1, scale1, shift1) * vn], axis=0),
                     (first - HALO + lax.broadcasted_iota(jnp.int32, (n_out, lanes), 0)) % seq_len)]
        x_res = jnp.concatenate([xp[HALO:, :], xm[...], xn[0:HALO, :]], axis=0)
    else:
        n_out = seq_len
        x_res = xm[...]
        h1 = _norm_mod(x_res, g1, scale1, shift1)
        zeros = jnp.zeros((HALO, D_MODEL), F32)
        pos = lax.broadcasted_iota(jnp.int32, (seq_len, lanes), 0)
        segments = [(jnp.concatenate([zeros, h1[s * seq_len:(s + 1) * seq_len], zeros], axis=0), pos)
                    for s in range(tm // seq_len)]
    yield
    ys = []
    for h_ext, pos in segments:
        groups = []
        for gi in range(N_POOL_GROUPS):
            groups.append(_pool_group(h_ext, n_out, pos, seq_len, wp_ref, gi))
            yield
        ys.append(jnp.concatenate(groups, axis=1))
    x3 = x_res + gate1 * (jnp.concatenate(ys, axis=0) * ps_ref[...])
    h2 = _norm_mod(x3, g2, scale2, shift2)
    if tm <= seq_len:
        zeros = jnp.zeros((HALO, D_MODEL), F32)
        hb_scr[...] = jnp.concatenate([zeros, h2[0:HALO] * vp, h2[HALO:HALO + tm],
                                       h2[HALO + tm:] * vn, zeros], axis=0).astype(BF16)
        xr_scr[...] = x3[HALO:HALO + tm]
    else:
        zeros_w = jnp.zeros((wide, D_MODEL), F32)
        hb_scr[...] = jnp.concatenate([zeros_w, h2, zeros_w], axis=0).astype(BF16)
        xr_scr[...] = x3
    yield


def _ffn_stage(hb_scr, xr_scr, mod_ref, wg_ref, wu_ref, wd_ref, cf_ref, fg_ref, out_ref,
               *, tm, seq_len):
    wide = 2 * HALO
    gate2 = mod_ref[0][:, 5 * D_MODEL:6 * D_MODEL]
    hb_ext = hb_scr[...]
    hb = hb_scr[wide:wide + tm, :]
    edge_masks = _seq_edge_masks(tm, seq_len, FF_CHUNK)
    acts = []
    for j in range(N_FF_CHUNKS):
        cols = slice(j * FF_CHUNK, (j + 1) * FF_CHUNK)
        u = _dwconv3_ext(_dot(hb_ext, wg_ref[:, cols]), cf_ref[:, cols], tm, edge_masks, wide)
        acts.append((_silu(u) * _dot(hb, wu_ref[:, cols])).astype(BF16))
        yield
    y = xr_scr[...] + gate2 * _dot(jnp.concatenate(acts, axis=1), wd_ref[...])
    ms = jnp.mean(y * y, axis=-1, keepdims=True)
    out_ref[...] = (y * lax.rsqrt(ms + EPS)) * fg_ref[...]
    yield


def _interleave(*stages):
    live = list(stages)
    while live:
        for stage in list(live):
            if next(stage, StopIteration) is StopIteration:
                live.remove(stage)


def _pool_ffn_kernel(xp, xm, xn, mod_cur, mod_prev, g1_ref, wp_ref, ps_ref, g2_ref,
                     wg_ref, wu_ref, wd_ref, cf_ref, fg_ref, out_ref,
                     hb_a, xr_a, hb_b, xr_b, *, tm, seq_len, n_tiles):
    stage = functools.partial(_pool_stage, xp, xm, xn, mod_cur, g1_ref, wp_ref, ps_ref, g2_ref,
                              tm=tm, seq_len=seq_len, n_tiles=n_tiles)
    finish = functools.partial(_ffn_stage, mod_ref=mod_prev, wg_ref=wg_ref, wu_ref=wu_ref,
                               wd_ref=wd_ref, cf_ref=cf_ref, fg_ref=fg_ref, out_ref=out_ref,
                               tm=tm, seq_len=seq_len)
    j = pl.program_id(0)

    @pl.when(j == 0)
    def _():
        out_ref[...] = jnp.zeros(out_ref.shape, out_ref.dtype)
        _interleave(stage(hb_a, xr_a))

    @pl.when(j % 2 == 1)
    def _():
        _interleave(finish(hb_a, xr_a), stage(hb_b, xr_b))

    @pl.when((j > 0) & (j % 2 == 0))
    def _():
        _interleave(finish(hb_b, xr_b), stage(hb_a, xr_a))


def _pool_ffn(x, mod3, row_base, row_stride, seq_len, tm, g1, w_pool, pool_scale, g2, wts, layer,
              final_g):
    n_rows = x.shape[0]
    n_tiles = n_rows // tm
    wide = 2 * HALO
    per = tm // wide
    last = n_rows // wide - 1
    cur = lambda j: jnp.minimum(j, n_tiles - 1)
    prev = lambda j: jnp.maximum(j - 1, 0)
    mod_row = lambda t: row_base + ((t * tm) // seq_len) * row_stride
    of_layer = lambda *shape: pl.BlockSpec((None,) + shape, lambda j: (layer, 0, 0))
    in_specs = [
        pl.BlockSpec((wide, D_MODEL), lambda j: (jnp.maximum(cur(j) * per - 1, 0), 0)),
        pl.BlockSpec((tm, D_MODEL), lambda j: (cur(j), 0)),
        pl.BlockSpec((wide, D_MODEL), lambda j: (jnp.minimum((cur(j) + 1) * per, last), 0)),
        pl.BlockSpec((1, 1, N_MOD * D_MODEL), lambda j: (mod_row(cur(j)), 0, 0)),
        pl.BlockSpec((1, 1, N_MOD * D_MODEL), lambda j: (mod_row(prev(j)), 0, 0)),
        _const_spec((1, D_MODEL)), _const_spec((N_POOL_GROUPS, POOL_GROUP, POOL_GROUP)),
        _const_spec((1, D_MODEL)), _const_spec((1, D_MODEL)),
        of_layer(D_MODEL, D_FF), of_layer(D_MODEL, D_FF), of_layer(D_FF, D_MODEL),
        of_layer(3, D_FF), _const_spec((1, D_MODEL))]
    return pl.pallas_call(
        functools.partial(_pool_ffn_kernel, tm=tm, seq_len=seq_len, n_tiles=n_tiles),
        grid=(n_tiles + 1,), in_specs=in_specs,
        out_specs=pl.BlockSpec((tm, D_MODEL), lambda j: (prev(j), 0)),
        out_shape=jax.ShapeDtypeStruct((n_rows, D_MODEL), F32),
        scratch_shapes=[pltpu.VMEM((tm + 2 * wide, D_MODEL), BF16), pltpu.VMEM((tm, D_MODEL), F32),
                        pltpu.VMEM((tm + 2 * wide, D_MODEL), BF16), pltpu.VMEM((tm, D_MODEL), F32)],
        compiler_params=_params(1, ("arbitrary",)), name="pool_ffn",
    )(x, x, x, mod3, mod3, g1, w_pool, pool_scale, g2, wts["w_gate"], wts["w_up"], wts["w_down"],
      wts["conv"], final_g)


def _sincos_2d(rows, cols, d):
    quarter = d // 4
    omega = 1.0 / (10000.0 ** (jnp.arange(quarter, dtype=F32) / quarter))
    er = jnp.arange(rows, dtype=F32)[:, None] * omega[None, :]
    ec = jnp.arange(cols, dtype=F32)[:, None] * omega[None, :]
    er = jnp.concatenate([jnp.sin(er), jnp.cos(er)], axis=-1)
    ec = jnp.concatenate([jnp.sin(ec), jnp.cos(ec)], axis=-1)
    pe = jnp.concatenate([jnp.broadcast_to(er[:, None, :], (rows, cols, d // 2)),
                          jnp.broadcast_to(ec[None, :, :], (rows, cols, d // 2))], axis=-1)
    return pe.reshape(rows * cols, d)


def _even_layer_weights(w_in, conv_qkv, a_log, dt_bias, conv_b, w_out):
    small = w_in[:, GATE_END:ALPHA_END]
    zeros = jnp.zeros((N_DIR * H_A,), F32)
    ea = jnp.concatenate([zeros, jnp.exp(a_log.astype(F32)).reshape(-1)])
    dtb = jnp.concatenate([zeros, dt_bias.astype(F32).reshape(-1)])
    return {
        "w_main": jnp.concatenate([w_in[:, :GATE_END], w_in[:, ALPHA_END:IN_AB]],
                                  axis=1).astype(BF16),
        "w_small_r": small.T.astype(BF16),
        "conv_qkv": conv_qkv, "conv_b": conv_b,
        "p_row": jnp.stack([ea, dtb], axis=1),
        "w_out": w_out.astype(BF16),
    }


def _ffn_weights(w_gate, w_up, conv, w_down):
    return {"w_gate": w_gate.astype(BF16), "w_up": w_up.astype(BF16),
            "w_down": w_down.astype(BF16), "conv": conv}


def _trunk(x, pe, mod, row_base, row_stride, n_seq, seq_len, tm, tm_ffn, s0, write_final, p):
    row = lambda a: a.reshape(1, -1)
    finals = None
    depth = mod.shape[0]
    for layer in range(depth):
        mod3 = mod[layer][:, None, :]
        place = (mod3, row_base, row_stride, seq_len, tm)
        place_ffn = (mod3, row_base, row_stride, seq_len, tm_ffn // 2)
        if layer % 2 == 0:
            e = layer // 2
            wts = p["even"][e]
            q, k, v, gate, gbr, yb = _inproj(x, pe, *place, row(p["norm_mix_g"][layer]), wts)
            local = _chunk_local(q, k, v, gbr)
            res = _scan(local, None if s0 is None else s0[:, e], n_seq, seq_len, write_final)
            o_f = res[0].reshape(-1, A_WIDTH)
            o_b = res[1].reshape(-1, A_WIDTH)
            if write_final:
                finals = res[2]
            delta = (pe, o_f, o_b, gate, yb, row(p["o_norm_g"][e]), wts["w_out"])
            x = _ffn(x, *place_ffn, row(p["norm_ffn_g"][layer]), p["ffn"], layer, None, delta)
        else:
            o = layer // 2
            assert layer == depth - 1
            x = _pool_ffn(x, *place_ffn, row(p["norm_mix_g"][layer]), p["w_pool"][o],
                          row(p["pool_scale"][o]), row(p["norm_ffn_g"][layer]), p["ffn"], layer,
                          row(p["final_norm_g"]))
    return x, finals


def kernel(x_prompt, x_sample, state_delta, c, c_ctx, norm_mix_g, norm_ffn_g, w_ada, b_ada,
           w_in_ab, conv_qkv, a_log, dt_bias, o_norm_g, conv_b, w_out_ab, w_pool, pool_scale,
           w_ffn_gate, w_ffn_up, ffn_conv, w_ffn_down, final_norm_g):
    batch, seq, d = x_prompt.shape
    dec_batch, dec_seq, _ = x_sample.shape
    depth = w_ada.shape[0]
    n_even = w_in_ab.shape[0]
    assert d == D_MODEL and depth == 2 and n_even == 1

    p = {
        "norm_mix_g": norm_mix_g, "norm_ffn_g": norm_ffn_g, "o_norm_g": o_norm_g,
        "pool_scale": pool_scale, "final_norm_g": final_norm_g,
        "w_pool": w_pool.astype(BF16),
        "even": [_even_layer_weights(w_in_ab[e], conv_qkv[e], a_log[e], dt_bias[e], conv_b[e],
                                     w_out_ab[e]) for e in range(n_even)],
        "ffn": _ffn_weights(w_ffn_gate, w_ffn_up, ffn_conv, w_ffn_down),
    }

    n_cond = 1 + dec_batch
    pad = (-n_cond) % HALO
    cond = jnp.concatenate([c_ctx[None, :], c, jnp.zeros((pad, d), F32)], axis=0)
    mod = _modulation(cond, w_ada, b_ada)

    y_prompt, finals = _trunk(x_prompt.reshape(batch * seq, d), None, mod, 0, 0, batch, seq,
                              min(seq, 256), 1024, None, True, p)
    pe = _sincos_2d(dec_seq // GRID_W, GRID_W, d)
    y_sample, _ = _trunk(x_sample.reshape(dec_batch * dec_seq, d), pe, mod, 1, 1, dec_batch,
                         dec_seq, 512, 1024, state_delta, False, p)
    return (y_prompt.reshape(batch, seq, d), y_sample.reshape(dec_batch, dec_seq, d),
            finals[:, None])
```
